```python
import math
import jax, jax.numpy as jnp
from jax import lax
import numpy as np

D_MODEL = 4096
BATCH = 2
SEQ = 4096
DEPTH = 1
DEC_BATCH = 8
DEC_SEQ = 64
PAST_LEN = 1024

CHUNK = 64
QBLK = 128
NORM_EPS = 1e-6
ROPE_THETA = 500000.0
M_HEADS = 8
M_QK = 128
M_V = 256
M_WIDTH = M_HEADS * M_V
A_HEADS = 16
A_KV_HEADS = 4
A_GROUP = A_HEADS // A_KV_HEADS
A_HD = 128
A_WIDTH = A_HEADS * A_HD
A_ROT = A_HD // 4
IDX_HEADS = 16
IDX_DIM = 64
IDX_ROT = IDX_DIM // 4
TOPK_MAX = 256
D_FF = 11008
CONV_W = 3
SPLIT_SIZES = (M_HEADS * M_QK, M_HEADS * M_QK, M_WIDTH, M_WIDTH, M_HEADS, M_HEADS,
               A_HEADS * A_HD, A_KV_HEADS * A_HD, A_KV_HEADS * A_HD,
               IDX_HEADS * IDX_DIM, IDX_DIM, IDX_HEADS, D_MODEL, D_MODEL)
D_IN = sum(SPLIT_SIZES)

kernel_name = 'hybrid_mlstm_dsa_streaming_step'


def rmsnorm(x, g):
    xf = x.astype(jnp.float32)
    y = xf * lax.rsqrt(jnp.mean(xf * xf, axis=-1, keepdims=True) + NORM_EPS)
    return (y * g.astype(jnp.float32)).astype(x.dtype)


def rope_partial(x, pos, rot):
    half = rot // 2
    inv_freq = jnp.exp(jnp.arange(half, dtype=jnp.float32) * (-2.0 * math.log(ROPE_THETA) / rot))
    ang = pos.astype(jnp.float32)[:, None] * inv_freq[None, :]
    cos = jnp.cos(ang)[:, None, :]
    sin = jnp.sin(ang)[:, None, :]
    xf = x.astype(jnp.float32)
    x1 = xf[..., :half]
    x2 = xf[..., half:rot]
    out = jnp.concatenate([x1 * cos - x2 * sin, x2 * cos + x1 * sin, xf[..., rot:]], axis=-1)
    return out.astype(x.dtype)


def mlstm_chunk(carry, xs):
    C, n, m = carry
    q, k, v, ig, lf = xs
    L = q.shape[2]
    b = jnp.cumsum(lf, axis=-1)
    causal = jnp.tril(jnp.ones((L, L), dtype=bool))
    logw = jnp.where(causal, b[..., :, None] - b[..., None, :] + ig[..., None, :], -jnp.inf)
    inter = b + m[..., None]
    m_t = jnp.maximum(inter, jnp.max(logw, axis=-1))
    a = jnp.exp(inter - m_t)
    s = jnp.einsum('bhtd,bhsd->bhts', q, k) * jnp.exp(logw - m_t[..., None])
    num = a[..., None] * jnp.einsum('bhvd,bhtd->bhtv', C, q) + jnp.einsum('bhts,bhsv->bhtv', s, v)
    den = a * jnp.einsum('bhd,bhtd->bht', n, q) + jnp.sum(s, axis=-1)
    h = num / jnp.maximum(jnp.abs(den), jnp.exp(-m_t))[..., None]
    m_new = m_t[..., -1]
    g = jnp.exp(b[..., -1:] - b + ig - m_new[..., None])
    decay = jnp.exp(b[..., -1] + m - m_new)
    C_new = decay[..., None, None] * C + jnp.einsum('bhsv,bhsd->bhvd', v * g[..., None], k)
    n_new = decay[..., None] * n + jnp.einsum('bhs,bhsd->bhd', g, k)
    return (C_new, n_new, m_new), h


def mlstm_sequence(q, k, v, ig, lf, C0, n0, m0):
    B, T, H = ig.shape
    clen = min(CHUNK, T)
    nc = T // clen

    def to_chunks(a):
        a = a.reshape((B, nc, clen, H) + a.shape[3:])
        return jnp.swapaxes(jnp.swapaxes(a, 0, 1), 2, 3)

    xs = (to_chunks(q), to_chunks(k), to_chunks(v), to_chunks(ig), to_chunks(lf))
    (C, n, m), h = lax.scan(mlstm_chunk, (C0, n0, m0), xs)
    h = jnp.swapaxes(jnp.swapaxes(h, 2, 3), 0, 1).reshape(B, T, H, M_V)
    return h, C, n, m


def attend_block(q, iq, iw, k_all, v_all, ik_all, vis, topk):
    f32 = jnp.float32
    B, T = q.shape[:2]
    isc = jnp.einsum('bthd,bsd->bths', iq.astype(f32), ik_all.astype(f32)) * IDX_DIM ** -0.5
    score = jnp.einsum('bths,bth->bts', jax.nn.relu(isc), iw.astype(f32)) * IDX_HEADS ** -0.5
    score = jnp.where(vis[None], score, -jnp.inf)
    vals, idx = lax.top_k(score, topk)
    valid = jnp.isfinite(vals)
    bidx = jnp.arange(B)[:, None, None]
    k_sel = k_all[bidx, idx].astype(f32)
    v_sel = v_all[bidx, idx].astype(f32)
    qg = q.astype(f32).reshape(B, T, A_KV_HEADS, A_GROUP, A_HD)
    s = jnp.einsum('btkgd,btskd->btkgs', qg, k_sel) * A_HD ** -0.5
    s = jnp.where(valid[:, :, None, None, :], s, -jnp.inf)
    p = jax.nn.softmax(s, axis=-1)
    o = jnp.einsum('btkgs,btskd->btkgd', p, v_sel)
    return o.reshape(B, T, A_WIDTH)


def sparse_attention(q, iq, iw, k_all, v_all, ik_all, past_len):
    B, T = q.shape[:2]
    L = k_all.shape[1]
    topk = min(TOPK_MAX, L // 4)
    key_chunk = jnp.arange(L) // CHUNK

    def block(args):
        qb, iqb, iwb, start = args
        qpos = past_len + start + jnp.arange(qb.shape[1])
        vis = key_chunk[None, :] <= (qpos // CHUNK)[:, None]
        return attend_block(qb, iqb, iwb, k_all, v_all, ik_all, vis, topk)

    qblk = QBLK if T % QBLK == 0 else T
    nb = T // qblk
    if nb == 1:
        return block((q, iq, iw, 0))

    def split(a):
        return jnp.moveaxis(a.reshape((B, nb, qblk) + a.shape[2:]), 1, 0)

    starts = jnp.arange(nb) * qblk
    out = lax.map(block, (split(q), split(iq), split(iw), starts))
    return jnp.moveaxis(out, 0, 1).reshape(B, T, A_WIDTH)


def layer(x, pos, past_k, past_v, past_ik, C0, n0, m0, conv0,
          w_in, b_igate, b_fgate, g_mhnorm, w_proj_m, w_proj_a, w_out,
          g_pre_mix, g_post_mix, g_pre_ffn, g_post_ffn, w_up, conv_w, conv_b, w_down):
    f32 = jnp.float32
    B, T, _ = x.shape
    P = past_k.shape[1]
    hn = rmsnorm(x, g_pre_mix)
    offsets = [int(o) for o in np.cumsum(SPLIT_SIZES)[:-1]]
    (mq, mk, mv, mo, mi, mf, aq, ak, av, iq, ik, iw, gm, ga) = jnp.split(hn @ w_in, offsets, axis=-1)

    mq = mq.reshape(B, T, M_HEADS, M_QK).astype(f32)
    mk = mk.reshape(B, T, M_HEADS, M_QK).astype(f32) * M_QK ** -0.5
    mv = mv.reshape(B, T, M_HEADS, M_V).astype(f32)
    ig = mi.astype(f32) + b_igate.astype(f32)
    lf = jax.nn.log_sigmoid(mf.astype(f32) + b_fgate.astype(f32))
    hm, C, n, m = mlstm_sequence(mq, mk, mv, ig, lf, C0.astype(f32), n0.astype(f32), m0.astype(f32))
    hm = rmsnorm(hm, g_mhnorm) * jax.nn.sigmoid(mo.reshape(B, T, M_HEADS, M_V).astype(f32))
    hm = hm.reshape(B, T, M_WIDTH).astype(x.dtype)

    aq = rope_partial(aq.reshape(B, T, A_HEADS, A_HD), pos, A_ROT)
    ak = rope_partial(ak.reshape(B, T, A_KV_HEADS, A_HD), pos, A_ROT)
    av = av.reshape(B, T, A_KV_HEADS, A_HD)
    iq = rope_partial(iq.reshape(B, T, IDX_HEADS, IDX_DIM), pos, IDX_ROT)
    ik = rope_partial(ik.reshape(B, T, 1, IDX_DIM), pos, IDX_ROT)[:, :, 0]
    k_all = jnp.concatenate([past_k, ak], axis=1)
    v_all = jnp.concatenate([past_v, av], axis=1)
    ik_all = jnp.concatenate([past_ik, ik], axis=1)
    ha = sparse_attention(aq, iq, iw, k_all, v_all, ik_all, P).astype(x.dtype)

    mix = jax.nn.sigmoid(gm) * (hm @ w_proj_m) + jax.nn.sigmoid(ga) * (ha @ w_proj_a)
    x = x + rmsnorm(mix @ w_out, g_post_mix)

    hf = rmsnorm(x, g_pre_ffn)
    gate, lin = jnp.split(hf @ w_up, 2, axis=-1)
    gpad = jnp.concatenate([conv0.astype(gate.dtype), gate], axis=1)
    gc = sum(conv_w[j] * gpad[:, j:j + T] for j in range(CONV_W)) + conv_b
    y = (jax.nn.gelu(gc, approximate=True) * lin) @ w_down
    x = x + rmsnorm(y, g_post_ffn)
    new_conv = gpad[:, T:]
    return x, ak, av, ik, C, n, m, new_conv


def setup_inputs(seed: int = 0) -> dict:
    key = jax.random.key(seed)
    ks = jax.random.split(key, 24)

    def nrm(k, shape, scale=1.0):
        return jax.random.normal(k, shape, dtype=jnp.float32) * scale

    return {
        'x_prompt': nrm(ks[0], (BATCH, SEQ, D_MODEL)),
        'x_sample': nrm(ks[1], (DEC_BATCH, DEC_SEQ, D_MODEL)),
        'cache_k': nrm(ks[2], (DEPTH, DEC_BATCH, PAST_LEN, A_KV_HEADS, A_HD)),
        'cache_v': nrm(ks[3], (DEPTH, DEC_BATCH, PAST_LEN, A_KV_HEADS, A_HD)),
        'cache_idx_k': nrm(ks[4], (DEPTH, DEC_BATCH, PAST_LEN, IDX_DIM)),
        'state_C': nrm(ks[5], (DEPTH, DEC_BATCH, M_HEADS, M_V, M_QK), 0.3),
        'state_n': nrm(ks[6], (DEPTH, DEC_BATCH, M_HEADS, M_QK), 0.3),
        'state_m': nrm(ks[7], (DEPTH, DEC_BATCH, M_HEADS)),
        'state_conv': nrm(ks[8], (DEPTH, DEC_BATCH, CONV_W - 1, D_FF)),
        'w_in': nrm(ks[9], (DEPTH, D_MODEL, D_IN), D_MODEL ** -0.5),
        'b_igate': nrm(ks[10], (DEPTH, M_HEADS), 0.1),
        'b_fgate': 3.0 + nrm(ks[11], (DEPTH, M_HEADS), 0.5),
        'g_mhnorm': 1.0 + nrm(ks[12], (DEPTH, M_HEADS, M_V), 0.05),
        'w_proj_m': nrm(ks[13], (DEPTH, M_WIDTH, D_MODEL), M_WIDTH ** -0.5),
        'w_proj_a': nrm(ks[14], (DEPTH, A_WIDTH, D_MODEL), A_WIDTH ** -0.5),
        'w_out': nrm(ks[15], (DEPTH, D_MODEL, D_MODEL), D_MODEL ** -0.5),
        'g_pre_mix': 1.0 + nrm(ks[16], (DEPTH, D_MODEL), 0.05),
        'g_post_mix': 1.0 + nrm(ks[17], (DEPTH, D_MODEL), 0.05),
        'g_pre_ffn': 1.0 + nrm(ks[18], (DEPTH, D_MODEL), 0.05),
        'g_post_ffn': 1.0 + nrm(ks[19], (DEPTH, D_MODEL), 0.05),
        'w_up': nrm(ks[20], (DEPTH, D_MODEL, 2 * D_FF), D_MODEL ** -0.5),
        'conv_w': nrm(ks[21], (DEPTH, CONV_W, D_FF), CONV_W ** -0.5),
        'conv_b': nrm(ks[22], (DEPTH, D_FF), 0.01),
        'w_down': nrm(ks[23], (DEPTH, D_FF, D_MODEL), D_FF ** -0.5),
    }


def reference(x_prompt, x_sample, cache_k, cache_v, cache_idx_k, state_C, state_n, state_m, state_conv,
              w_in, b_igate, b_fgate, g_mhnorm, w_proj_m, w_proj_a, w_out,
              g_pre_mix, g_post_mix, g_pre_ffn, g_post_ffn, w_up, conv_w, conv_b, w_down):
    f32 = jnp.float32
    B, S, _ = x_prompt.shape
    Ts = x_sample.shape[1]
    P = cache_k.shape[2]
    pos_p = jnp.arange(S)
    pos_s = P + jnp.arange(Ts)
    empty_kv = jnp.zeros((B, 0, A_KV_HEADS, A_HD), x_prompt.dtype)
    empty_ik = jnp.zeros((B, 0, IDX_DIM), x_prompt.dtype)
    C0 = jnp.zeros((B, M_HEADS, M_V, M_QK), f32)
    n0 = jnp.zeros((B, M_HEADS, M_QK), f32)
    m0 = jnp.zeros((B, M_HEADS), f32)
    conv0 = jnp.zeros((B, CONV_W - 1, D_FF), x_prompt.dtype)

    yp, ys = x_prompt, x_sample
    outs_p, outs_s = [], []
    for l in range(DEPTH):
        weights = (w_in[l], b_igate[l], b_fgate[l], g_mhnorm[l], w_proj_m[l], w_proj_a[l], w_out[l],
                   g_pre_mix[l], g_post_mix[l], g_pre_ffn[l], g_post_ffn[l], w_up[l], conv_w[l], conv_b[l], w_down[l])
        yp, *st_p = layer(yp, pos_p, empty_kv, empty_kv, empty_ik, C0, n0, m0, conv0, *weights)
        ys, *st_s = layer(ys, pos_s, cache_k[l], cache_v[l], cache_idx_k[l],
                          state_C[l], state_n[l], state_m[l], state_conv[l], *weights)
        outs_p.append(st_p)
        outs_s.append(st_s)

    def stk(outs, i):
        return jnp.stack([o[i] for o in outs])

    return (yp, ys,
            stk(outs_p, 0), stk(outs_p, 1), stk(outs_p, 2), stk(outs_p, 3), stk(outs_p, 4), stk(outs_p, 5), stk(outs_p, 6),
            stk(outs_s, 0), stk(outs_s, 1), stk(outs_s, 2), stk(outs_s, 3), stk(outs_s, 4), stk(outs_s, 5), stk(outs_s, 6))
```

```python
import functools
import math

import jax
import jax.numpy as jnp
from jax import lax
from jax.experimental import pallas as pl
from jax.experimental.pallas import tpu as pltpu

CHUNK = 64
NORM_EPS = 1e-6
ROPE_THETA = 500000.0
M_HEADS = 8
M_QK = 128
M_V = 256
A_HEADS = 16
A_KV_HEADS = 4
A_GROUP = A_HEADS // A_KV_HEADS
A_HD = 128
A_ROT = A_HD // 4
IDX_HEADS = 16
IDX_DIM = 64
IDX_ROT = IDX_DIM // 4
TOPK_MAX = 256
CONV_W = 3

LANES = 128
SUBLANES = 8
VMEM_LIMIT_BYTES = 52 * 1024 * 1024

OFF_MQ = 0
OFF_MK = OFF_MQ + M_HEADS * M_QK
OFF_MV = OFF_MK + M_HEADS * M_QK
OFF_MO = OFF_MV + M_HEADS * M_V
OFF_AQ = OFF_MO + M_HEADS * M_V
OFF_AK = OFF_AQ + A_HEADS * A_HD
OFF_AV = OFF_AK + A_KV_HEADS * A_HD
OFF_IQ = OFF_AV + A_KV_HEADS * A_HD
OFF_GM = OFF_IQ + IDX_HEADS * IDX_DIM
SM_IK = 0
SM_MI = SM_IK + IDX_DIM
SM_MF = SM_MI + M_HEADS
SM_IW = SM_MF + M_HEADS
SM_END = SM_IW + IDX_HEADS

INT_MIN = -2 ** 31
KEY_NEG_INF = INT_MIN + 0x7FFFFF
NEG_BIG = -1e30

_BF16 = jnp.bfloat16
_F32 = jnp.float32


def _tile(dim, target, quantum=LANES):
    if dim <= target:
        return dim
    t = (target // quantum) * quantum
    while t >= quantum:
        if dim % t == 0:
            return t
        t -= quantum
    return dim


def _params(sem):
    return pltpu.CompilerParams(dimension_semantics=sem, vmem_limit_bytes=VMEM_LIMIT_BYTES)


def _dot(a, b):
    return jnp.dot(a, b, preferred_element_type=_F32)


def _dot_nt(a, b):
    return lax.dot_general(a, b, (((1,), (1,)), ((), ())), preferred_element_type=_F32)


def _dot_tn(a, b):
    return lax.dot_general(a, b, (((0,), (0,)), ((), ())), preferred_element_type=_F32)


def _rmsnorm_kernel(x_ref, g_ref, o_ref):
    x = x_ref[...]
    ms = jnp.mean(x * x, axis=-1, keepdims=True)
    o_ref[...] = (x * lax.rsqrt(ms + NORM_EPS) * g_ref[...]).astype(o_ref.dtype)


def _rmsnorm_cast(x, g):
    m, d = x.shape
    tm = _tile(m, 256, SUBLANES)
    return pl.pallas_call(
        _rmsnorm_kernel,
        out_shape=jax.ShapeDtypeStruct((m, d), _BF16),
        grid=(m // tm,),
        in_specs=[pl.BlockSpec((tm, d), lambda i: (i, 0)),
                  pl.BlockSpec((1, d), lambda i: (0, 0))],
        out_specs=pl.BlockSpec((tm, d), lambda i: (i, 0)),
        compiler_params=_params(("parallel",)),
        name="rmsnorm_cast",
    )(x, g.reshape(1, d))


def _mm_kernel(a_ref, b_ref, o_ref, acc_ref, *, nk):
    k = pl.program_id(2)

    @pl.when(k == 0)
    def _():
        acc_ref[...] = jnp.zeros_like(acc_ref)

    acc_ref[...] += _dot(a_ref[...], b_ref[...])

    @pl.when(k == nk - 1)
    def _():
        o_ref[...] = acc_ref[...].astype(o_ref.dtype)


def _mm1_kernel(a_ref, b_ref, o_ref):
    o_ref[...] = _dot(a_ref[...], b_ref[...]).astype(o_ref.dtype)


def _matmul(a, b, out_dtype, *, tm=512, tn=1024, tk=None, name="matmul"):
    m, kd = a.shape
    _, n = b.shape
    tm = _tile(m, tm, SUBLANES)
    tn = _tile(n, tn)
    tk = kd if tk is None else _tile(kd, tk)
    nk = kd // tk
    if nk == 1:
        return pl.pallas_call(
            _mm1_kernel,
            out_shape=jax.ShapeDtypeStruct((m, n), out_dtype),
            grid=(n // tn, m // tm),
            in_specs=[pl.BlockSpec((tm, kd), lambda j, i: (i, 0)),
                      pl.BlockSpec((kd, tn), lambda j, i: (0, j))],
            out_specs=pl.BlockSpec((tm, tn), lambda j, i: (i, j)),
            compiler_params=_params(("parallel", "parallel")),
            name=name,
        )(a, b)
    return pl.pallas_call(
        functools.partial(_mm_kernel, nk=nk),
        out_shape=jax.ShapeDtypeStruct((m, n), out_dtype),
        grid=(n // tn, m // tm, nk),
        in_specs=[pl.BlockSpec((tm, tk), lambda j, i, k: (i, k)),
                  pl.BlockSpec((tk, tn), lambda j, i, k: (k, j))],
        out_specs=pl.BlockSpec((tm, tn), lambda j, i, k: (i, j)),
        scratch_shapes=[pltpu.VMEM((tm, tn), _F32)],
        compiler_params=_params(("parallel", "parallel", "arbitrary")),
        name=name,
    )(a, b)


def _mlstm_kernel(*refs, L, has_state):
    if has_state:
        (q_ref, k_ref, v_ref, o_ref, g_ref, gb_ref, gn_ref, c0_ref, s0_ref,
         h_ref, c_out_ref, s_out_ref, c_scr, n_scr, m_scr) = refs
    else:
        (q_ref, k_ref, v_ref, o_ref, g_ref, gb_ref, gn_ref,
         h_ref, c_out_ref, s_out_ref, c_scr, n_scr, m_scr) = refs
    head = pl.program_id(1)
    c = pl.program_id(2)
    nc = pl.num_programs(2)

    @pl.when(c == 0)
    def _():
        if has_state:
            c_scr[...] = c0_ref[0, 0]
            n_scr[...] = s0_ref[0, 0:1, :]
            m_scr[...] = s0_ref[0, 1:2, :]
        else:
            c_scr[...] = jnp.zeros_like(c_scr)
            n_scr[...] = jnp.zeros_like(n_scr)
            m_scr[...] = jnp.zeros_like(m_scr)

    gates = g_ref[...] + gb_ref[...]
    lane = lax.broadcasted_iota(jnp.int32, gates.shape, 1)
    ig_col = jnp.sum(jnp.where(lane == SM_MI + head, gates, 0.0), axis=1, keepdims=True)
    mf_col = jnp.sum(jnp.where(lane == SM_MF + head, gates, 0.0), axis=1, keepdims=True)
    lf_col = jnp.minimum(mf_col, 0.0) - jnp.log1p(jnp.exp(-jnp.abs(mf_col)))

    ri = lax.broadcasted_iota(jnp.int32, (L, L), 0)
    ci = lax.broadcasted_iota(jnp.int32, (L, L), 1)
    eye = ri == ci
    tril = ci <= ri
    lf_row = jnp.sum(jnp.where(eye, lf_col, 0.0), axis=0, keepdims=True)
    ig_row = jnp.sum(jnp.where(eye, ig_col, 0.0), axis=0, keepdims=True)
    b_col = jnp.sum(jnp.where(tril, lf_row, 0.0), axis=1, keepdims=True)
    b_row = jnp.sum(jnp.where(ri <= ci, lf_col, 0.0), axis=0, keepdims=True)
    logw = jnp.where(tril, b_col - b_row + ig_row, -jnp.inf)

    m_prev = m_scr[:, 0:1]
    inter = b_col + m_prev
    m_t = jnp.maximum(inter, jnp.max(logw, axis=1, keepdims=True))
    a = jnp.exp(inter - m_t)
    sw = jnp.exp(logw - m_t)

    q = q_ref[...]
    k = k_ref[...] * (M_QK ** -0.5)
    v = v_ref[...]
    qb = q.astype(_BF16)
    kb = k.astype(_BF16)
    s = _dot_nt(qb, kb) * sw
    c_old = c_scr[...]
    n_old = n_scr[...]
    num = a * _dot_nt(qb, c_old.astype(_BF16)) + _dot(s.astype(_BF16), v.astype(_BF16))
    den = a * jnp.sum(q * n_old, axis=1, keepdims=True) + jnp.sum(s, axis=1, keepdims=True)
    h = num / jnp.maximum(jnp.abs(den), jnp.exp(-m_t))

    m_new = m_t[L - 1:L, :]
    b_last = b_col[L - 1:L, :]
    g_col = jnp.exp(b_last - b_col + ig_col - m_new)
    decay = jnp.exp(b_last + m_prev - m_new)
    c_new = decay * c_old + _dot_tn((v * g_col).astype(_BF16), kb)
    n_new = decay * n_old + jnp.sum(g_col * k, axis=0, keepdims=True)
    c_scr[...] = c_new
    n_scr[...] = n_new
    m_scr[...] = jnp.broadcast_to(m_new, m_scr.shape)

    ms = jnp.mean(h * h, axis=1, keepdims=True)
    y = h * lax.rsqrt(ms + NORM_EPS) * gn_ref[0]
    h_ref[...] = (y * jax.nn.sigmoid(o_ref[...])).astype(h_ref.dtype)

    @pl.when(c == nc - 1)
    def _():
        c_out_ref[0, 0] = c_new
        s_out_ref[0] = jnp.zeros(s_out_ref.shape[1:], _F32)
        s_out_ref[0, 0:1, :] = n_new
        s_out_ref[0, 1:2, :] = jnp.broadcast_to(m_new, (1, M_QK))


def _mlstm(proj, small, gate_bias, g_mhnorm, row0, nseq, T, L, state):
    nc = T // L
    rb0 = row0 // L
    has_state = state is not None

    def rows(b, h, c):
        return rb0 + b * nc + c

    in_specs = [
        pl.BlockSpec((L, M_QK), lambda b, h, c: (rows(b, h, c), OFF_MQ // M_QK + h)),
        pl.BlockSpec((L, M_QK), lambda b, h, c: (rows(b, h, c), OFF_MK // M_QK + h)),
        pl.BlockSpec((L, M_V), lambda b, h, c: (rows(b, h, c), OFF_MV // M_V + h)),
        pl.BlockSpec((L, M_V), lambda b, h, c: (rows(b, h, c), OFF_MO // M_V + h)),
        pl.BlockSpec((L, LANES), lambda b, h, c: (rows(b, h, c), 0)),
        pl.BlockSpec((1, LANES), lambda b, h, c: (0, 0)),
        pl.BlockSpec((1, 1, M_V), lambda b, h, c: (h, 0, 0)),
    ]
    args = [proj, proj, proj, proj, small, gate_bias, g_mhnorm.reshape(M_HEADS, 1, M_V)]
    if has_state:
        c0, s0 = state
        in_specs += [pl.BlockSpec((1, 1, M_V, M_QK), lambda b, h, c: (b, h, 0, 0)),
                     pl.BlockSpec((1, SUBLANES, M_QK), lambda b, h, c: (b * M_HEADS + h, 0, 0))]
        args += [c0, s0]
    out_shape = (jax.ShapeDtypeStruct((nseq * T, M_HEADS * M_V), _BF16),
                 jax.ShapeDtypeStruct((nseq, M_HEADS, M_V, M_QK), _F32),
                 jax.ShapeDtypeStruct((nseq * M_HEADS, SUBLANES, M_QK), _F32))
    out_specs = (pl.BlockSpec((L, M_V), lambda b, h, c: (b * nc + c, h)),
                 pl.BlockSpec((1, 1, M_V, M_QK), lambda b, h, c: (b, h, 0, 0)),
                 pl.BlockSpec((1, SUBLANES, M_QK), lambda b, h, c: (b * M_HEADS + h, 0, 0)))
    hm, c_new, stats = pl.pallas_call(
        functools.partial(_mlstm_kernel, L=L, has_state=has_state),
        out_shape=out_shape,
        grid=(nseq, M_HEADS, nc),
        in_specs=in_specs,
        out_specs=out_specs,
        scratch_shapes=[pltpu.VMEM((M_V, M_QK), _F32), pltpu.VMEM((1, M_QK), _F32),
                        pltpu.VMEM((1, M_QK), _F32)],
        compiler_params=_params(("parallel", "parallel", "arbitrary")),
        name="mlstm",
    )(*args)
    n_new = stats[:, 0, :].reshape(nseq, M_HEADS, M_QK)
    m_new = stats[:, 1, 0].reshape(nseq, M_HEADS)
    return hm, c_new, n_new, m_new


def _rope(x, cos, sin_lo, sin_hi, half):
    n = x.shape[-1]
    return (x * cos + pltpu.roll(x, n - half, 1) * sin_lo + pltpu.roll(x, half, 1) * sin_hi)


def _rope_kernel(aq_ref, ak_ref, av_ref, iq_ref, sm_ref, ta_ref, ti_ref,
                 q_out, k_out, kb_out, vb_out, iq_out, ik_out):
    ca, sa_lo, sa_hi = ta_ref[0], ta_ref[1], ta_ref[2]
    ci, si_lo, si_hi = ti_ref[0], ti_ref[1], ti_ref[2]
    for h in range(A_HEADS):
        sl = slice(h * A_HD, (h + 1) * A_HD)
        q_out[:, sl] = _rope(aq_ref[:, sl], ca, sa_lo, sa_hi, A_ROT // 2).astype(q_out.dtype)
    for h in range(A_KV_HEADS):
        sl = slice(h * A_HD, (h + 1) * A_HD)
        kr = _rope(ak_ref[:, sl], ca, sa_lo, sa_hi, A_ROT // 2)
        k_out[:, sl] = kr
        kb_out[:, sl] = kr.astype(kb_out.dtype)
    vb_out[...] = av_ref[...].astype(vb_out.dtype)
    for p in range(IDX_HEADS * IDX_DIM // LANES):
        sl = slice(p * LANES, (p + 1) * LANES)
        iq_out[:, sl] = _rope(iq_ref[:, sl], ci, si_lo, si_hi, IDX_ROT // 2)
    ik = _rope(sm_ref[...], ci, si_lo, si_hi, IDX_ROT // 2)
    ik_out[...] = ik[:, SM_IK:SM_IK + IDX_DIM]


def _rope_tables(pos, rot, width, reps_valid):
    half = rot // 2
    inv_freq = jnp.exp(jnp.arange(half, dtype=_F32) * (-2.0 * math.log(ROPE_THETA) / rot))
    ang = pos.astype(_F32)[:, None] * inv_freq[None, :]
    cos, sin = jnp.cos(ang), jnp.sin(ang)
    m = pos.shape[0]
    one = jnp.ones((m, width - rot), _F32)
    zero = jnp.zeros((m, width - rot), _F32)
    zh = jnp.zeros((m, half), _F32)
    c_head = jnp.concatenate([cos, cos, one], axis=1)
    lo_head = jnp.concatenate([-sin, zh, zero], axis=1)
    hi_head = jnp.concatenate([zh, sin, zero], axis=1)
    reps = LANES // width
    ident = (jnp.ones((m, width), _F32), jnp.zeros((m, width), _F32), jnp.zeros((m, width), _F32))
    out = []
    for t, idt in zip((c_head, lo_head, hi_head), ident):
        out.append(jnp.concatenate([t if r < reps_valid else idt for r in range(reps)], axis=1))
    return jnp.stack(out)


def _rope_all(proj, small, pos):
    m = proj.shape[0]
    tm = _tile(m, 256, SUBLANES)
    ta = _rope_tables(pos, A_ROT, A_HD, 1)
    ti = _rope_tables(pos, IDX_ROT, IDX_DIM, LANES // IDX_DIM)
    wq, wk, wi = A_HEADS * A_HD, A_KV_HEADS * A_HD, IDX_HEADS * IDX_DIM
    out_shape = (jax.ShapeDtypeStruct((m, wq), _BF16),
                 jax.ShapeDtypeStruct((m, wk), _F32),
                 jax.ShapeDtypeStruct((m, wk), _BF16),
                 jax.ShapeDtypeStruct((m, wk), _BF16),
                 jax.ShapeDtypeStruct((m, wi), _F32),
                 jax.ShapeDtypeStruct((m, IDX_DIM), _F32))
    return pl.pallas_call(
        _rope_kernel,
        out_shape=out_shape,
        grid=(m // tm,),
        in_specs=[pl.BlockSpec((tm, wq), lambda i: (i, OFF_AQ // wq)),
                  pl.BlockSpec((tm, wk), lambda i: (i, OFF_AK // wk)),
                  pl.BlockSpec((tm, wk), lambda i: (i, OFF_AV // wk)),
                  pl.BlockSpec((tm, wi), lambda i: (i, OFF_IQ // wi)),
                  pl.BlockSpec((tm, LANES), lambda i: (i, 0)),
                  pl.BlockSpec((3, tm, LANES), lambda i: (0, i, 0)),
                  pl.BlockSpec((3, tm, LANES), lambda i: (0, i, 0))],
        out_specs=(pl.BlockSpec((tm, wq), lambda i: (i, 0)),
                   pl.BlockSpec((tm, wk), lambda i: (i, 0)),
                   pl.BlockSpec((tm, wk), lambda i: (i, 0)),
                   pl.BlockSpec((tm, wk), lambda i: (i, 0)),
                   pl.BlockSpec((tm, wi), lambda i: (i, 0)),
                   pl.BlockSpec((tm, IDX_DIM), lambda i: (i, 0))),
        compiler_params=_params(("parallel",)),
        name="rope",
    )(proj, proj, proj, proj, small, ta, ti)


def _attn_kernel(q_ref, iq_ref, sm_ref, k_ref, v_ref, ik_ref, o_ref, key_scr,
                 *, TQ, TK, past_len, l_valid, topk):
    i = pl.program_id(1)
    q_start = past_len + i * TQ
    qpos = q_start + lax.broadcasted_iota(jnp.int32, (TQ, 1), 0)
    qchunk = qpos // CHUNK
    last_vis = jnp.minimum(((q_start + TQ - 1) // CHUNK + 1) * CHUNK, l_valid)
    nkt = (last_vis + TK - 1) // TK
    lane_pos = lax.broadcasted_iota(jnp.int32, (1, TK), 1)

    def score_tile(kt, carry):
        k0 = pl.multiple_of(kt * TK, TK)
        ik_t = ik_ref[pl.ds(k0, TK), :].astype(_BF16)
        acc = jnp.zeros((TQ, TK), _F32)
        for h in range(IDX_HEADS):
            iq_h = iq_ref[:, h * IDX_DIM:(h + 1) * IDX_DIM].astype(_BF16)
            w_h = sm_ref[:, SM_IW + h:SM_IW + h + 1] * ((IDX_DIM ** -0.5) * (IDX_HEADS ** -0.5))
            acc = acc + jnp.maximum(_dot_nt(iq_h, ik_t), 0.0) * w_h
        acc = acc + 0.0
        bits = pltpu.bitcast(acc, jnp.int32)
        key = jnp.where(bits < 0, bits ^ 0x7FFFFFFF, bits)
        kpos = k0 + lane_pos
        vis = ((kpos // CHUNK) <= qchunk) & (kpos < l_valid)
        key_scr[kt] = jnp.where(vis, key, INT_MIN)
        return carry

    lax.fori_loop(0, nkt, score_tile, 0)

    def count(pred_fn):
        def body(kt, cnt):
            return cnt + jnp.sum(pred_fn(key_scr[kt], kt).astype(jnp.int32), axis=1, keepdims=True)
        return lax.fori_loop(0, nkt, body, jnp.zeros((TQ, 1), jnp.int32))

    def bit_step(it, thr_u):
        cand_u = thr_u | lax.shift_left(jnp.int32(1), 31 - it)
        cand_s = cand_u ^ INT_MIN
        cnt = count(lambda key, kt: key >= cand_s)
        return jnp.where(cnt >= topk, cand_u, thr_u)

    thr = lax.fori_loop(0, 32, bit_step, jnp.zeros((TQ, 1), jnp.int32)) ^ INT_MIN
    n_ge = count(lambda key, kt: key >= thr)
    n_gt = count(lambda key, kt: key > thr)

    def tie_search(_):
        need = topk - n_gt

        nbits = (key_scr.shape[0] * TK).bit_length()

        def pos_step(it, p):
            cand = p | lax.shift_left(jnp.int32(1), nbits - 1 - it)
            cnt = count(lambda key, kt: (key == thr) & ((kt * TK + lane_pos) < cand))
            return jnp.where(cnt < need, cand, p)

        return lax.fori_loop(0, nbits, pos_step, jnp.zeros((TQ, 1), jnp.int32))

    has_tie = jnp.max(n_ge.astype(_F32)) > topk
    p_last = lax.cond(has_tie, tie_search, lambda _: jnp.full((TQ, 1), 2 ** 31 - 1, jnp.int32), 0)

    scale = A_HD ** -0.5
    for g in range(A_KV_HEADS):
        qg = jnp.concatenate(
            [q_ref[:, (g * A_GROUP + j) * A_HD:(g * A_GROUP + j + 1) * A_HD] for j in range(A_GROUP)], axis=0)

        def attn_tile(kt, carry, g=g, qg=qg):
            m_i, l_i, acc = carry
            k0 = pl.multiple_of(kt * TK, TK)
            k_t = k_ref[pl.ds(k0, TK), g * A_HD:(g + 1) * A_HD]
            v_t = v_ref[pl.ds(k0, TK), g * A_HD:(g + 1) * A_HD]
            key = key_scr[kt]
            sel = ((key > thr) | ((key == thr) & ((k0 + lane_pos) <= p_last))) & (key > KEY_NEG_INF)
            s = (_dot_nt(qg, k_t) * scale).reshape(A_GROUP, TQ, TK)
            s = jnp.where(sel[None], s, NEG_BIG)
            m_new = jnp.maximum(m_i, jnp.max(s, axis=-1, keepdims=True))
            alpha = jnp.exp(m_i - m_new)
            p = jnp.where(sel[None], jnp.exp(s - m_new), 0.0)
            l_new = alpha * l_i + jnp.sum(p, axis=-1, keepdims=True)
            pv = _dot(p.reshape(A_GROUP * TQ, TK).astype(_BF16), v_t).reshape(A_GROUP, TQ, A_HD)
            return m_new, l_new, alpha * acc + pv

        init = (jnp.full((A_GROUP, TQ, 1), NEG_BIG, _F32), jnp.zeros((A_GROUP, TQ, 1), _F32),
                jnp.zeros((A_GROUP, TQ, A_HD), _F32))
        _, l_f, acc_f = lax.fori_loop(0, nkt, attn_tile, init)
        out = acc_f / l_f
        for j in range(A_GROUP):
            hh = g * A_GROUP + j
            o_ref[:, hh * A_HD:(hh + 1) * A_HD] = out[j].astype(o_ref.dtype)


def _attention(q_rot, iq_rot, small, k_all, v_all, ik_all, row0, nseq, T, TQ, TK, lp, past_len, l_valid):
    nq = T // TQ
    qb0 = row0 // TQ
    topk = min(TOPK_MAX, l_valid // 4)
    wq, wk, wi = A_HEADS * A_HD, A_KV_HEADS * A_HD, IDX_HEADS * IDX_DIM
    return pl.pallas_call(
        functools.partial(_attn_kernel, TQ=TQ, TK=TK, past_len=past_len, l_valid=l_valid, topk=topk),
        out_shape=jax.ShapeDtypeStruct((nseq * T, wq), _BF16),
        grid=(nseq, nq),
        in_specs=[pl.BlockSpec((TQ, wq), lambda b, i: (qb0 + b * nq + i, 0)),
                  pl.BlockSpec((TQ, wi), lambda b, i: (qb0 + b * nq + i, 0)),
                  pl.BlockSpec((TQ, LANES), lambda b, i: (qb0 + b * nq + i, 0)),
                  pl.BlockSpec((lp, wk), lambda b, i: (b, 0)),
                  pl.BlockSpec((lp, wk), lambda b, i: (b, 0)),
                  pl.BlockSpec((lp, IDX_DIM), lambda b, i: (b, 0))],
        out_specs=pl.BlockSpec((TQ, wq), lambda b, i: (b * nq + i, 0)),
        scratch_shapes=[pltpu.VMEM((lp // TK, TQ, TK), jnp.int32)],
        compiler_params=_params(("parallel", "arbitrary")),
        name="sparse_attention",
    )(q_rot, iq_rot, small, k_all, v_all, ik_all)


def _merge_kernel(hm_ref, ha_ref, wm_ref, wa_ref, gm_ref, ga_ref, o_ref):
    pm = _dot(hm_ref[...], wm_ref[...])
    pa = _dot(ha_ref[...], wa_ref[...])
    o_ref[...] = (jax.nn.sigmoid(gm_ref[...]) * pm + jax.nn.sigmoid(ga_ref[...]) * pa).astype(o_ref.dtype)


def _merge(hm, ha, wm, wa, proj, d):
    m = hm.shape[0]
    tm = _tile(m, 512, SUBLANES)
    tn = _tile(math.gcd(d, OFF_GM), 1024)
    gm0 = OFF_GM // tn
    ga0 = (OFF_GM + d) // tn
    km, ka = hm.shape[1], ha.shape[1]
    return pl.pallas_call(
        _merge_kernel,
        out_shape=jax.ShapeDtypeStruct((m, d), _BF16),
        grid=(d // tn, m // tm),
        in_specs=[pl.BlockSpec((tm, km), lambda j, i: (i, 0)),
                  pl.BlockSpec((tm, ka), lambda j, i: (i, 0)),
                  pl.BlockSpec((km, tn), lambda j, i: (0, j)),
                  pl.BlockSpec((ka, tn), lambda j, i: (0, j)),
                  pl.BlockSpec((tm, tn), lambda j, i: (i, gm0 + j)),
                  pl.BlockSpec((tm, tn), lambda j, i: (i, ga0 + j))],
        out_specs=pl.BlockSpec((tm, tn), lambda j, i: (i, j)),
        compiler_params=_params(("parallel", "parallel")),
        name="gated_merge",
    )(hm, ha, wm, wa, proj, proj)


def _resnorm_kernel(x_ref, y_ref, g_ref, *out_refs_and_g2, with_next):
    y = y_ref[...]
    ms = jnp.mean(y * y, axis=-1, keepdims=True)
    x1 = x_ref[...] + y * lax.rsqrt(ms + NORM_EPS) * g_ref[...]
    if with_next:
        g2_ref, x1_ref, h_ref = out_refs_and_g2
        x1_ref[...] = x1
        ms1 = jnp.mean(x1 * x1, axis=-1, keepdims=True)
        h_ref[...] = (x1 * lax.rsqrt(ms1 + NORM_EPS) * g2_ref[...]).astype(h_ref.dtype)
    else:
        (x1_ref,) = out_refs_and_g2
        x1_ref[...] = x1


def _resnorm(x, y, g, g_next=None):
    m, d = x.shape
    tm = _tile(m, 256, SUBLANES)
    row = pl.BlockSpec((tm, d), lambda i: (i, 0))
    vec = pl.BlockSpec((1, d), lambda i: (0, 0))
    with_next = g_next is not None
    if with_next:
        return pl.pallas_call(
            functools.partial(_resnorm_kernel, with_next=True),
            out_shape=(jax.ShapeDtypeStruct((m, d), _F32), jax.ShapeDtypeStruct((m, d), _BF16)),
            grid=(m // tm,), in_specs=[row, row, vec, vec], out_specs=(row, row),
            compiler_params=_params(("parallel",)), name="resnorm_next",
        )(x, y, g.reshape(1, d), g_next.reshape(1, d))
    return pl.pallas_call(
        functools.partial(_resnorm_kernel, with_next=False),
        out_shape=jax.ShapeDtypeStruct((m, d), _F32),
        grid=(m // tm,), in_specs=[row, row, vec], out_specs=row,
        compiler_params=_params(("parallel",)), name="resnorm",
    )(x, y, g.reshape(1, d))


def _ffn_up_kernel(*refs, tm, seq_len, has_init):
    if has_init:
        h_ref, wg_ref, wl_ref, cw_ref, cb_ref, init_ref, z_ref, tail_ref, carry_scr = refs
    else:
        h_ref, wg_ref, wl_ref, cw_ref, cb_ref, z_ref, tail_ref, carry_scr = refs
    i = pl.program_id(1)
    tn = z_ref.shape[1]

    @pl.when(i == 0)
    def _():
        carry_scr[...] = jnp.zeros_like(carry_scr)

    h = h_ref[...]
    gate = _dot(h, wg_ref[...])
    lin = _dot(h, wl_ref[...])
    row = lax.broadcasted_iota(jnp.int32, (tm, 1), 0)
    t = (i * tm + row) % seq_len
    carry = carry_scr[...]
    prev1 = pltpu.roll(gate, 1, 0)
    prev2 = pltpu.roll(gate, 2, 0)
    prev1 = jnp.where(row == 0, carry[SUBLANES - 1:SUBLANES], prev1)
    prev2 = jnp.where(row == 0, carry[SUBLANES - 2:SUBLANES - 1],
                      jnp.where(row == 1, carry[SUBLANES - 1:SUBLANES], prev2))
    if has_init:
        nseg = tm // seq_len
        init0 = jnp.concatenate([jnp.broadcast_to(init_ref[s, 0:1, :], (seq_len, tn)) for s in range(nseg)], axis=0)
        init1 = jnp.concatenate([jnp.broadcast_to(init_ref[s, 1:2, :], (seq_len, tn)) for s in range(nseg)], axis=0)
    else:
        init0 = init1 = jnp.zeros((1, tn), _F32)
    prev1 = jnp.where(t == 0, init1, prev1)
    prev2 = jnp.where(t == 0, init0, jnp.where(t == 1, init1, prev2))
    gc = cw_ref[0:1, :] * prev2 + cw_ref[1:2, :] * prev1 + cw_ref[2:3, :] * gate + cb_ref[...]
    act = 0.5 * gc * (1.0 + jnp.tanh(math.sqrt(2.0 / math.pi) * (gc + 0.044715 * (gc * gc * gc))))
    z_ref[...] = (act * lin).astype(z_ref.dtype)
    carry_scr[...] = gate[tm - SUBLANES:tm, :]
    seg = min(tm, seq_len)
    for s in range(tm // seg):
        tail_ref[s * SUBLANES:(s + 1) * SUBLANES, :] = gate[(s + 1) * seg - SUBLANES:(s + 1) * seg, :]


def _ffn_up(hf, w_gate, w_lin, conv_w, conv_b, row0, nrows, seq_len, init):
    d = hf.shape[1]
    dff = w_gate.shape[1]
    has_init = init is not None
    tm = _tile(nrows, 512, SUBLANES)
    if has_init:
        assert tm % seq_len == 0
    else:
        assert seq_len % tm == 0
    tn = _tile(dff, 256)
    seg = min(tm, seq_len)
    rb0 = row0 // tm
    ntail = tm // seg * SUBLANES
    in_specs = [pl.BlockSpec((tm, d), lambda j, i: (rb0 + i, 0)),
                pl.BlockSpec((d, tn), lambda j, i: (0, j)),
                pl.BlockSpec((d, tn), lambda j, i: (0, j)),
                pl.BlockSpec((CONV_W, tn), lambda j, i: (0, j)),
                pl.BlockSpec((1, tn), lambda j, i: (0, j))]
    args = [hf, w_gate, w_lin, conv_w, conv_b.reshape(1, dff)]
    if has_init:
        nseg = tm // seq_len
        in_specs.append(pl.BlockSpec((nseg, CONV_W - 1, tn), lambda j, i: (i, 0, j)))
        args.append(init)
    z, tails = pl.pallas_call(
        functools.partial(_ffn_up_kernel, tm=tm, seq_len=seq_len, has_init=has_init),
        out_shape=(jax.ShapeDtypeStruct((nrows, dff), _BF16),
                   jax.ShapeDtypeStruct((nrows // tm * ntail, dff), _F32)),
        grid=(dff // tn, nrows // tm),
        in_specs=in_specs,
        out_specs=(pl.BlockSpec((tm, tn), lambda j, i: (i, j)),
                   pl.BlockSpec((ntail, tn), lambda j, i: (i, j))),
        scratch_shapes=[pltpu.VMEM((SUBLANES, tn), _F32)],
        compiler_params=_params(("parallel", "arbitrary")),
        name="ffn_up_conv",
    )(*args)
    nseq = nrows // seq_len
    tails = tails.reshape(nrows // seg, SUBLANES, dff)
    segs_per_seq = seq_len // seg
    last = tails[segs_per_seq - 1::segs_per_seq, SUBLANES - (CONV_W - 1):, :]
    return z, last.reshape(nseq, CONV_W - 1, dff)


def _layer(x_all, geom, cache, state, w):
    (B, S, Bd, Td, P) = geom
    (cache_k, cache_v, cache_ik) = cache
    (state_C, state_n, state_m, state_conv) = state
    (w_in, b_igate, b_fgate, g_mhnorm, w_proj_m, w_proj_a, w_out,
     g_pre_mix, g_post_mix, g_pre_ffn, g_post_ffn, w_up, conv_w, conv_b, w_down) = w
    m_all, d = x_all.shape
    mp = B * S
    dff = w_down.shape[0]
    wk = A_KV_HEADS * A_HD

    c_mi = OFF_AQ
    c_aq = c_mi + 2 * M_HEADS
    c_ik = c_aq + (OFF_GM - OFF_AQ)
    c_iw = c_ik + IDX_DIM
    c_gm = c_iw + IDX_HEADS
    w_big = jnp.concatenate([w_in[:, :c_mi], w_in[:, c_aq:c_ik], w_in[:, c_gm:]], axis=1).astype(_BF16)
    w_small = jnp.concatenate([w_in[:, c_ik:c_iw], w_in[:, c_mi:c_aq], w_in[:, c_iw:c_gm],
                               jnp.zeros((d, LANES - SM_END), w_in.dtype)], axis=1).astype(_BF16)
    gate_bias = jnp.concatenate([jnp.zeros((SM_MI,), _F32), b_igate.astype(_F32), b_fgate.astype(_F32),
                                 jnp.zeros((LANES - SM_IW,), _F32)]).reshape(1, LANES)

    hn = _rmsnorm_cast(x_all, g_pre_mix)
    proj = _matmul(hn, w_big, _F32, tm=512, tn=1024, name="in_proj")
    small = _matmul(hn, w_small, _F32, tm=512, tn=LANES, name="in_proj_small")

    lp_chunk = _tile(S, 256, CHUNK)
    hm_p, c_p, n_p, m_p = _mlstm(proj, small, gate_bias, g_mhnorm, 0, B, S, lp_chunk, None)
    s0 = jnp.zeros((Bd * M_HEADS, SUBLANES, M_QK), _F32)
    s0 = s0.at[:, 0, :].set(state_n.reshape(Bd * M_HEADS, M_QK).astype(_F32))
    s0 = s0.at[:, 1, :].set(jnp.broadcast_to(state_m.reshape(Bd * M_HEADS, 1).astype(_F32), (Bd * M_HEADS, M_QK)))
    hm_s, c_s, n_s, m_s = _mlstm(proj, small, gate_bias, g_mhnorm, mp, Bd, Td, min(CHUNK, Td),
                                 (state_C.astype(_F32), s0))
    hm = jnp.concatenate([hm_p, hm_s], axis=0)

    pos = jnp.concatenate([jnp.tile(jnp.arange(S, dtype=jnp.int32), B),
                           jnp.tile(P + jnp.arange(Td, dtype=jnp.int32), Bd)])
    q_rot, k_rot, k_bf, v_bf, iq_rot, ik_rot = _rope_all(proj, small, pos)
    tq_p = _tile(S, 128, CHUNK)
    tk_p = _tile(S, 512)
    ha_p = _attention(q_rot, iq_rot, small, k_bf, v_bf, ik_rot, 0, B, S, tq_p, tk_p, S, 0, S)
    l_s = P + Td
    tk_s = 256
    lp_s = -(-l_s // tk_s) * tk_s
    pad = lp_s - l_s
    k_s = jnp.concatenate([cache_k.reshape(Bd, P, wk).astype(_BF16), k_bf[mp:].reshape(Bd, Td, wk),
                           jnp.zeros((Bd, pad, wk), _BF16)], axis=1).reshape(Bd * lp_s, wk)
    v_s = jnp.concatenate([cache_v.reshape(Bd, P, wk).astype(_BF16), v_bf[mp:].reshape(Bd, Td, wk),
                           jnp.zeros((Bd, pad, wk), _BF16)], axis=1).reshape(Bd * lp_s, wk)
    ik_s = jnp.concatenate([cache_ik.astype(_F32), ik_rot[mp:].reshape(Bd, Td, IDX_DIM),
                            jnp.zeros((Bd, pad, IDX_DIM), _F32)], axis=1).reshape(Bd * lp_s, IDX_DIM)
    ha_s = _attention(q_rot, iq_rot, small, k_s, v_s, ik_s, mp, Bd, Td, Td, tk_s, lp_s, P, l_s)
    ha = jnp.concatenate([ha_p, ha_s], axis=0)

    mix = _merge(hm, ha, w_proj_m.astype(_BF16), w_proj_a.astype(_BF16), proj, d)
    y1 = _matmul(mix, w_out.astype(_BF16), _F32, tm=512, tn=1024, name="out_proj")
    x1, hf = _resnorm(x_all, y1, g_post_mix, g_pre_ffn)

    w_up_b = w_up.astype(_BF16)
    w_gate, w_lin = w_up_b[:, :dff], w_up_b[:, dff:]
    z_p, conv_p = _ffn_up(hf, w_gate, w_lin, conv_w, conv_b, 0, mp, S, None)
    z_s, conv_s = _ffn_up(hf, w_gate, w_lin, conv_w, conv_b, mp, Bd * Td, Td, state_conv.astype(_F32))
    z = jnp.concatenate([z_p, z_s], axis=0)
    y2 = _matmul(z, w_down.astype(_BF16), _F32, tm=512, tn=1024, tk=dff // 2, name="down_proj")
    x2 = _resnorm(x1, y2, g_post_ffn)

    av = proj[:, OFF_AV:OFF_AV + wk]
    outs_p = (k_rot[:mp].reshape(B, S, A_KV_HEADS, A_HD), av[:mp].reshape(B, S, A_KV_HEADS, A_HD),
              ik_rot[:mp].reshape(B, S, IDX_DIM), c_p, n_p, m_p, conv_p)
    outs_s = (k_rot[mp:].reshape(Bd, Td, A_KV_HEADS, A_HD), av[mp:].reshape(Bd, Td, A_KV_HEADS, A_HD),
              ik_rot[mp:].reshape(Bd, Td, IDX_DIM), c_s, n_s, m_s, conv_s)
    return x2, outs_p, outs_s


def kernel(x_prompt, x_sample, cache_k, cache_v, cache_idx_k, state_C, state_n, state_m, state_conv,
           w_in, b_igate, b_fgate, g_mhnorm, w_proj_m, w_proj_a, w_out,
           g_pre_mix, g_post_mix, g_pre_ffn, g_post_ffn, w_up, conv_w, conv_b, w_down):
    B, S, d = x_prompt.shape
    Bd, Td, _ = x_sample.shape
    P = cache_k.shape[2]
    depth = w_in.shape[0]
    mp = B * S
    x_all = jnp.concatenate([x_prompt.reshape(mp, d), x_sample.reshape(Bd * Td, d)], axis=0)
    all_p, all_s = [], []
    for l in range(depth):
        w = (w_in[l], b_igate[l], b_fgate[l], g_mhnorm[l], w_proj_m[l], w_proj_a[l], w_out[l],
             g_pre_mix[l], g_post_mix[l], g_pre_ffn[l], g_post_ffn[l], w_up[l], conv_w[l], conv_b[l], w_down[l])
        x_all, outs_p, outs_s = _layer(x_all, (B, S, Bd, Td, P), (cache_k[l], cache_v[l], cache_idx_k[l]),
                                       (state_C[l], state_n[l], state_m[l], state_conv[l]), w)
        all_p.append(outs_p)
        all_s.append(outs_s)

    def stk(outs, i):
        return jnp.stack([o[i] for o in outs])

    yp = x_all[:mp].reshape(B, S, d)
    ys = x_all[mp:].reshape(Bd, Td, d)
    return (yp, ys) + tuple(stk(all_p, i) for i in range(7)) + tuple(stk(all_s, i) for i in range(7))
```

```python
import functools
import math

import jax
import jax.numpy as jnp
from jax import lax
from jax.experimental import pallas as pl
from jax.experimental.pallas import tpu as pltpu

CHUNK = 64
NORM_EPS = 1e-6
ROPE_THETA = 500000.0
M_HEADS = 8
M_QK = 128
M_V = 256
A_HEADS = 16
A_KV_HEADS = 4
A_GROUP = A_HEADS // A_KV_HEADS
A_HD = 128
A_ROT = A_HD // 4
IDX_HEADS = 16
IDX_DIM = 64
IDX_ROT = IDX_DIM // 4
TOPK_MAX = 256
CONV_W = 3

LANES = 128
SUBLANES = 8
VMEM_LIMIT_BYTES = 52 * 1024 * 1024

A_MQ = 0
A_MK = A_MQ + M_HEADS * M_QK
A_MV = A_MK + M_HEADS * M_QK
A_MO = A_MV + M_HEADS * M_V
A_END = A_MO + M_HEADS * M_V
B_AQ = 0
B_AK = B_AQ + A_HEADS * A_HD
B_AV = B_AK + A_KV_HEADS * A_HD
B_IQ = B_AV + A_KV_HEADS * A_HD
B_END = B_IQ + IDX_HEADS * IDX_DIM
SM_IK = 0
SM_MI = SM_IK + IDX_DIM
SM_MF = SM_MI + M_HEADS
SM_IW = SM_MF + M_HEADS
SM_END = SM_IW + IDX_HEADS
IDX_XW = 4 * IDX_DIM

INT_MIN = -2 ** 31
KEY_NEG_INF = INT_MIN + 0x7FFFFF
NEG_BIG = -1e30

_BF16 = jnp.bfloat16
_F32 = jnp.float32


def _tile(dim, target, quantum=LANES):
    if dim <= target:
        return dim
    t = (target // quantum) * quantum
    while t >= quantum:
        if dim % t == 0:
            return t
        t -= quantum
    return dim


def _params(sem):
    return pltpu.CompilerParams(dimension_semantics=sem, vmem_limit_bytes=VMEM_LIMIT_BYTES)


def _dot(a, b):
    return jnp.dot(a, b, preferred_element_type=_F32)


def _dot_nt(a, b):
    return lax.dot_general(a, b, (((1,), (1,)), ((), ())), preferred_element_type=_F32)


def _dot_tn(a, b):
    return lax.dot_general(a, b, (((0,), (0,)), ((), ())), preferred_element_type=_F32)


def _rmsnorm_kernel(x_ref, g_ref, o_ref):
    x = x_ref[...]
    ms = jnp.mean(x * x, axis=-1, keepdims=True)
    o_ref[...] = (x * lax.rsqrt(ms + NORM_EPS) * g_ref[...]).astype(o_ref.dtype)


def _rmsnorm_cast(x, g):
    m, d = x.shape
    tm = _tile(m, 256, SUBLANES)
    return pl.pallas_call(
        _rmsnorm_kernel,
        out_shape=jax.ShapeDtypeStruct((m, d), _BF16),
        grid=(m // tm,),
        in_specs=[pl.BlockSpec((tm, d), lambda i: (i, 0)),
                  pl.BlockSpec((1, d), lambda i: (0, 0))],
        out_specs=pl.BlockSpec((tm, d), lambda i: (i, 0)),
        compiler_params=_params(("parallel",)),
        name="rmsnorm_cast",
    )(x, g.reshape(1, d))


def _mm_kernel(a_ref, b_ref, o_ref, acc_ref, *, nk):
    k = pl.program_id(2)

    @pl.when(k == 0)
    def _():
        acc_ref[...] = jnp.zeros_like(acc_ref)

    acc_ref[...] += _dot(a_ref[...], b_ref[...])

    @pl.when(k == nk - 1)
    def _():
        o_ref[...] = acc_ref[...].astype(o_ref.dtype)


def _mm1_kernel(a_ref, b_ref, o_ref):
    o_ref[...] = _dot(a_ref[...], b_ref[...]).astype(o_ref.dtype)


def _matmul(a, b, out_dtype, *, tm=512, tn=1024, tk=None, name="matmul"):
    m, kd = a.shape
    _, n = b.shape
    tm = _tile(m, tm, SUBLANES)
    tn = _tile(n, tn)
    tk = kd if tk is None else _tile(kd, tk)
    nk = kd // tk
    if nk == 1:
        return pl.pallas_call(
            _mm1_kernel,
            out_shape=jax.ShapeDtypeStruct((m, n), out_dtype),
            grid=(n // tn, m // tm),
            in_specs=[pl.BlockSpec((tm, kd), lambda j, i: (i, 0)),
                      pl.BlockSpec((kd, tn), lambda j, i: (0, j))],
            out_specs=pl.BlockSpec((tm, tn), lambda j, i: (i, j)),
            compiler_params=_params(("parallel", "parallel")),
            name=name,
        )(a, b)
    return pl.pallas_call(
        functools.partial(_mm_kernel, nk=nk),
        out_shape=jax.ShapeDtypeStruct((m, n), out_dtype),
        grid=(n // tn, m // tm, nk),
        in_specs=[pl.BlockSpec((tm, tk), lambda j, i, k: (i, k)),
                  pl.BlockSpec((tk, tn), lambda j, i, k: (k, j))],
        out_specs=pl.BlockSpec((tm, tn), lambda j, i, k: (i, j)),
        scratch_shapes=[pltpu.VMEM((tm, tn), _F32)],
        compiler_params=_params(("parallel", "parallel", "arbitrary")),
        name=name,
    )(a, b)


def _mlstm_kernel(*refs, L, has_state):
    if has_state:
        (q_ref, k_ref, v_ref, o_ref, g_ref, gb_ref, gn_ref, c0_ref, s0_ref,
         h_ref, c_out_ref, s_out_ref, c_scr, n_scr, m_scr) = refs
    else:
        (q_ref, k_ref, v_ref, o_ref, g_ref, gb_ref, gn_ref,
         h_ref, c_out_ref, s_out_ref, c_scr, n_scr, m_scr) = refs
    head = pl.program_id(1)
    c = pl.program_id(2)
    nc = pl.num_programs(2)

    @pl.when(c == 0)
    def _():
        if has_state:
            c_scr[...] = c0_ref[0, 0]
            n_scr[...] = s0_ref[0, 0:1, :]
            m_scr[...] = s0_ref[0, 1:2, :]
        else:
            c_scr[...] = jnp.zeros_like(c_scr)
            n_scr[...] = jnp.zeros_like(n_scr)
            m_scr[...] = jnp.zeros_like(m_scr)

    gates = g_ref[...] + gb_ref[...]
    lane = lax.broadcasted_iota(jnp.int32, gates.shape, 1)
    ig_col = jnp.sum(jnp.where(lane == SM_MI + head, gates, 0.0), axis=1, keepdims=True)
    mf_col = jnp.sum(jnp.where(lane == SM_MF + head, gates, 0.0), axis=1, keepdims=True)
    lf_col = jnp.minimum(mf_col, 0.0) - jnp.log1p(jnp.exp(-jnp.abs(mf_col)))

    ri = lax.broadcasted_iota(jnp.int32, (L, L), 0)
    ci = lax.broadcasted_iota(jnp.int32, (L, L), 1)
    eye = ri == ci
    tril = ci <= ri
    lf_row = jnp.sum(jnp.where(eye, lf_col, 0.0), axis=0, keepdims=True)
    ig_row = jnp.sum(jnp.where(eye, ig_col, 0.0), axis=0, keepdims=True)
    b_col = jnp.sum(jnp.where(tril, lf_row, 0.0), axis=1, keepdims=True)
    b_row = jnp.sum(jnp.where(ri <= ci, lf_col, 0.0), axis=0, keepdims=True)
    logw = jnp.where(tril, b_col - b_row + ig_row, -jnp.inf)

    m_prev = m_scr[:, 0:1]
    inter = b_col + m_prev
    m_t = jnp.maximum(inter, jnp.max(logw, axis=1, keepdims=True))
    a = jnp.exp(inter - m_t)
    sw = jnp.exp(logw - m_t)

    q = q_ref[...]
    k = k_ref[...] * (M_QK ** -0.5)
    v = v_ref[...]
    qb = q.astype(_BF16)
    kb = k.astype(_BF16)
    s = _dot_nt(qb, kb) * sw
    c_old = c_scr[...]
    n_old = n_scr[...]
    num = a * _dot_nt(qb, c_old.astype(_BF16)) + _dot(s.astype(_BF16), v.astype(_BF16))
    den = a * jnp.sum(q * n_old, axis=1, keepdims=True) + jnp.sum(s, axis=1, keepdims=True)
    h = num / jnp.maximum(jnp.abs(den), jnp.exp(-m_t))

    m_new = m_t[L - 1:L, :]
    b_last = b_col[L - 1:L, :]
    g_col = jnp.exp(b_last - b_col + ig_col - m_new)
    decay = jnp.exp(b_last + m_prev - m_new)
    c_new = decay * c_old + _dot_tn((v * g_col).astype(_BF16), kb)
    n_new = decay * n_old + jnp.sum(g_col * k, axis=0, keepdims=True)
    c_scr[...] = c_new
    n_scr[...] = n_new
    m_scr[...] = jnp.broadcast_to(m_new, m_scr.shape)

    ms = jnp.mean(h * h, axis=1, keepdims=True)
    y = h * lax.rsqrt(ms + NORM_EPS) * gn_ref[0]
    h_ref[...] = (y * jax.nn.sigmoid(o_ref[...])).astype(h_ref.dtype)

    @pl.when(c == nc - 1)
    def _():
        c_out_ref[0, 0] = c_new
        s_out_ref[0] = jnp.zeros(s_out_ref.shape[1:], _F32)
        s_out_ref[0, 0:1, :] = n_new
        s_out_ref[0, 1:2, :] = jnp.broadcast_to(m_new, (1, M_QK))


def _mlstm(proj, small, gate_bias, g_mhnorm, row0, nseq, T, L, state):
    nc = T // L
    rb0 = row0 // L
    has_state = state is not None

    def rows(b, h, c):
        return rb0 + b * nc + c

    in_specs = [
        pl.BlockSpec((L, M_QK), lambda b, h, c: (rows(b, h, c), A_MQ // M_QK + h)),
        pl.BlockSpec((L, M_QK), lambda b, h, c: (rows(b, h, c), A_MK // M_QK + h)),
        pl.BlockSpec((L, M_V), lambda b, h, c: (rows(b, h, c), A_MV // M_V + h)),
        pl.BlockSpec((L, M_V), lambda b, h, c: (rows(b, h, c), A_MO // M_V + h)),
        pl.BlockSpec((L, LANES), lambda b, h, c: (rows(b, h, c), 0)),
        pl.BlockSpec((1, LANES), lambda b, h, c: (0, 0)),
        pl.BlockSpec((1, 1, M_V), lambda b, h, c: (h, 0, 0)),
    ]
    args = [proj, proj, proj, proj, small, gate_bias, g_mhnorm.reshape(M_HEADS, 1, M_V)]
    if has_state:
        c0, s0 = state
        in_specs += [pl.BlockSpec((1, 1, M_V, M_QK), lambda b, h, c: (b, h, 0, 0)),
                     pl.BlockSpec((1, SUBLANES, M_QK), lambda b, h, c: (b * M_HEADS + h, 0, 0))]
        args += [c0, s0]
    out_shape = (jax.ShapeDtypeStruct((nseq * T, M_HEADS * M_V), _BF16),
                 jax.ShapeDtypeStruct((nseq, M_HEADS, M_V, M_QK), _F32),
                 jax.ShapeDtypeStruct((nseq * M_HEADS, SUBLANES, M_QK), _F32))
    out_specs = (pl.BlockSpec((L, M_V), lambda b, h, c: (b * nc + c, h)),
                 pl.BlockSpec((1, 1, M_V, M_QK), lambda b, h, c: (b, h, 0, 0)),
                 pl.BlockSpec((1, SUBLANES, M_QK), lambda b, h, c: (b * M_HEADS + h, 0, 0)))
    hm, c_new, stats = pl.pallas_call(
        functools.partial(_mlstm_kernel, L=L, has_state=has_state),
        out_shape=out_shape,
        grid=(nseq, M_HEADS, nc),
        in_specs=in_specs,
        out_specs=out_specs,
        scratch_shapes=[pltpu.VMEM((M_V, M_QK), _F32), pltpu.VMEM((1, M_QK), _F32),
                        pltpu.VMEM((1, M_QK), _F32)],
        compiler_params=_params(("parallel", "parallel", "arbitrary")),
        name="mlstm",
    )(*args)
    n_new = stats[:, 0, :].reshape(nseq, M_HEADS, M_QK)
    m_new = stats[:, 1, 0].reshape(nseq, M_HEADS)
    return hm, c_new, n_new, m_new


def _rope(x, cos, sin_lo, sin_hi, half):
    n = x.shape[-1]
    return (x * cos + pltpu.roll(x, n - half, 1) * sin_lo + pltpu.roll(x, half, 1) * sin_hi)


def _hi_lo(x):
    hi = x.astype(_BF16).astype(_F32)
    return hi, x - hi


def _rope_kernel(aq_ref, ak_ref, av_ref, iq_ref, sm_ref, ta_ref, ti_ref,
                 q_out, k_out, kb_out, vb_out, iqx_out, ik_out, ikx_out):
    ca, sa_lo, sa_hi = ta_ref[0], ta_ref[1], ta_ref[2]
    ci, si_lo, si_hi = ti_ref[0], ti_ref[1], ti_ref[2]
    for h in range(A_HEADS):
        sl = slice(h * A_HD, (h + 1) * A_HD)
        q_out[:, sl] = _rope(aq_ref[:, sl], ca, sa_lo, sa_hi, A_ROT // 2).astype(q_out.dtype)
    for h in range(A_KV_HEADS):
        sl = slice(h * A_HD, (h + 1) * A_HD)
        kr = _rope(ak_ref[:, sl], ca, sa_lo, sa_hi, A_ROT // 2)
        k_out[:, sl] = kr
        kb_out[:, sl] = kr.astype(kb_out.dtype)
    vb_out[...] = av_ref[...].astype(vb_out.dtype)
    low = lax.broadcasted_iota(jnp.int32, (1, LANES), 1) < IDX_DIM
    for p in range(IDX_HEADS * IDX_DIM // LANES):
        x = _rope(iq_ref[:, p * LANES:(p + 1) * LANES], ci, si_lo, si_hi, IDX_ROT // 2)
        hi, lo = _hi_lo(x)
        hi_sw = pltpu.roll(hi, IDX_DIM, 1)
        lo_sw = pltpu.roll(lo, IDX_DIM, 1)
        c0 = 2 * p * IDX_XW
        iqx_out[:, c0:c0 + LANES] = jnp.where(low, hi, hi_sw).astype(iqx_out.dtype)
        iqx_out[:, c0 + LANES:c0 + 2 * LANES] = jnp.where(low, lo, 0.0).astype(iqx_out.dtype)
        iqx_out[:, c0 + 2 * LANES:c0 + 3 * LANES] = jnp.where(low, hi_sw, hi).astype(iqx_out.dtype)
        iqx_out[:, c0 + 3 * LANES:c0 + 4 * LANES] = jnp.where(low, lo_sw, 0.0).astype(iqx_out.dtype)
    ik = _rope(sm_ref[...], ci, si_lo, si_hi, IDX_ROT // 2)
    ik_out[...] = ik[:, SM_IK:SM_IK + IDX_DIM]
    hi, lo = _hi_lo(ik)
    ikx_out[:, 0:LANES] = jnp.where(low, hi, pltpu.roll(lo, IDX_DIM, 1)).astype(ikx_out.dtype)
    ikx_out[:, LANES:2 * LANES] = jnp.where(low, hi, 0.0).astype(ikx_out.dtype)


def _rope_tables(pos, rot, width, reps_valid):
    half = rot // 2
    inv_freq = jnp.exp(jnp.arange(half, dtype=_F32) * (-2.0 * math.log(ROPE_THETA) / rot))
    ang = pos.astype(_F32)[:, None] * inv_freq[None, :]
    cos, sin = jnp.cos(ang), jnp.sin(ang)
    m = pos.shape[0]
    one = jnp.ones((m, width - rot), _F32)
    zero = jnp.zeros((m, width - rot), _F32)
    zh = jnp.zeros((m, half), _F32)
    c_head = jnp.concatenate([cos, cos, one], axis=1)
    lo_head = jnp.concatenate([-sin, zh, zero], axis=1)
    hi_head = jnp.concatenate([zh, sin, zero], axis=1)
    reps = LANES // width
    ident = (jnp.ones((m, width), _F32), jnp.zeros((m, width), _F32), jnp.zeros((m, width), _F32))
    out = []
    for t, idt in zip((c_head, lo_head, hi_head), ident):
        out.append(jnp.concatenate([t if r < reps_valid else idt for r in range(reps)], axis=1))
    return jnp.stack(out)


def _rope_all(proj, small, pos):
    m = proj.shape[0]
    tm = _tile(m, 256, SUBLANES)
    ta = _rope_tables(pos, A_ROT, A_HD, 1)
    ti = _rope_tables(pos, IDX_ROT, IDX_DIM, LANES // IDX_DIM)
    wq, wk, wi = A_HEADS * A_HD, A_KV_HEADS * A_HD, IDX_HEADS * IDX_DIM
    wix = IDX_HEADS * IDX_XW
    out_shape = (jax.ShapeDtypeStruct((m, wq), _BF16),
                 jax.ShapeDtypeStruct((m, wk), _F32),
                 jax.ShapeDtypeStruct((m, wk), _BF16),
                 jax.ShapeDtypeStruct((m, wk), _BF16),
                 jax.ShapeDtypeStruct((m, wix), _BF16),
                 jax.ShapeDtypeStruct((m, IDX_DIM), _F32),
                 jax.ShapeDtypeStruct((m, IDX_XW), _BF16))
    return pl.pallas_call(
        _rope_kernel,
        out_shape=out_shape,
        grid=(m // tm,),
        in_specs=[pl.BlockSpec((tm, wq), lambda i: (i, B_AQ // wq)),
                  pl.BlockSpec((tm, wk), lambda i: (i, B_AK // wk)),
                  pl.BlockSpec((tm, wk), lambda i: (i, B_AV // wk)),
                  pl.BlockSpec((tm, wi), lambda i: (i, B_IQ // wi)),
                  pl.BlockSpec((tm, LANES), lambda i: (i, 0)),
                  pl.BlockSpec((3, tm, LANES), lambda i: (0, i, 0)),
                  pl.BlockSpec((3, tm, LANES), lambda i: (0, i, 0))],
        out_specs=(pl.BlockSpec((tm, wq), lambda i: (i, 0)),
                   pl.BlockSpec((tm, wk), lambda i: (i, 0)),
                   pl.BlockSpec((tm, wk), lambda i: (i, 0)),
                   pl.BlockSpec((tm, wk), lambda i: (i, 0)),
                   pl.BlockSpec((tm, wix), lambda i: (i, 0)),
                   pl.BlockSpec((tm, IDX_DIM), lambda i: (i, 0)),
                   pl.BlockSpec((tm, IDX_XW), lambda i: (i, 0))),
        compiler_params=_params(("parallel",)),
        name="rope",
    )(proj, proj, proj, proj, small, ta, ti)


def _attn_kernel(q_ref, iqx_ref, sm_ref, k_ref, v_ref, ikx_ref, o_ref, key_scr, bias_scr, w_scr,
                 *, TQ, TK, past_len, l_valid, topk):
    i = pl.program_id(1)
    q_start = past_len + i * TQ
    qpos = q_start + lax.broadcasted_iota(jnp.int32, (TQ, 1), 0)
    qchunk = qpos // CHUNK
    last_vis = jnp.minimum(((q_start + TQ - 1) // CHUNK + 1) * CHUNK, l_valid)
    nkt = (last_vis + TK - 1) // TK
    lane_pos = lax.broadcasted_iota(jnp.int32, (1, TK), 1)

    w_scale = (IDX_DIM ** -0.5) * (IDX_HEADS ** -0.5)
    for h in range(IDX_HEADS):
        w_scr[h] = jnp.broadcast_to(sm_ref[:, SM_IW + h:SM_IW + h + 1] * w_scale, (TQ, LANES))
    lane128 = lax.broadcasted_iota(jnp.int32, (1, LANES), 1)
    sc = min(TK, 2 * LANES)

    def score_tile(kt, carry):
        for cc in range(TK // sc):
            k0 = pl.multiple_of(kt * TK + cc * sc, sc)
            ik_c = ikx_ref[pl.ds(k0, sc), :]
            accs = [jnp.zeros((TQ, LANES), _F32) for _ in range(sc // LANES)]
            for h in range(IDX_HEADS):
                isc = _dot_nt(iqx_ref[:, h * IDX_XW:(h + 1) * IDX_XW], ik_c)
                w_h = w_scr[h]
                for c in range(sc // LANES):
                    accs[c] = accs[c] + jnp.maximum(isc[:, c * LANES:(c + 1) * LANES], 0.0) * w_h
            for c in range(sc // LANES):
                bits = pltpu.bitcast(accs[c] + 0.0, jnp.int32)
                key = jnp.where(bits < 0, bits ^ 0x7FFFFFFF, bits)
                kpos = k0 + c * LANES + lane128
                vis = ((kpos // CHUNK) <= qchunk) & (kpos < l_valid)
                col = cc * sc + c * LANES
                key_scr[kt, :, col:col + LANES] = jnp.where(vis, key, INT_MIN)
        return carry

    lax.fori_loop(0, nkt, score_tile, 0)

    def lane_fold(x):
        out = x[:, 0:LANES]
        for c in range(1, TK // LANES):
            out = out + x[:, c * LANES:(c + 1) * LANES]
        return out

    def count(pred_fn):
        def body(kt, part):
            return part + lane_fold(pred_fn(key_scr[kt], kt).astype(jnp.int32))
        part = lax.fori_loop(0, nkt, body, jnp.zeros((TQ, LANES), jnp.int32))
        return jnp.sum(part, axis=1, keepdims=True)

    def bit_step(it, thr_u):
        cand_u = thr_u | lax.shift_left(jnp.int32(1), 31 - it)
        cand_s = cand_u ^ INT_MIN
        cnt = count(lambda key, kt: key >= cand_s)
        return jnp.where(cnt >= topk, cand_u, thr_u)

    thr = lax.fori_loop(0, 32, bit_step, jnp.zeros((TQ, 1), jnp.int32)) ^ INT_MIN
    n_ge = count(lambda key, kt: key >= thr)

    def tie_search(_):
        need = topk - count(lambda key, kt: key > thr)
        nbits = (key_scr.shape[0] * TK).bit_length()

        def pos_step(it, p):
            cand = p | lax.shift_left(jnp.int32(1), nbits - 1 - it)
            cnt = count(lambda key, kt: (key == thr) & ((kt * TK + lane_pos) < cand))
            return jnp.where(cnt < need, cand, p)

        return lax.fori_loop(0, nbits, pos_step, jnp.zeros((TQ, 1), jnp.int32))

    has_tie = jnp.max(n_ge.astype(_F32)) > topk
    p_last = lax.cond(has_tie, tie_search, lambda _: jnp.full((TQ, 1), 2 ** 31 - 1, jnp.int32), 0)

    def bias_tile(kt, carry):
        key = key_scr[kt]
        sel = ((key > thr) | ((key == thr) & ((kt * TK + lane_pos) <= p_last))) & (key > KEY_NEG_INF)
        bias_scr[kt] = jnp.where(sel, 0.0, NEG_BIG)
        return carry

    lax.fori_loop(0, nkt, bias_tile, 0)

    scale = (A_HD ** -0.5) * math.log2(math.e)
    for g in range(A_KV_HEADS):
        qg = jnp.concatenate(
            [q_ref[:, (g * A_GROUP + j) * A_HD:(g * A_GROUP + j + 1) * A_HD] for j in range(A_GROUP)], axis=0)

        def attn_tile(kt, carry, g=g, qg=qg):
            m_i, l_i, acc = carry
            k0 = pl.multiple_of(kt * TK, TK)
            k_t = k_ref[pl.ds(k0, TK), g * A_HD:(g + 1) * A_HD]
            v_t = v_ref[pl.ds(k0, TK), g * A_HD:(g + 1) * A_HD]
            s = _dot_nt(qg, k_t).reshape(A_GROUP, TQ, TK) * scale + bias_scr[kt][None]
            m_new = jnp.maximum(m_i, jnp.max(s, axis=-1, keepdims=True))
            alpha = jnp.exp2(m_i - m_new)
            p = jnp.exp2(s - m_new)
            l_new = alpha * l_i + jnp.sum(p, axis=-1, keepdims=True)
            pv = _dot(p.reshape(A_GROUP * TQ, TK).astype(_BF16), v_t).reshape(A_GROUP, TQ, A_HD)
            return m_new, l_new, alpha * acc + pv

        init = (jnp.full((A_GROUP, TQ, 1), NEG_BIG, _F32), jnp.zeros((A_GROUP, TQ, 1), _F32),
                jnp.zeros((A_GROUP, TQ, A_HD), _F32))
        _, l_f, acc_f = lax.fori_loop(0, nkt, attn_tile, init)
        out = acc_f / l_f
        for j in range(A_GROUP):
            hh = g * A_GROUP + j
            o_ref[:, hh * A_HD:(hh + 1) * A_HD] = out[j].astype(o_ref.dtype)


def _attention(q_rot, iq_rot, small, k_all, v_all, ik_all, row0, nseq, T, TQ, TK, lp, past_len, l_valid):
    nq = T // TQ
    qb0 = row0 // TQ
    topk = min(TOPK_MAX, l_valid // 4)
    wq, wk, wix = A_HEADS * A_HD, A_KV_HEADS * A_HD, IDX_HEADS * IDX_XW
    return pl.pallas_call(
        functools.partial(_attn_kernel, TQ=TQ, TK=TK, past_len=past_len, l_valid=l_valid, topk=topk),
        out_shape=jax.ShapeDtypeStruct((nseq * T, wq), _BF16),
        grid=(nseq, nq),
        in_specs=[pl.BlockSpec((TQ, wq), lambda b, i: (qb0 + b * nq + i, 0)),
                  pl.BlockSpec((TQ, wix), lambda b, i: (qb0 + b * nq + i, 0)),
                  pl.BlockSpec((TQ, LANES), lambda b, i: (qb0 + b * nq + i, 0)),
                  pl.BlockSpec((lp, wk), lambda b, i: (b, 0)),
                  pl.BlockSpec((lp, wk), lambda b, i: (b, 0)),
                  pl.BlockSpec((lp, IDX_XW), lambda b, i: (b, 0))],
        out_specs=pl.BlockSpec((TQ, wq), lambda b, i: (b * nq + i, 0)),
        scratch_shapes=[pltpu.VMEM((lp // TK, TQ, TK), jnp.int32),
                        pltpu.VMEM((lp // TK, TQ, TK), _F32),
                        pltpu.VMEM((IDX_HEADS, TQ, LANES), _F32)],
        compiler_params=_params(("parallel", "arbitrary")),
        name="sparse_attention",
    )(q_rot, iq_rot, small, k_all, v_all, ik_all)


def _merge_kernel(hm_ref, ha_ref, wm_ref, wa_ref, gm_ref, ga_ref, o_ref):
    pm = _dot(hm_ref[...], wm_ref[...])
    pa = _dot(ha_ref[...], wa_ref[...])
    o_ref[...] = (jax.nn.sigmoid(gm_ref[...]) * pm + jax.nn.sigmoid(ga_ref[...]) * pa).astype(o_ref.dtype)


def _merge(hm, ha, wm, wa, proj, d):
    m = hm.shape[0]
    tm = _tile(m, 512, SUBLANES)
    tn = _tile(d, 1024)
    gm0 = 0
    ga0 = d // tn
    km, ka = hm.shape[1], ha.shape[1]
    return pl.pallas_call(
        _merge_kernel,
        out_shape=jax.ShapeDtypeStruct((m, d), _BF16),
        grid=(d // tn, m // tm),
        in_specs=[pl.BlockSpec((tm, km), lambda j, i: (i, 0)),
                  pl.BlockSpec((tm, ka), lambda j, i: (i, 0)),
                  pl.BlockSpec((km, tn), lambda j, i: (0, j)),
                  pl.BlockSpec((ka, tn), lambda j, i: (0, j)),
                  pl.BlockSpec((tm, tn), lambda j, i: (i, gm0 + j)),
                  pl.BlockSpec((tm, tn), lambda j, i: (i, ga0 + j))],
        out_specs=pl.BlockSpec((tm, tn), lambda j, i: (i, j)),
        compiler_params=_params(("parallel", "parallel")),
        name="gated_merge",
    )(hm, ha, wm, wa, proj, proj)


def _resnorm_kernel(x_ref, y_ref, g_ref, *out_refs_and_g2, with_next):
    y = y_ref[...]
    ms = jnp.mean(y * y, axis=-1, keepdims=True)
    x1 = x_ref[...] + y * lax.rsqrt(ms + NORM_EPS) * g_ref[...]
    if with_next:
        g2_ref, x1_ref, h_ref = out_refs_and_g2
        x1_ref[...] = x1
        ms1 = jnp.mean(x1 * x1, axis=-1, keepdims=True)
        h_ref[...] = (x1 * lax.rsqrt(ms1 + NORM_EPS) * g2_ref[...]).astype(h_ref.dtype)
    else:
        (x1_ref,) = out_refs_and_g2
        x1_ref[...] = x1


def _resnorm(x, y, g, g_next=None):
    m, d = x.shape
    tm = _tile(m, 256, SUBLANES)
    row = pl.BlockSpec((tm, d), lambda i: (i, 0))
    vec = pl.BlockSpec((1, d), lambda i: (0, 0))
    with_next = g_next is not None
    if with_next:
        return pl.pallas_call(
            functools.partial(_resnorm_kernel, with_next=True),
            out_shape=(jax.ShapeDtypeStruct((m, d), _F32), jax.ShapeDtypeStruct((m, d), _BF16)),
            grid=(m // tm,), in_specs=[row, row, vec, vec], out_specs=(row, row),
            compiler_params=_params(("parallel",)), name="resnorm_next",
        )(x, y, g.reshape(1, d), g_next.reshape(1, d))
    return pl.pallas_call(
        functools.partial(_resnorm_kernel, with_next=False),
        out_shape=jax.ShapeDtypeStruct((m, d), _F32),
        grid=(m // tm,), in_specs=[row, row, vec], out_specs=row,
        compiler_params=_params(("parallel",)), name="resnorm",
    )(x, y, g.reshape(1, d))


def _ffn_up_kernel(*refs, tm, seq_len, has_init):
    if has_init:
        h_ref, wg_ref, wl_ref, cw_ref, cb_ref, init_ref, z_ref, tail_ref, g_scr = refs
    else:
        h_ref, wg_ref, wl_ref, cw_ref, cb_ref, z_ref, tail_ref, g_scr = refs
    i = pl.program_id(1)
    tn = z_ref.shape[1]
    halo = SUBLANES

    h = h_ref[...]
    gate = _dot(h, wg_ref[...])
    lin = _dot(h, wl_ref[...])
    g_scr[halo:halo + tm, :] = gate
    if has_init:
        @pl.when(i == 0)
        def _():
            g_scr[0:halo, :] = jnp.zeros((halo, tn), _F32)

        prev1 = g_scr[halo - 1:halo - 1 + tm, :]
        prev2 = g_scr[halo - 2:halo - 2 + tm, :]
        nseg = tm // seq_len
        init0 = jnp.concatenate([jnp.broadcast_to(init_ref[s, 0:1, :], (seq_len, tn)) for s in range(nseg)], axis=0)
        init1 = jnp.concatenate([jnp.broadcast_to(init_ref[s, 1:2, :], (seq_len, tn)) for s in range(nseg)], axis=0)
        t = lax.broadcasted_iota(jnp.int32, (tm, 1), 0) % seq_len
        prev1 = jnp.where(t == 0, init1, prev1)
        prev2 = jnp.where(t == 0, init0, jnp.where(t == 1, init1, prev2))
    else:
        @pl.when((i * tm) % seq_len == 0)
        def _():
            g_scr[0:halo, :] = jnp.zeros((halo, tn), _F32)

        prev1 = g_scr[halo - 1:halo - 1 + tm, :]
        prev2 = g_scr[halo - 2:halo - 2 + tm, :]
    gc = cw_ref[0:1, :] * prev2 + cw_ref[1:2, :] * prev1 + cw_ref[2:3, :] * gate + cb_ref[...]
    c = -2.0 * math.sqrt(2.0 / math.pi)
    act = gc / (1.0 + jnp.exp(gc * (c + (c * 0.044715) * (gc * gc))))
    z_ref[...] = (act * lin).astype(z_ref.dtype)
    g_scr[0:halo, :] = gate[tm - halo:tm, :]
    seg = min(tm, seq_len)
    for s in range(tm // seg):
        tail_ref[s * SUBLANES:(s + 1) * SUBLANES, :] = gate[(s + 1) * seg - SUBLANES:(s + 1) * seg, :]


def _ffn_up(hf, w_up, conv_w, conv_b, row0, nrows, seq_len, init):
    d = hf.shape[1]
    dff = w_up.shape[1] // 2
    has_init = init is not None
    tm = _tile(nrows, 512, SUBLANES)
    if has_init:
        assert tm % seq_len == 0
    else:
        assert seq_len % tm == 0
    tn = _tile(dff, 256)
    nj = dff // tn
    seg = min(tm, seq_len)
    rb0 = row0 // tm
    ntail = tm // seg * SUBLANES
    in_specs = [pl.BlockSpec((tm, d), lambda j, i: (rb0 + i, 0)),
                pl.BlockSpec((d, tn), lambda j, i: (0, j)),
                pl.BlockSpec((d, tn), lambda j, i: (0, nj + j)),
                pl.BlockSpec((CONV_W, tn), lambda j, i: (0, j)),
                pl.BlockSpec((1, tn), lambda j, i: (0, j))]
    args = [hf, w_up, w_up, conv_w, conv_b.reshape(1, dff)]
    if has_init:
        nseg = tm // seq_len
        in_specs.append(pl.BlockSpec((nseg, CONV_W - 1, tn), lambda j, i: (i, 0, j)))
        args.append(init)
    z, tails = pl.pallas_call(
        functools.partial(_ffn_up_kernel, tm=tm, seq_len=seq_len, has_init=has_init),
        out_shape=(jax.ShapeDtypeStruct((nrows, dff), _BF16),
                   jax.ShapeDtypeStruct((nrows // tm * ntail, dff), _F32)),
        grid=(dff // tn, nrows // tm),
        in_specs=in_specs,
        out_specs=(pl.BlockSpec((tm, tn), lambda j, i: (i, j)),
                   pl.BlockSpec((ntail, tn), lambda j, i: (i, j))),
        scratch_shapes=[pltpu.VMEM((SUBLANES + tm, tn), _F32)],
        compiler_params=_params(("parallel", "arbitrary")),
        name="ffn_up_conv",
    )(*args)
    nseq = nrows // seq_len
    tails = tails.reshape(nrows // seg, SUBLANES, dff)
    segs_per_seq = seq_len // seg
    last = tails[segs_per_seq - 1::segs_per_seq, SUBLANES - (CONV_W - 1):, :]
    return z, last.reshape(nseq, CONV_W - 1, dff)


def _layer(x_all, geom, cache, state, w):
    (B, S, Bd, Td, P) = geom
    (cache_k, cache_v, cache_ik) = cache
    (state_C, state_n, state_m, state_conv) = state
    (w_in, b_igate, b_fgate, g_mhnorm, w_proj_m, w_proj_a, w_out,
     g_pre_mix, g_post_mix, g_pre_ffn, g_post_ffn, w_up, conv_w, conv_b, w_down) = w
    m_all, d = x_all.shape
    mp = B * S
    dff = w_down.shape[0]
    wk = A_KV_HEADS * A_HD

    c_mi = A_END
    c_aq = c_mi + 2 * M_HEADS
    c_ik = c_aq + B_END
    c_iw = c_ik + IDX_DIM
    c_gm = c_iw + IDX_HEADS
    w_a = w_in[:, :c_mi].astype(_BF16)
    w_b = w_in[:, c_aq:c_ik].astype(_BF16)
    w_c = w_in[:, c_gm:].astype(_BF16)
    w_small = jnp.concatenate([w_in[:, c_ik:c_iw], w_in[:, c_mi:c_aq], w_in[:, c_iw:c_gm],
                               jnp.zeros((d, LANES - SM_END), w_in.dtype)], axis=1).astype(_BF16)
    gate_bias = jnp.concatenate([jnp.zeros((SM_MI,), _F32), b_igate.astype(_F32), b_fgate.astype(_F32),
                                 jnp.zeros((LANES - SM_IW,), _F32)]).reshape(1, LANES)

    hn = _rmsnorm_cast(x_all, g_pre_mix)
    proj_a = _matmul(hn, w_a, _F32, tm=512, tn=1024, name="in_proj_mlstm")
    proj_b = _matmul(hn, w_b, _F32, tm=512, tn=1024, name="in_proj_attn")
    proj_c = _matmul(hn, w_c, _F32, tm=512, tn=1024, name="in_proj_gates")
    small = _matmul(hn, w_small, _F32, tm=512, tn=LANES, name="in_proj_small")

    lp_chunk = _tile(S, 256, CHUNK)
    hm_p, c_p, n_p, m_p = _mlstm(proj_a, small, gate_bias, g_mhnorm, 0, B, S, lp_chunk, None)
    s0 = jnp.zeros((Bd * M_HEADS, SUBLANES, M_QK), _F32)
    s0 = s0.at[:, 0, :].set(state_n.reshape(Bd * M_HEADS, M_QK).astype(_F32))
    s0 = s0.at[:, 1, :].set(jnp.broadcast_to(state_m.reshape(Bd * M_HEADS, 1).astype(_F32), (Bd * M_HEADS, M_QK)))
    hm_s, c_s, n_s, m_s = _mlstm(proj_a, small, gate_bias, g_mhnorm, mp, Bd, Td, min(CHUNK, Td),
                                 (state_C.astype(_F32), s0))
    hm = jnp.concatenate([hm_p, hm_s], axis=0)

    pos = jnp.concatenate([jnp.tile(jnp.arange(S, dtype=jnp.int32), B),
                           jnp.tile(P + jnp.arange(Td, dtype=jnp.int32), Bd)])
    q_rot, k_rot, k_bf, v_bf, iqx, ik_rot, ikx = _rope_all(proj_b, small, pos)
    tq_p = _tile(S, 128, CHUNK)
    tk_p = _tile(S, 512)
    ha_p = _attention(q_rot, iqx, small, k_bf, v_bf, ikx, 0, B, S, tq_p, tk_p, S, 0, S)
    l_s = P + Td
    tk_s = 256
    lp_s = -(-l_s // tk_s) * tk_s
    pad = lp_s - l_s
    k_s = jnp.concatenate([cache_k.reshape(Bd, P, wk).astype(_BF16), k_bf[mp:].reshape(Bd, Td, wk),
                           jnp.zeros((Bd, pad, wk), _BF16)], axis=1).reshape(Bd * lp_s, wk)
    v_s = jnp.concatenate([cache_v.reshape(Bd, P, wk).astype(_BF16), v_bf[mp:].reshape(Bd, Td, wk),
                           jnp.zeros((Bd, pad, wk), _BF16)], axis=1).reshape(Bd * lp_s, wk)
    cik = cache_ik.astype(_F32)
    cik_hi = cik.astype(_BF16)
    cik_lo = (cik - cik_hi.astype(_F32)).astype(_BF16)
    cikx = jnp.concatenate([cik_hi, cik_lo, cik_hi, jnp.zeros_like(cik_hi)], axis=-1)
    ikx_s = jnp.concatenate([cikx, ikx[mp:].reshape(Bd, Td, IDX_XW),
                             jnp.zeros((Bd, pad, IDX_XW), _BF16)], axis=1).reshape(Bd * lp_s, IDX_XW)
    ha_s = _attention(q_rot, iqx, small, k_s, v_s, ikx_s, mp, Bd, Td, Td, tk_s, lp_s, P, l_s)
    ha = jnp.concatenate([ha_p, ha_s], axis=0)

    mix = _merge(hm, ha, w_proj_m.astype(_BF16), w_proj_a.astype(_BF16), proj_c, d)
    y1 = _matmul(mix, w_out.astype(_BF16), _F32, tm=512, tn=1024, name="out_proj")
    x1, hf = _resnorm(x_all, y1, g_post_mix, g_pre_ffn)

    w_up_b = w_up.astype(_BF16)
    z_p, conv_p = _ffn_up(hf, w_up_b, conv_w, conv_b, 0, mp, S, None)
    z_s, conv_s = _ffn_up(hf, w_up_b, conv_w, conv_b, mp, Bd * Td, Td, state_conv.astype(_F32))
    z = jnp.concatenate([z_p, z_s], axis=0)
    y2 = _matmul(z, w_down.astype(_BF16), _F32, tm=512, tn=1024, tk=dff // 2, name="down_proj")
    x2 = _resnorm(x1, y2, g_post_ffn)

    av = proj_b[:, B_AV:B_AV + wk]
    outs_p = (k_rot[:mp].reshape(B, S, A_KV_HEADS, A_HD), av[:mp].reshape(B, S, A_KV_HEADS, A_HD),
              ik_rot[:mp].reshape(B, S, IDX_DIM), c_p, n_p, m_p, conv_p)
    outs_s = (k_rot[mp:].reshape(Bd, Td, A_KV_HEADS, A_HD), av[mp:].reshape(Bd, Td, A_KV_HEADS, A_HD),
              ik_rot[mp:].reshape(Bd, Td, IDX_DIM), c_s, n_s, m_s, conv_s)
    return x2, outs_p, outs_s


def kernel(x_prompt, x_sample, cache_k, cache_v, cache_idx_k, state_C, state_n, state_m, state_conv,
           w_in, b_igate, b_fgate, g_mhnorm, w_proj_m, w_proj_a, w_out,
           g_pre_mix, g_post_mix, g_pre_ffn, g_post_ffn, w_up, conv_w, conv_b, w_down):
    B, S, d = x_prompt.shape
    Bd, Td, _ = x_sample.shape
    P = cache_k.shape[2]
    depth = w_in.shape[0]
    mp = B * S
    x_all = jnp.concatenate([x_prompt.reshape(mp, d), x_sample.reshape(Bd * Td, d)], axis=0)
    all_p, all_s = [], []
    for l in range(depth):
        w = (w_in[l], b_igate[l], b_fgate[l], g_mhnorm[l], w_proj_m[l], w_proj_a[l], w_out[l],
             g_pre_mix[l], g_post_mix[l], g_pre_ffn[l], g_post_ffn[l], w_up[l], conv_w[l], conv_b[l], w_down[l])
        x_all, outs_p, outs_s = _layer(x_all, (B, S, Bd, Td, P), (cache_k[l], cache_v[l], cache_idx_k[l]),
                                       (state_C[l], state_n[l], state_m[l], state_conv[l]), w)
        all_p.append(outs_p)
        all_s.append(outs_s)

    def stk(outs, i):
        return jnp.stack([o[i] for o in outs])

    yp = x_all[:mp].reshape(B, S, d)
    ys = x_all[mp:].reshape(Bd, Td, d)
    return (yp, ys) + tuple(stk(all_p, i) for i in range(7)) + tuple(stk(all_s, i) for i in range(7))
```

```python
import functools
import math

import jax
import jax.numpy as jnp
from jax import lax
from jax.experimental import pallas as pl
from jax.experimental.pallas import tpu as pltpu

CHUNK = 64
NORM_EPS = 1e-6
ROPE_THETA = 500000.0
M_HEADS = 8
M_QK = 128
M_V = 256
A_HEADS = 16
A_KV_HEADS = 4
A_GROUP = A_HEADS // A_KV_HEADS
A_HD = 128
A_ROT = A_HD // 4
IDX_HEADS = 16
IDX_DIM = 64
IDX_ROT = IDX_DIM // 4
TOPK_MAX = 256
CONV_W = 3

LANES = 128
SUBLANES = 8
VMEM_LIMIT_BYTES = 52 * 1024 * 1024

A_MQ = 0
A_MK = A_MQ + M_HEADS * M_QK
A_MV = A_MK + M_HEADS * M_QK
A_MO = A_MV + M_HEADS * M_V
A_END = A_MO + M_HEADS * M_V
B_AQ = 0
B_AK = B_AQ + A_HEADS * A_HD
B_AV = B_AK + A_KV_HEADS * A_HD
B_IQ = B_AV + A_KV_HEADS * A_HD
B_END = B_IQ + IDX_HEADS * IDX_DIM
SM_IK = 0
SM_MI = SM_IK + IDX_DIM
SM_MF = SM_MI + M_HEADS
SM_IW = SM_MF + M_HEADS
SM_END = SM_IW + IDX_HEADS
IDX_XW = 4 * IDX_DIM

INT_MIN = -2 ** 31
KEY_NEG_INF = INT_MIN + 0x7FFFFF
NEG_BIG = -1e30

_BF16 = jnp.bfloat16
_F32 = jnp.float32


def _tile(dim, target, quantum=LANES):
    if dim <= target:
        return dim
    t = (target // quantum) * quantum
    while t >= quantum:
        if dim % t == 0:
            return t
        t -= quantum
    return dim


def _params(sem):
    return pltpu.CompilerParams(dimension_semantics=sem, vmem_limit_bytes=VMEM_LIMIT_BYTES)


def _dot(a, b):
    return jnp.dot(a, b, preferred_element_type=_F32)


def _dot_nt(a, b):
    return lax.dot_general(a, b, (((1,), (1,)), ((), ())), preferred_element_type=_F32)


def _dot_tn(a, b):
    return lax.dot_general(a, b, (((0,), (0,)), ((), ())), preferred_element_type=_F32)


def _w_in_prep_kernel(w_ref, a_ref, b_ref, c_ref, *, c_b, c_c):
    a_ref[...] = w_ref[:, 0:a_ref.shape[1]].astype(a_ref.dtype)
    b_ref[...] = w_ref[:, c_b:c_b + b_ref.shape[1]].astype(b_ref.dtype)
    c_ref[...] = w_ref[:, c_c:c_c + c_ref.shape[1]].astype(c_ref.dtype)


def _w_in_prep(w_in, c_b, c_c):
    d, d_in = w_in.shape
    tr = _tile(d, 128, SUBLANES)
    widths = (A_END, B_END, d_in - c_c)
    return pl.pallas_call(
        functools.partial(_w_in_prep_kernel, c_b=c_b, c_c=c_c),
        out_shape=tuple(jax.ShapeDtypeStruct((d, w), _BF16) for w in widths),
        grid=(d // tr,),
        in_specs=[pl.BlockSpec((tr, d_in), lambda i: (i, 0))],
        out_specs=tuple(pl.BlockSpec((tr, w), lambda i: (i, 0)) for w in widths),
        compiler_params=_params(("parallel",)),
        name="w_in_prep",
    )(w_in)


def _two_source_specs(tm, d, n_first):
    return (pl.BlockSpec((tm, d), lambda i: (jnp.minimum(i, n_first - 1), 0)),
            pl.BlockSpec((tm, d), lambda i: (jnp.maximum(i - n_first, 0), 0)))


def _rmsnorm_kernel(xa_ref, xb_ref, g_ref, o_ref, *, n_first):
    def body(x_ref):
        x = x_ref[...]
        ms = jnp.mean(x * x, axis=-1, keepdims=True)
        o_ref[...] = (x * lax.rsqrt(ms + NORM_EPS) * g_ref[...]).astype(o_ref.dtype)

    i = pl.program_id(0)
    pl.when(i < n_first)(lambda: body(xa_ref))
    pl.when(i >= n_first)(lambda: body(xb_ref))


def _row_tile(m_a, m_b):
    return _tile(math.gcd(m_a, m_b), 256, SUBLANES)


def _rmsnorm_cast(xa, xb, g):
    (m_a, d), m_b = xa.shape, xb.shape[0]
    tm = _row_tile(m_a, m_b)
    n_first = m_a // tm
    return pl.pallas_call(
        functools.partial(_rmsnorm_kernel, n_first=n_first),
        out_shape=jax.ShapeDtypeStruct((m_a + m_b, d), _BF16),
        grid=((m_a + m_b) // tm,),
        in_specs=[*_two_source_specs(tm, d, n_first), pl.BlockSpec((1, d), lambda i: (0, 0))],
        out_specs=pl.BlockSpec((tm, d), lambda i: (i, 0)),
        compiler_params=_params(("parallel",)),
        name="rmsnorm_cast",
    )(xa, xb, g.reshape(1, d))


def _mm_kernel(a_ref, b_ref, o_ref, acc_ref, *, nk):
    k = pl.program_id(2)

    @pl.when(k == 0)
    def _():
        acc_ref[...] = jnp.zeros_like(acc_ref)

    acc_ref[...] += _dot(a_ref[...], b_ref[...])

    @pl.when(k == nk - 1)
    def _():
        o_ref[...] = acc_ref[...].astype(o_ref.dtype)


def _mm1_kernel(a_ref, b_ref, o_ref):
    o_ref[...] = _dot(a_ref[...], b_ref[...]).astype(o_ref.dtype)


def _matmul(a, b, out_dtype, *, tm=512, tn=1024, tk=None, name="matmul"):
    m, kd = a.shape
    _, n = b.shape
    tm = _tile(m, tm, SUBLANES)
    tn = _tile(n, tn)
    tk = kd if tk is None else _tile(kd, tk)
    nk = kd // tk
    if nk == 1:
        return pl.pallas_call(
            _mm1_kernel,
            out_shape=jax.ShapeDtypeStruct((m, n), out_dtype),
            grid=(n // tn, m // tm),
            in_specs=[pl.BlockSpec((tm, kd), lambda j, i: (i, 0)),
                      pl.BlockSpec((kd, tn), lambda j, i: (0, j))],
            out_specs=pl.BlockSpec((tm, tn), lambda j, i: (i, j)),
            compiler_params=_params(("parallel", "parallel")),
            name=name,
        )(a, b)
    return pl.pallas_call(
        functools.partial(_mm_kernel, nk=nk),
        out_shape=jax.ShapeDtypeStruct((m, n), out_dtype),
        grid=(n // tn, m // tm, nk),
        in_specs=[pl.BlockSpec((tm, tk), lambda j, i, k: (i, k)),
                  pl.BlockSpec((tk, tn), lambda j, i, k: (k, j))],
        out_specs=pl.BlockSpec((tm, tn), lambda j, i, k: (i, j)),
        scratch_shapes=[pltpu.VMEM((tm, tn), _F32)],
        compiler_params=_params(("parallel", "parallel", "arbitrary")),
        name=name,
    )(a, b)


def _mlstm_kernel(*refs, L, has_state, has_alias):
    q_ref, k_ref, v_ref, o_ref, g_ref, gb_ref, gn_ref = refs[:7]
    if has_state:
        c0_ref, s0_ref = refs[7:9]
    h_ref, c_out_ref, s_out_ref, c_scr, n_scr, m_scr = refs[7 + 2 * has_state + has_alias:]
    head = pl.program_id(1)
    c = pl.program_id(2)
    nc = pl.num_programs(2)

    @pl.when(c == 0)
    def _():
        if has_state:
            c_scr[...] = c0_ref[0, 0]
            n_scr[...] = s0_ref[0, 0:1, :]
            m_scr[...] = s0_ref[0, 1:2, :]
        else:
            c_scr[...] = jnp.zeros_like(c_scr)
            n_scr[...] = jnp.zeros_like(n_scr)
            m_scr[...] = jnp.zeros_like(m_scr)

    gates = g_ref[...] + gb_ref[...]
    lane = lax.broadcasted_iota(jnp.int32, gates.shape, 1)
    ig_col = jnp.sum(jnp.where(lane == SM_MI + head, gates, 0.0), axis=1, keepdims=True)
    mf_col = jnp.sum(jnp.where(lane == SM_MF + head, gates, 0.0), axis=1, keepdims=True)
    lf_col = jnp.minimum(mf_col, 0.0) - jnp.log1p(jnp.exp(-jnp.abs(mf_col)))

    ri = lax.broadcasted_iota(jnp.int32, (L, L), 0)
    ci = lax.broadcasted_iota(jnp.int32, (L, L), 1)
    eye = ri == ci
    tril = ci <= ri
    lf_row = jnp.sum(jnp.where(eye, lf_col, 0.0), axis=0, keepdims=True)
    ig_row = jnp.sum(jnp.where(eye, ig_col, 0.0), axis=0, keepdims=True)
    b_col = jnp.sum(jnp.where(tril, lf_row, 0.0), axis=1, keepdims=True)
    b_row = jnp.sum(jnp.where(ri <= ci, lf_col, 0.0), axis=0, keepdims=True)
    logw = jnp.where(tril, b_col - b_row + ig_row, -jnp.inf)

    m_prev = m_scr[:, 0:1]
    inter = b_col + m_prev
    m_t = jnp.maximum(inter, jnp.max(logw, axis=1, keepdims=True))
    a = jnp.exp(inter - m_t)
    sw = jnp.exp(logw - m_t)

    q = q_ref[...]
    k = k_ref[...] * (M_QK ** -0.5)
    v = v_ref[...]
    qb = q.astype(_BF16)
    kb = k.astype(_BF16)
    s = _dot_nt(qb, kb) * sw
    c_old = c_scr[...]
    n_old = n_scr[...]
    num = a * _dot_nt(qb, c_old.astype(_BF16)) + _dot(s.astype(_BF16), v.astype(_BF16))
    den = a * jnp.sum(q * n_old, axis=1, keepdims=True) + jnp.sum(s, axis=1, keepdims=True)
    h = num / jnp.maximum(jnp.abs(den), jnp.exp(-m_t))

    m_new = m_t[L - 1:L, :]
    b_last = b_col[L - 1:L, :]
    g_col = jnp.exp(b_last - b_col + ig_col - m_new)
    decay = jnp.exp(b_last + m_prev - m_new)
    c_new = decay * c_old + _dot_tn((v * g_col).astype(_BF16), kb)
    n_new = decay * n_old + jnp.sum(g_col * k, axis=0, keepdims=True)
    c_scr[...] = c_new
    n_scr[...] = n_new
    m_scr[...] = jnp.broadcast_to(m_new, m_scr.shape)

    ms = jnp.mean(h * h, axis=1, keepdims=True)
    y = h * lax.rsqrt(ms + NORM_EPS) * gn_ref[0]
    h_ref[...] = (y * jax.nn.sigmoid(o_ref[...])).astype(h_ref.dtype)

    @pl.when(c == nc - 1)
    def _():
        c_out_ref[0, 0] = c_new
        s_out_ref[0] = jnp.zeros(s_out_ref.shape[1:], _F32)
        s_out_ref[0, 0:1, :] = n_new
        s_out_ref[0, 1:2, :] = jnp.broadcast_to(m_new, (1, M_QK))


def _mlstm(proj, small, gate_bias, g_mhnorm, row0, nseq, T, L, state, h_prev=None):
    nc = T // L
    rb0 = row0 // L
    has_state = state is not None
    has_alias = h_prev is not None

    def rows(b, h, c):
        return rb0 + b * nc + c

    in_specs = [
        pl.BlockSpec((L, M_QK), lambda b, h, c: (rows(b, h, c), A_MQ // M_QK + h)),
        pl.BlockSpec((L, M_QK), lambda b, h, c: (rows(b, h, c), A_MK // M_QK + h)),
        pl.BlockSpec((L, M_V), lambda b, h, c: (rows(b, h, c), A_MV // M_V + h)),
        pl.BlockSpec((L, M_V), lambda b, h, c: (rows(b, h, c), A_MO // M_V + h)),
        pl.BlockSpec((L, LANES), lambda b, h, c: (rows(b, h, c), 0)),
        pl.BlockSpec((1, LANES), lambda b, h, c: (0, 0)),
        pl.BlockSpec((1, 1, M_V), lambda b, h, c: (h, 0, 0)),
    ]
    args = [proj, proj, proj, proj, small, gate_bias, g_mhnorm.reshape(M_HEADS, 1, M_V)]
    if has_state:
        c0, s0 = state
        in_specs += [pl.BlockSpec((1, 1, M_V, M_QK), lambda b, h, c: (b, h, 0, 0)),
                     pl.BlockSpec((1, SUBLANES, M_QK), lambda b, h, c: (b * M_HEADS + h, 0, 0))]
        args += [c0, s0]
    aliases = {}
    if has_alias:
        aliases = {len(args): 0}
        in_specs.append(pl.BlockSpec(memory_space=pl.ANY))
        args.append(h_prev)
    out_shape = (jax.ShapeDtypeStruct((proj.shape[0], M_HEADS * M_V), _BF16),
                 jax.ShapeDtypeStruct((nseq, M_HEADS, M_V, M_QK), _F32),
                 jax.ShapeDtypeStruct((nseq * M_HEADS, SUBLANES, M_QK), _F32))
    out_specs = (pl.BlockSpec((L, M_V), lambda b, h, c: (rows(b, h, c), h)),
                 pl.BlockSpec((1, 1, M_V, M_QK), lambda b, h, c: (b, h, 0, 0)),
                 pl.BlockSpec((1, SUBLANES, M_QK), lambda b, h, c: (b * M_HEADS + h, 0, 0)))
    hm, c_new, stats = pl.pallas_call(
        functools.partial(_mlstm_kernel, L=L, has_state=has_state, has_alias=has_alias),
        out_shape=out_shape,
        grid=(nseq, M_HEADS, nc),
        in_specs=in_specs,
        out_specs=out_specs,
        scratch_shapes=[pltpu.VMEM((M_V, M_QK), _F32), pltpu.VMEM((1, M_QK), _F32),
                        pltpu.VMEM((1, M_QK), _F32)],
        input_output_aliases=aliases,
        compiler_params=_params(("parallel", "parallel", "arbitrary")),
        name="mlstm",
    )(*args)
    n_new = stats[:, 0, :].reshape(nseq, M_HEADS, M_QK)
    m_new = stats[:, 1, 0].reshape(nseq, M_HEADS)
    return hm, c_new, n_new, m_new


def _rope(x, cos, sin_lo, sin_hi, half):
    n = x.shape[-1]
    return (x * cos + pltpu.roll(x, n - half, 1) * sin_lo + pltpu.roll(x, half, 1) * sin_hi)


def _hi_lo(x):
    hi = x.astype(_BF16).astype(_F32)
    return hi, x - hi


def _rope_kernel(aq_ref, ak_ref, av_ref, iq_ref, sm_ref, ta_ref, ti_ref,
                 q_out, k_out, kb_out, vb_out, iqx_out, ik_out, ikx_out):
    ca, sa_lo, sa_hi = ta_ref[0], ta_ref[1], ta_ref[2]
    ci, si_lo, si_hi = ti_ref[0], ti_ref[1], ti_ref[2]
    for h in range(A_HEADS):
        sl = slice(h * A_HD, (h + 1) * A_HD)
        q_out[:, sl] = _rope(aq_ref[:, sl], ca, sa_lo, sa_hi, A_ROT // 2).astype(q_out.dtype)
    for h in range(A_KV_HEADS):
        sl = slice(h * A_HD, (h + 1) * A_HD)
        kr = _rope(ak_ref[:, sl], ca, sa_lo, sa_hi, A_ROT // 2)
        k_out[:, sl] = kr
        kb_out[:, sl] = kr.astype(kb_out.dtype)
    vb_out[...] = av_ref[...].astype(vb_out.dtype)
    low = lax.broadcasted_iota(jnp.int32, (1, LANES), 1) < IDX_DIM
    for p in range(IDX_HEADS * IDX_DIM // LANES):
        x = _rope(iq_ref[:, p * LANES:(p + 1) * LANES], ci, si_lo, si_hi, IDX_ROT // 2)
        hi, lo = _hi_lo(x)
        hi_sw = pltpu.roll(hi, IDX_DIM, 1)
        lo_sw = pltpu.roll(lo, IDX_DIM, 1)
        c0 = 2 * p * IDX_XW
        iqx_out[:, c0:c0 + LANES] = jnp.where(low, hi, hi_sw).astype(iqx_out.dtype)
        iqx_out[:, c0 + LANES:c0 + 2 * LANES] = jnp.where(low, lo, 0.0).astype(iqx_out.dtype)
        iqx_out[:, c0 + 2 * LANES:c0 + 3 * LANES] = jnp.where(low, hi_sw, hi).astype(iqx_out.dtype)
        iqx_out[:, c0 + 3 * LANES:c0 + 4 * LANES] = jnp.where(low, lo_sw, 0.0).astype(iqx_out.dtype)
    ik = _rope(sm_ref[...], ci, si_lo, si_hi, IDX_ROT // 2)
    ik_out[...] = ik[:, SM_IK:SM_IK + IDX_DIM]
    hi, lo = _hi_lo(ik)
    ikx_out[:, 0:LANES] = jnp.where(low, hi, pltpu.roll(lo, IDX_DIM, 1)).astype(ikx_out.dtype)
    ikx_out[:, LANES:2 * LANES] = jnp.where(low, hi, 0.0).astype(ikx_out.dtype)


def _rope_tables(pos, rot, width, reps_valid):
    half = rot // 2
    inv_freq = jnp.exp(jnp.arange(half, dtype=_F32) * (-2.0 * math.log(ROPE_THETA) / rot))
    ang = pos.astype(_F32)[:, None] * inv_freq[None, :]
    cos, sin = jnp.cos(ang), jnp.sin(ang)
    m = pos.shape[0]
    one = jnp.ones((m, width - rot), _F32)
    zero = jnp.zeros((m, width - rot), _F32)
    zh = jnp.zeros((m, half), _F32)
    c_head = jnp.concatenate([cos, cos, one], axis=1)
    lo_head = jnp.concatenate([-sin, zh, zero], axis=1)
    hi_head = jnp.concatenate([zh, sin, zero], axis=1)
    reps = LANES // width
    ident = (jnp.ones((m, width), _F32), jnp.zeros((m, width), _F32), jnp.zeros((m, width), _F32))
    out = []
    for t, idt in zip((c_head, lo_head, hi_head), ident):
        out.append(jnp.concatenate([t if r < reps_valid else idt for r in range(reps)], axis=1))
    return jnp.stack(out)


def _rope_all(proj, small, pos):
    m = proj.shape[0]
    tm = _tile(m, 256, SUBLANES)
    ta = _rope_tables(pos, A_ROT, A_HD, 1)
    ti = _rope_tables(pos, IDX_ROT, IDX_DIM, LANES // IDX_DIM)
    wq, wk, wi = A_HEADS * A_HD, A_KV_HEADS * A_HD, IDX_HEADS * IDX_DIM
    wix = IDX_HEADS * IDX_XW
    out_shape = (jax.ShapeDtypeStruct((m, wq), _BF16),
                 jax.ShapeDtypeStruct((m, wk), _F32),
                 jax.ShapeDtypeStruct((m, wk), _BF16),
                 jax.ShapeDtypeStruct((m, wk), _BF16),
                 jax.ShapeDtypeStruct((m, wix), _BF16),
                 jax.ShapeDtypeStruct((m, IDX_DIM), _F32),
                 jax.ShapeDtypeStruct((m, IDX_XW), _BF16))
    return pl.pallas_call(
        _rope_kernel,
        out_shape=out_shape,
        grid=(m // tm,),
        in_specs=[pl.BlockSpec((tm, wq), lambda i: (i, B_AQ // wq)),
                  pl.BlockSpec((tm, wk), lambda i: (i, B_AK // wk)),
                  pl.BlockSpec((tm, wk), lambda i: (i, B_AV // wk)),
                  pl.BlockSpec((tm, wi), lambda i: (i, B_IQ // wi)),
                  pl.BlockSpec((tm, LANES), lambda i: (i, 0)),
                  pl.BlockSpec((3, tm, LANES), lambda i: (0, i, 0)),
                  pl.BlockSpec((3, tm, LANES), lambda i: (0, i, 0))],
        out_specs=(pl.BlockSpec((tm, wq), lambda i: (i, 0)),
                   pl.BlockSpec((tm, wk), lambda i: (i, 0)),
                   pl.BlockSpec((tm, wk), lambda i: (i, 0)),
                   pl.BlockSpec((tm, wk), lambda i: (i, 0)),
                   pl.BlockSpec((tm, wix), lambda i: (i, 0)),
                   pl.BlockSpec((tm, IDX_DIM), lambda i: (i, 0)),
                   pl.BlockSpec((tm, IDX_XW), lambda i: (i, 0))),
        compiler_params=_params(("parallel",)),
        name="rope",
    )(proj, proj, proj, proj, small, ta, ti)


def _attn_kernel(*refs, TQ, TK, past_len, l_valid, topk):
    q_ref, iqx_ref, sm_ref, k_ref, v_ref, ikx_ref = refs[:6]
    o_ref, key_scr, bias_scr, w_scr = refs[-4:]
    i = pl.program_id(1)
    q_start = past_len + i * TQ
    qpos = q_start + lax.broadcasted_iota(jnp.int32, (TQ, 1), 0)
    qchunk = qpos // CHUNK
    last_vis = jnp.minimum(((q_start + TQ - 1) // CHUNK + 1) * CHUNK, l_valid)
    nkt = (last_vis + TK - 1) // TK
    lane_pos = lax.broadcasted_iota(jnp.int32, (1, TK), 1)

    w_scale = (IDX_DIM ** -0.5) * (IDX_HEADS ** -0.5)
    for h in range(IDX_HEADS):
        w_scr[h] = jnp.broadcast_to(sm_ref[:, SM_IW + h:SM_IW + h + 1] * w_scale, (TQ, LANES))
    lane128 = lax.broadcasted_iota(jnp.int32, (1, LANES), 1)
    sc = min(TK, 2 * LANES)

    def score_tile(kt, carry):
        for cc in range(TK // sc):
            k0 = pl.multiple_of(kt * TK + cc * sc, sc)
            ik_c = ikx_ref[pl.ds(k0, sc), :]
            accs = [jnp.zeros((TQ, LANES), _F32) for _ in range(sc // LANES)]
            for h in range(IDX_HEADS):
                isc = _dot_nt(iqx_ref[:, h * IDX_XW:(h + 1) * IDX_XW], ik_c)
                w_h = w_scr[h]
                for c in range(sc // LANES):
                    accs[c] = accs[c] + jnp.maximum(isc[:, c * LANES:(c + 1) * LANES], 0.0) * w_h
            for c in range(sc // LANES):
                bits = pltpu.bitcast(accs[c] + 0.0, jnp.int32)
                key = jnp.where(bits < 0, bits ^ 0x7FFFFFFF, bits)
                kpos = k0 + c * LANES + lane128
                vis = ((kpos // CHUNK) <= qchunk) & (kpos < l_valid)
                col = cc * sc + c * LANES
                key_scr[kt, :, col:col + LANES] = jnp.where(vis, key, INT_MIN)
        return carry

    lax.fori_loop(0, nkt, score_tile, 0)

    def lane_fold(x):
        out = x[:, 0:LANES]
        for c in range(1, TK // LANES):
            out = out + x[:, c * LANES:(c + 1) * LANES]
        return out

    def count(pred_fn):
        def body(kt, part):
            return part + lane_fold(pred_fn(key_scr[kt], kt).astype(jnp.int32))
        part = lax.fori_loop(0, nkt, body, jnp.zeros((TQ, LANES), jnp.int32))
        return jnp.sum(part, axis=1, keepdims=True)

    def bit_step(it, thr_u):
        cand_u = thr_u | lax.shift_left(jnp.int32(1), 31 - it)
        cand_s = cand_u ^ INT_MIN
        cnt = count(lambda key, kt: key >= cand_s)
        return jnp.where(cnt >= topk, cand_u, thr_u)

    thr = lax.fori_loop(0, 32, bit_step, jnp.zeros((TQ, 1), jnp.int32)) ^ INT_MIN
    n_ge = count(lambda key, kt: key >= thr)

    def tie_search(_):
        need = topk - count(lambda key, kt: key > thr)
        nbits = (key_scr.shape[0] * TK).bit_length()

        def pos_step(it, p):
            cand = p | lax.shift_left(jnp.int32(1), nbits - 1 - it)
            cnt = count(lambda key, kt: (key == thr) & ((kt * TK + lane_pos) < cand))
            return jnp.where(cnt < need, cand, p)

        return lax.fori_loop(0, nbits, pos_step, jnp.zeros((TQ, 1), jnp.int32))

    has_tie = jnp.max(n_ge.astype(_F32)) > topk
    p_last = lax.cond(has_tie, tie_search, lambda _: jnp.full((TQ, 1), 2 ** 31 - 1, jnp.int32), 0)

    def bias_tile(kt, carry):
        key = key_scr[kt]
        sel = ((key > thr) | ((key == thr) & ((kt * TK + lane_pos) <= p_last))) & (key > KEY_NEG_INF)
        bias_scr[kt] = jnp.where(sel, 0.0, NEG_BIG)
        return carry

    lax.fori_loop(0, nkt, bias_tile, 0)

    scale = (A_HD ** -0.5) * math.log2(math.e)
    for g in range(A_KV_HEADS):
        qg = jnp.concatenate(
            [q_ref[:, (g * A_GROUP + j) * A_HD:(g * A_GROUP + j + 1) * A_HD] for j in range(A_GROUP)], axis=0)

        def attn_tile(kt, carry, g=g, qg=qg):
            m_i, l_i, acc = carry
            k0 = pl.multiple_of(kt * TK, TK)
            k_t = k_ref[pl.ds(k0, TK), g * A_HD:(g + 1) * A_HD]
            v_t = v_ref[pl.ds(k0, TK), g * A_HD:(g + 1) * A_HD]
            s = _dot_nt(qg, k_t).reshape(A_GROUP, TQ, TK) * scale + bias_scr[kt][None]
            m_new = jnp.maximum(m_i, jnp.max(s, axis=-1, keepdims=True))
            alpha = jnp.exp2(m_i - m_new)
            p = jnp.exp2(s - m_new)
            l_new = alpha * l_i + jnp.sum(p, axis=-1, keepdims=True)
            pv = _dot(p.reshape(A_GROUP * TQ, TK).astype(_BF16), v_t).reshape(A_GROUP, TQ, A_HD)
            return m_new, l_new, alpha * acc + pv

        init = (jnp.full((A_GROUP, TQ, 1), NEG_BIG, _F32), jnp.zeros((A_GROUP, TQ, 1), _F32),
                jnp.zeros((A_GROUP, TQ, A_HD), _F32))
        _, l_f, acc_f = lax.fori_loop(0, nkt, attn_tile, init)
        out = acc_f / l_f
        for j in range(A_GROUP):
            hh = g * A_GROUP + j
            o_ref[:, hh * A_HD:(hh + 1) * A_HD] = out[j].astype(o_ref.dtype)


def _attention(q_rot, iq_rot, small, k_all, v_all, ik_all, row0, nseq, T, TQ, TK, lp, past_len, l_valid, o_prev=None):
    nq = T // TQ
    qb0 = row0 // TQ
    topk = min(TOPK_MAX, l_valid // 4)
    wq, wk, wix = A_HEADS * A_HD, A_KV_HEADS * A_HD, IDX_HEADS * IDX_XW
    in_specs = [pl.BlockSpec((TQ, wq), lambda b, i: (qb0 + b * nq + i, 0)),
                pl.BlockSpec((TQ, wix), lambda b, i: (qb0 + b * nq + i, 0)),
                pl.BlockSpec((TQ, LANES), lambda b, i: (qb0 + b * nq + i, 0)),
                pl.BlockSpec((lp, wk), lambda b, i: (b, 0)),
                pl.BlockSpec((lp, wk), lambda b, i: (b, 0)),
                pl.BlockSpec((lp, IDX_XW), lambda b, i: (b, 0))]
    args = [q_rot, iq_rot, small, k_all, v_all, ik_all]
    aliases = {}
    if o_prev is not None:
        aliases = {len(args): 0}
        in_specs.append(pl.BlockSpec(memory_space=pl.ANY))
        args.append(o_prev)
    return pl.pallas_call(
        functools.partial(_attn_kernel, TQ=TQ, TK=TK, past_len=past_len, l_valid=l_valid, topk=topk),
        out_shape=jax.ShapeDtypeStruct((q_rot.shape[0], wq), _BF16),
        grid=(nseq, nq),
        in_specs=in_specs,
        out_specs=pl.BlockSpec((TQ, wq), lambda b, i: (qb0 + b * nq + i, 0)),
        scratch_shapes=[pltpu.VMEM((lp // TK, TQ, TK), jnp.int32),
                        pltpu.VMEM((lp // TK, TQ, TK), _F32),
                        pltpu.VMEM((IDX_HEADS, TQ, LANES), _F32)],
        input_output_aliases=aliases,
        compiler_params=_params(("parallel", "arbitrary")),
        name="sparse_attention",
    )(*args)


def _merge_kernel(hm_ref, ha_ref, wm_ref, wa_ref, gm_ref, ga_ref, o_ref):
    pm = _dot(hm_ref[...], wm_ref[...])
    pa = _dot(ha_ref[...], wa_ref[...])
    o_ref[...] = (jax.nn.sigmoid(gm_ref[...]) * pm + jax.nn.sigmoid(ga_ref[...]) * pa).astype(o_ref.dtype)


def _merge(hm, ha, wm, wa, proj, d):
    m = hm.shape[0]
    tm = _tile(m, 512, SUBLANES)
    tn = _tile(d, 1024)
    gm0 = 0
    ga0 = d // tn
    km, ka = hm.shape[1], ha.shape[1]
    return pl.pallas_call(
        _merge_kernel,
        out_shape=jax.ShapeDtypeStruct((m, d), _BF16),
        grid=(d // tn, m // tm),
        in_specs=[pl.BlockSpec((tm, km), lambda j, i: (i, 0)),
                  pl.BlockSpec((tm, ka), lambda j, i: (i, 0)),
                  pl.BlockSpec((km, tn), lambda j, i: (0, j)),
                  pl.BlockSpec((ka, tn), lambda j, i: (0, j)),
                  pl.BlockSpec((tm, tn), lambda j, i: (i, gm0 + j)),
                  pl.BlockSpec((tm, tn), lambda j, i: (i, ga0 + j))],
        out_specs=pl.BlockSpec((tm, tn), lambda j, i: (i, j)),
        compiler_params=_params(("parallel", "parallel")),
        name="gated_merge",
    )(hm, ha, wm, wa, proj, proj)


def _resnorm_next_kernel(xa_ref, xb_ref, y_ref, g_ref, g2_ref, x1_ref, h_ref, *, n_first):
    def body(x_ref):
        y = y_ref[...]
        ms = jnp.mean(y * y, axis=-1, keepdims=True)
        x1 = x_ref[...] + y * lax.rsqrt(ms + NORM_EPS) * g_ref[...]
        x1_ref[...] = x1
        ms1 = jnp.mean(x1 * x1, axis=-1, keepdims=True)
        h_ref[...] = (x1 * lax.rsqrt(ms1 + NORM_EPS) * g2_ref[...]).astype(h_ref.dtype)

    i = pl.program_id(0)
    pl.when(i < n_first)(lambda: body(xa_ref))
    pl.when(i >= n_first)(lambda: body(xb_ref))


def _resnorm_next(xa, xb, y, g, g_next):
    m, d = y.shape
    tm = _row_tile(xa.shape[0], xb.shape[0])
    n_first = xa.shape[0] // tm
    row = pl.BlockSpec((tm, d), lambda i: (i, 0))
    vec = pl.BlockSpec((1, d), lambda i: (0, 0))
    return pl.pallas_call(
        functools.partial(_resnorm_next_kernel, n_first=n_first),
        out_shape=(jax.ShapeDtypeStruct((m, d), _F32), jax.ShapeDtypeStruct((m, d), _BF16)),
        grid=(m // tm,), in_specs=[*_two_source_specs(tm, d, n_first), row, vec, vec], out_specs=(row, row),
        compiler_params=_params(("parallel",)), name="resnorm_next",
    )(xa, xb, y, g.reshape(1, d), g_next.reshape(1, d))


def _resnorm_kernel(x_ref, y_ref, g_ref, o_ref):
    y = y_ref[...]
    ms = jnp.mean(y * y, axis=-1, keepdims=True)
    o_ref[...] = x_ref[...] + y * lax.rsqrt(ms + NORM_EPS) * g_ref[...]


def _resnorm(x, y, g):
    m, d = x.shape
    tm = _tile(m, 256, SUBLANES)
    row = pl.BlockSpec((tm, d), lambda i: (i, 0))
    vec = pl.BlockSpec((1, d), lambda i: (0, 0))
    return pl.pallas_call(
        _resnorm_kernel,
        out_shape=jax.ShapeDtypeStruct((m, d), _F32),
        grid=(m // tm,), in_specs=[row, row, vec], out_specs=row,
        compiler_params=_params(("parallel",)), name="resnorm",
    )(x, y, g.reshape(1, d))


HALO = SUBLANES


def _conv_gelu_gate(cw_ref, cb_ref, prev2, prev1, gate, lin):
    gc = cw_ref[0:1, :] * prev2 + cw_ref[1:2, :] * prev1 + cw_ref[2:3, :] * gate + cb_ref[...]
    c = -2.0 * math.sqrt(2.0 / math.pi)
    act = gc / (1.0 + jnp.exp(gc * (c + (c * 0.044715) * (gc * gc))))
    return act * lin


def _ffn_up_first_kernel(h_ref, wg_ref, wl_ref, cw_ref, cb_ref, z_ref, tail_ref, wg_bf, wl_bf, g_scr,
                         *, tm, seq_len):
    i = pl.program_id(1)
    tn = z_ref.shape[1]

    @pl.when(i == 0)
    def _():
        wg_bf[...] = wg_ref[...].astype(wg_bf.dtype)
        wl_bf[...] = wl_ref[...].astype(wl_bf.dtype)

    @pl.when((i * tm) % seq_len == 0)
    def _():
        g_scr[0:HALO, :] = jnp.zeros((HALO, tn), _F32)

    h = h_ref[...]
    gate = _dot(h, wg_bf[...])
    lin = _dot(h, wl_bf[...])
    g_scr[HALO:HALO + tm, :] = gate
    zed = _conv_gelu_gate(cw_ref, cb_ref, g_scr[HALO - 2:HALO - 2 + tm, :], g_scr[HALO - 1:HALO - 1 + tm, :],
                          gate, lin)
    z_ref[...] = zed.astype(z_ref.dtype)
    g_scr[0:HALO, :] = gate[tm - HALO:tm, :]
    tail_ref[...] = gate[tm - SUBLANES:tm, :]


def _ffn_up_state_kernel(h_ref, wg_ref, wl_ref, cw_ref, cb_ref, init_ref, z_any, z_ref, tail_ref, g_scr,
                         *, tm, seq_len):
    tn = z_ref.shape[1]
    h = h_ref[...]
    gate = _dot(h, wg_ref[...].astype(_BF16))
    lin = _dot(h, wl_ref[...].astype(_BF16))
    g_scr[0:HALO, :] = jnp.zeros((HALO, tn), _F32)
    g_scr[HALO:HALO + tm, :] = gate
    prev1 = g_scr[HALO - 1:HALO - 1 + tm, :]
    prev2 = g_scr[HALO - 2:HALO - 2 + tm, :]
    nseg = tm // seq_len
    init0 = jnp.concatenate([jnp.broadcast_to(init_ref[s, 0:1, :], (seq_len, tn)) for s in range(nseg)], axis=0)
    init1 = jnp.concatenate([jnp.broadcast_to(init_ref[s, 1:2, :], (seq_len, tn)) for s in range(nseg)], axis=0)
    t = lax.broadcasted_iota(jnp.int32, (tm, 1), 0) % seq_len
    prev1 = jnp.where(t == 0, init1, prev1)
    prev2 = jnp.where(t == 0, init0, jnp.where(t == 1, init1, prev2))
    z_ref[...] = _conv_gelu_gate(cw_ref, cb_ref, prev2, prev1, gate, lin).astype(z_ref.dtype)
    for s in range(nseg):
        tail_ref[s * SUBLANES:(s + 1) * SUBLANES, :] = gate[(s + 1) * seq_len - SUBLANES:(s + 1) * seq_len, :]


def _conv_state(tails, nseq, segs_per_seq, dff):
    tails = tails.reshape(nseq * segs_per_seq, SUBLANES, dff)
    last = tails[segs_per_seq - 1::segs_per_seq, SUBLANES - (CONV_W - 1):, :]
    return last.reshape(nseq, CONV_W - 1, dff)


def _ffn_up_specs(d, tn, nj, h_spec):
    return [h_spec,
            pl.BlockSpec((d, tn), lambda j, i: (0, j)),
            pl.BlockSpec((d, tn), lambda j, i: (0, nj + j)),
            pl.BlockSpec((CONV_W, tn), lambda j, i: (0, j)),
            pl.BlockSpec((1, tn), lambda j, i: (0, j))]


def _ffn_up_first(hf, w_up, conv_w, conv_b, nrows, seq_len):
    m_all, d = hf.shape
    dff = w_up.shape[1] // 2
    tm = _tile(seq_len, 512, SUBLANES)
    tn = _tile(dff, 256)
    nj = dff // tn
    nt = nrows // tm
    specs = _ffn_up_specs(d, tn, nj, pl.BlockSpec((tm, d), lambda j, i: (i, 0)))
    z, tails = pl.pallas_call(
        functools.partial(_ffn_up_first_kernel, tm=tm, seq_len=seq_len),
        out_shape=(jax.ShapeDtypeStruct((m_all, dff), _BF16),
                   jax.ShapeDtypeStruct((nt * SUBLANES, dff), _F32)),
        grid=(nj, nt),
        in_specs=specs,
        out_specs=(pl.BlockSpec((tm, tn), lambda j, i: (i, j)),
                   pl.BlockSpec((SUBLANES, tn), lambda j, i: (i, j))),
        scratch_shapes=[pltpu.VMEM((d, tn), _BF16), pltpu.VMEM((d, tn), _BF16),
                        pltpu.VMEM((HALO + tm, tn), _F32)],
        compiler_params=_params(("parallel", "arbitrary")),
        name="ffn_up_conv_first",
    )(hf, w_up, w_up, conv_w, conv_b.reshape(1, dff))
    return z, _conv_state(tails, nrows // seq_len, seq_len // tm, dff)


def _ffn_up_state(hf, w_up, conv_w, conv_b, row0, nrows, seq_len, init, z_prev):
    m_all, d = hf.shape
    dff = w_up.shape[1] // 2
    tm = _tile(nrows, 512, seq_len)
    tn = _tile(dff, 256)
    nj = dff // tn
    nseg = tm // seq_len
    rb0 = row0 // tm
    specs = _ffn_up_specs(d, tn, nj, pl.BlockSpec((tm, d), lambda j, i: (rb0 + i, 0)))
    specs += [pl.BlockSpec((nseg, CONV_W - 1, tn), lambda j, i: (i, 0, j)), pl.BlockSpec(memory_space=pl.ANY)]
    z, tails = pl.pallas_call(
        functools.partial(_ffn_up_state_kernel, tm=tm, seq_len=seq_len),
        out_shape=(jax.ShapeDtypeStruct((m_all, dff), _BF16),
                   jax.ShapeDtypeStruct((nrows // seq_len * SUBLANES, dff), _F32)),
        grid=(nj, nrows // tm),
        in_specs=specs,
        out_specs=(pl.BlockSpec((tm, tn), lambda j, i: (rb0 + i, j)),
                   pl.BlockSpec((nseg * SUBLANES, tn), lambda j, i: (i, j))),
        scratch_shapes=[pltpu.VMEM((HALO + tm, tn), _F32)],
        input_output_aliases={6: 0},
        compiler_params=_params(("parallel", "arbitrary")),
        name="ffn_up_conv_state",
    )(hf, w_up, w_up, conv_w, conv_b.reshape(1, dff), init, z_prev)
    return z, _conv_state(tails, nrows // seq_len, 1, dff)


def _layer(xa, xb, geom, cache, state, w):
    (B, S, Bd, Td, P) = geom
    (cache_k, cache_v, cache_ik) = cache
    (state_C, state_n, state_m, state_conv) = state
    (w_in, b_igate, b_fgate, g_mhnorm, w_proj_m, w_proj_a, w_out,
     g_pre_mix, g_post_mix, g_pre_ffn, g_post_ffn, w_up, conv_w, conv_b, w_down) = w
    mp, d = xa.shape
    dff = w_down.shape[0]
    wk = A_KV_HEADS * A_HD

    c_mi = A_END
    c_aq = c_mi + 2 * M_HEADS
    c_ik = c_aq + B_END
    c_iw = c_ik + IDX_DIM
    c_gm = c_iw + IDX_HEADS
    w_a, w_b, w_c = _w_in_prep(w_in, c_aq, c_gm)
    w_small = jnp.concatenate([w_in[:, c_ik:c_iw], w_in[:, c_mi:c_aq], w_in[:, c_iw:c_gm],
                               jnp.zeros((d, LANES - SM_END), w_in.dtype)], axis=1).astype(_BF16)
    gate_bias = jnp.concatenate([jnp.zeros((SM_MI,), _F32), b_igate.astype(_F32), b_fgate.astype(_F32),
                                 jnp.zeros((LANES - SM_IW,), _F32)]).reshape(1, LANES)

    hn = _rmsnorm_cast(xa, xb, g_pre_mix)
    proj_a = _matmul(hn, w_a, _F32, tm=512, tn=1024, name="in_proj_mlstm")
    proj_b = _matmul(hn, w_b, _F32, tm=512, tn=1024, name="in_proj_attn")
    proj_c = _matmul(hn, w_c, _F32, tm=512, tn=1024, name="in_proj_gates")
    small = _matmul(hn, w_small, _F32, tm=512, tn=LANES, name="in_proj_small")

    lp_chunk = _tile(S, 256, CHUNK)
    hm_p, c_p, n_p, m_p = _mlstm(proj_a, small, gate_bias, g_mhnorm, 0, B, S, lp_chunk, None)
    s0 = jnp.zeros((Bd * M_HEADS, SUBLANES, M_QK), _F32)
    s0 = s0.at[:, 0, :].set(state_n.reshape(Bd * M_HEADS, M_QK).astype(_F32))
    s0 = s0.at[:, 1, :].set(jnp.broadcast_to(state_m.reshape(Bd * M_HEADS, 1).astype(_F32), (Bd * M_HEADS, M_QK)))
    hm, c_s, n_s, m_s = _mlstm(proj_a, small, gate_bias, g_mhnorm, mp, Bd, Td, min(CHUNK, Td),
                               (state_C.astype(_F32), s0), hm_p)

    pos = jnp.concatenate([jnp.tile(jnp.arange(S, dtype=jnp.int32), B),
                           jnp.tile(P + jnp.arange(Td, dtype=jnp.int32), Bd)])
    q_rot, k_rot, k_bf, v_bf, iqx, ik_rot, ikx = _rope_all(proj_b, small, pos)
    tq_p = _tile(S, 256, CHUNK)
    tk_p = _tile(S, 512)
    ha_p = _attention(q_rot, iqx, small, k_bf, v_bf, ikx, 0, B, S, tq_p, tk_p, S, 0, S)
    l_s = P + Td
    tk_s = 256
    lp_s = -(-l_s // tk_s) * tk_s
    pad = lp_s - l_s
    k_s = jnp.concatenate([cache_k.reshape(Bd, P, wk).astype(_BF16), k_bf[mp:].reshape(Bd, Td, wk),
                           jnp.zeros((Bd, pad, wk), _BF16)], axis=1).reshape(Bd * lp_s, wk)
    v_s = jnp.concatenate([cache_v.reshape(Bd, P, wk).astype(_BF16), v_bf[mp:].reshape(Bd, Td, wk),
                           jnp.zeros((Bd, pad, wk), _BF16)], axis=1).reshape(Bd * lp_s, wk)
    cik = cache_ik.astype(_F32)
    cik_hi = cik.astype(_BF16)
    cik_lo = (cik - cik_hi.astype(_F32)).astype(_BF16)
    cikx = jnp.concatenate([cik_hi, cik_lo, cik_hi, jnp.zeros_like(cik_hi)], axis=-1)
    ikx_s = jnp.concatenate([cikx, ikx[mp:].reshape(Bd, Td, IDX_XW),
                             jnp.zeros((Bd, pad, IDX_XW), _BF16)], axis=1).reshape(Bd * lp_s, IDX_XW)
    ha = _attention(q_rot, iqx, small, k_s, v_s, ikx_s, mp, Bd, Td, Td, tk_s, lp_s, P, l_s, ha_p)

    mix = _merge(hm, ha, w_proj_m.astype(_BF16), w_proj_a.astype(_BF16), proj_c, d)
    y1 = _matmul(mix, w_out.astype(_BF16), _F32, tm=512, tn=1024, name="out_proj")
    x1, hf = _resnorm_next(xa, xb, y1, g_post_mix, g_pre_ffn)

    w_up = w_up.astype(_F32)
    z_p, conv_p = _ffn_up_first(hf, w_up, conv_w, conv_b, mp, S)
    z, conv_s = _ffn_up_state(hf, w_up, conv_w, conv_b, mp, Bd * Td, Td, state_conv.astype(_F32), z_p)
    y2 = _matmul(z, w_down.astype(_BF16), _F32, tm=512, tn=1024, tk=dff // 2, name="down_proj")
    x2 = _resnorm(x1, y2, g_post_ffn)

    av = proj_b[:, B_AV:B_AV + wk]
    outs_p = (k_rot[:mp].reshape(B, S, A_KV_HEADS, A_HD), av[:mp].reshape(B, S, A_KV_HEADS, A_HD),
              ik_rot[:mp].reshape(B, S, IDX_DIM), c_p, n_p, m_p, conv_p)
    outs_s = (k_rot[mp:].reshape(Bd, Td, A_KV_HEADS, A_HD), av[mp:].reshape(Bd, Td, A_KV_HEADS, A_HD),
              ik_rot[mp:].reshape(Bd, Td, IDX_DIM), c_s, n_s, m_s, conv_s)
    return x2, outs_p, outs_s


def kernel(x_prompt, x_sample, cache_k, cache_v, cache_idx_k, state_C, state_n, state_m, state_conv,
           w_in, b_igate, b_fgate, g_mhnorm, w_proj_m, w_proj_a, w_out,
           g_pre_mix, g_post_mix, g_pre_ffn, g_post_ffn, w_up, conv_w, conv_b, w_down):
    B, S, d = x_prompt.shape
    Bd, Td, _ = x_sample.shape
    P = cache_k.shape[2]
    depth = w_in.shape[0]
    mp = B * S
    xa, xb = x_prompt.reshape(mp, d), x_sample.reshape(Bd * Td, d)
    all_p, all_s = [], []
    for l in range(depth):
        w = (w_in[l], b_igate[l], b_fgate[l], g_mhnorm[l], w_proj_m[l], w_proj_a[l], w_out[l],
             g_pre_mix[l], g_post_mix[l], g_pre_ffn[l], g_post_ffn[l], w_up[l], conv_w[l], conv_b[l], w_down[l])
        x_all, outs_p, outs_s = _layer(xa, xb, (B, S, Bd, Td, P), (cache_k[l], cache_v[l], cache_idx_k[l]),
                                       (state_C[l], state_n[l], state_m[l], state_conv[l]), w)
        xa, xb = x_all[:mp], x_all[mp:]
        all_p.append(outs_p)
        all_s.append(outs_s)

    def stk(outs, i):
        return jnp.stack([o[i] for o in outs])

    yp = x_all[:mp].reshape(B, S, d)
    ys = x_all[mp:].reshape(Bd, Td, d)
    return (yp, ys) + tuple(stk(all_p, i) for i in range(7)) + tuple(stk(all_s, i) for i in range(7))
```

```python
import functools
import math

import jax
import jax.numpy as jnp
from jax import lax
from jax.experimental import pallas as pl
from jax.experimental.pallas import tpu as pltpu

CHUNK = 64
NORM_EPS = 1e-6
ROPE_THETA = 500000.0
M_HEADS = 8
M_QK = 128
M_V = 256
A_HEADS = 16
A_KV_HEADS = 4
A_GROUP = A_HEADS // A_KV_HEADS
A_HD = 128
A_ROT = A_HD // 4
IDX_HEADS = 16
IDX_DIM = 64
IDX_ROT = IDX_DIM // 4
TOPK_MAX = 256
CONV_W = 3

LANES = 128
SUBLANES = 8
VMEM_LIMIT_BYTES = 52 * 1024 * 1024

A_MQ = 0
A_MK = A_MQ + M_HEADS * M_QK
A_MV = A_MK + M_HEADS * M_QK
A_MO = A_MV + M_HEADS * M_V
A_END = A_MO + M_HEADS * M_V
B_AQ = 0
B_AK = B_AQ + A_HEADS * A_HD
B_AV = B_AK + A_KV_HEADS * A_HD
B_IQ = B_AV + A_KV_HEADS * A_HD
B_END = B_IQ + IDX_HEADS * IDX_DIM
SM_IK = 0
SM_MI = SM_IK + IDX_DIM
SM_MF = SM_MI + M_HEADS
SM_IW = SM_MF + M_HEADS
SM_END = SM_IW + IDX_HEADS
IDX_XW = 4 * IDX_DIM

INT_MIN = -2 ** 31
KEY_NEG_INF = INT_MIN + 0x7FFFFF
NEG_BIG = -1e30

_BF16 = jnp.bfloat16
_F32 = jnp.float32


def _tile(dim, target, quantum=LANES):
    if dim <= target:
        return dim
    t = (target // quantum) * quantum
    while t >= quantum:
        if dim % t == 0:
            return t
        t -= quantum
    return dim


def _params(sem):
    return pltpu.CompilerParams(dimension_semantics=sem, vmem_limit_bytes=VMEM_LIMIT_BYTES)


def _dot(a, b):
    return jnp.dot(a, b, preferred_element_type=_F32)


def _dot_nt(a, b):
    return lax.dot_general(a, b, (((1,), (1,)), ((), ())), preferred_element_type=_F32)


def _dot_tn(a, b):
    return lax.dot_general(a, b, (((0,), (0,)), ((), ())), preferred_element_type=_F32)


def _mm_wt_kernel(a_ref, w_ref, o_ref, w_bf):
    @pl.when(pl.program_id(1) == 0)
    def _():
        w_bf[...] = w_ref[...].astype(w_bf.dtype)

    o_ref[...] = _dot_nt(a_ref[...], w_bf[...]).astype(o_ref.dtype)


def _matmul_wt(a, w_t, row0, n, out_dtype, *, tm=512, tn=512, name="matmul_wt"):
    m, kd = a.shape
    tm = _tile(m, tm, SUBLANES)
    tn = _tile(n, tn)
    return pl.pallas_call(
        _mm_wt_kernel,
        out_shape=jax.ShapeDtypeStruct((m, n), out_dtype),
        grid=(n // tn, m // tm),
        in_specs=[pl.BlockSpec((tm, kd), lambda j, i: (i, 0)),
                  pl.BlockSpec((pl.Element(tn), pl.Element(kd)),
                               lambda j, i: (pl.multiple_of(row0 + j * tn, SUBLANES), 0))],
        out_specs=pl.BlockSpec((tm, tn), lambda j, i: (i, j)),
        scratch_shapes=[pltpu.VMEM((tn, kd), _BF16)],
        compiler_params=_params(("parallel", "arbitrary")),
        name=name,
    )(a, w_t)


def _two_source_specs(tm, d, n_first):
    return (pl.BlockSpec((tm, d), lambda i: (jnp.minimum(i, n_first - 1), 0)),
            pl.BlockSpec((tm, d), lambda i: (jnp.maximum(i - n_first, 0), 0)))


def _rmsnorm_kernel(xa_ref, xb_ref, g_ref, o_ref, *, n_first):
    def body(x_ref):
        x = x_ref[...]
        ms = jnp.mean(x * x, axis=-1, keepdims=True)
        o_ref[...] = (x * lax.rsqrt(ms + NORM_EPS) * g_ref[...]).astype(o_ref.dtype)

    i = pl.program_id(0)
    pl.when(i < n_first)(lambda: body(xa_ref))
    pl.when(i >= n_first)(lambda: body(xb_ref))


def _row_tile(m_a, m_b):
    return _tile(math.gcd(m_a, m_b), 256, SUBLANES)


def _rmsnorm_cast(xa, xb, g):
    (m_a, d), m_b = xa.shape, xb.shape[0]
    tm = _row_tile(m_a, m_b)
    n_first = m_a // tm
    return pl.pallas_call(
        functools.partial(_rmsnorm_kernel, n_first=n_first),
        out_shape=jax.ShapeDtypeStruct((m_a + m_b, d), _BF16),
        grid=((m_a + m_b) // tm,),
        in_specs=[*_two_source_specs(tm, d, n_first), pl.BlockSpec((1, d), lambda i: (0, 0))],
        out_specs=pl.BlockSpec((tm, d), lambda i: (i, 0)),
        compiler_params=_params(("parallel",)),
        name="rmsnorm_cast",
    )(xa, xb, g.reshape(1, d))


def _mm_kernel(a_ref, b_ref, o_ref, acc_ref, *, nk):
    k = pl.program_id(2)

    @pl.when(k == 0)
    def _():
        acc_ref[...] = jnp.zeros_like(acc_ref)

    acc_ref[...] += _dot(a_ref[...], b_ref[...])

    @pl.when(k == nk - 1)
    def _():
        o_ref[...] = acc_ref[...].astype(o_ref.dtype)


def _mm1_kernel(a_ref, b_ref, o_ref):
    o_ref[...] = _dot(a_ref[...], b_ref[...]).astype(o_ref.dtype)


def _matmul(a, b, out_dtype, *, tm=512, tn=1024, tk=None, name="matmul"):
    m, kd = a.shape
    _, n = b.shape
    tm = _tile(m, tm, SUBLANES)
    tn = _tile(n, tn)
    tk = kd if tk is None else _tile(kd, tk)
    nk = kd // tk
    if nk == 1:
        return pl.pallas_call(
            _mm1_kernel,
            out_shape=jax.ShapeDtypeStruct((m, n), out_dtype),
            grid=(n // tn, m // tm),
            in_specs=[pl.BlockSpec((tm, kd), lambda j, i: (i, 0)),
                      pl.BlockSpec((kd, tn), lambda j, i: (0, j))],
            out_specs=pl.BlockSpec((tm, tn), lambda j, i: (i, j)),
            compiler_params=_params(("parallel", "parallel")),
            name=name,
        )(a, b)
    return pl.pallas_call(
        functools.partial(_mm_kernel, nk=nk),
        out_shape=jax.ShapeDtypeStruct((m, n), out_dtype),
        grid=(n // tn, m // tm, nk),
        in_specs=[pl.BlockSpec((tm, tk), lambda j, i, k: (i, k)),
                  pl.BlockSpec((tk, tn), lambda j, i, k: (k, j))],
        out_specs=pl.BlockSpec((tm, tn), lambda j, i, k: (i, j)),
        scratch_shapes=[pltpu.VMEM((tm, tn), _F32)],
        compiler_params=_params(("parallel", "parallel", "arbitrary")),
        name=name,
    )(a, b)


def _mlstm_kernel(*refs, L, has_state, has_alias):
    q_ref, k_ref, v_ref, o_ref, g_ref, gb_ref, gn_ref = refs[:7]
    if has_state:
        c0_ref, s0_ref = refs[7:9]
    h_ref, c_out_ref, s_out_ref, c_scr, n_scr, m_scr = refs[7 + 2 * has_state + has_alias:]
    head = pl.program_id(1)
    c = pl.program_id(2)
    nc = pl.num_programs(2)

    @pl.when(c == 0)
    def _():
        if has_state:
            c_scr[...] = c0_ref[0, 0]
            n_scr[...] = s0_ref[0, 0:1, :]
            m_scr[...] = s0_ref[0, 1:2, :]
        else:
            c_scr[...] = jnp.zeros_like(c_scr)
            n_scr[...] = jnp.zeros_like(n_scr)
            m_scr[...] = jnp.zeros_like(m_scr)

    gates = g_ref[...] + gb_ref[...]
    lane = lax.broadcasted_iota(jnp.int32, gates.shape, 1)
    ig_col = jnp.sum(jnp.where(lane == SM_MI + head, gates, 0.0), axis=1, keepdims=True)
    mf_col = jnp.sum(jnp.where(lane == SM_MF + head, gates, 0.0), axis=1, keepdims=True)
    lf_col = jnp.minimum(mf_col, 0.0) - jnp.log1p(jnp.exp(-jnp.abs(mf_col)))

    ri = lax.broadcasted_iota(jnp.int32, (L, L), 0)
    ci = lax.broadcasted_iota(jnp.int32, (L, L), 1)
    eye = ri == ci
    tril = ci <= ri
    lf_row = jnp.sum(jnp.where(eye, lf_col, 0.0), axis=0, keepdims=True)
    ig_row = jnp.sum(jnp.where(eye, ig_col, 0.0), axis=0, keepdims=True)
    b_col = jnp.sum(jnp.where(tril, lf_row, 0.0), axis=1, keepdims=True)
    b_row = jnp.sum(jnp.where(ri <= ci, lf_col, 0.0), axis=0, keepdims=True)
    logw = jnp.where(tril, b_col - b_row + ig_row, -jnp.inf)

    m_prev = m_scr[:, 0:1]
    inter = b_col + m_prev
    m_t = jnp.maximum(inter, jnp.max(logw, axis=1, keepdims=True))
    a = jnp.exp(inter - m_t)
    sw = jnp.exp(logw - m_t)

    q = q_ref[...]
    k = k_ref[...] * (M_QK ** -0.5)
    v = v_ref[...]
    qb = q.astype(_BF16)
    kb = k.astype(_BF16)
    s = _dot_nt(qb, kb) * sw
    c_old = c_scr[...]
    n_old = n_scr[...]
    num = a * _dot_nt(qb, c_old.astype(_BF16)) + _dot(s.astype(_BF16), v.astype(_BF16))
    den = a * jnp.sum(q * n_old, axis=1, keepdims=True) + jnp.sum(s, axis=1, keepdims=True)
    h = num / jnp.maximum(jnp.abs(den), jnp.exp(-m_t))

    m_new = m_t[L - 1:L, :]
    b_last = b_col[L - 1:L, :]
    g_col = jnp.exp(b_last - b_col + ig_col - m_new)
    decay = jnp.exp(b_last + m_prev - m_new)
    c_new = decay * c_old + _dot_tn((v * g_col).astype(_BF16), kb)
    n_new = decay * n_old + jnp.sum(g_col * k, axis=0, keepdims=True)
    c_scr[...] = c_new
    n_scr[...] = n_new
    m_scr[...] = jnp.broadcast_to(m_new, m_scr.shape)

    ms = jnp.mean(h * h, axis=1, keepdims=True)
    y = h * lax.rsqrt(ms + NORM_EPS) * gn_ref[0]
    h_ref[...] = (y * jax.nn.sigmoid(o_ref[...])).astype(h_ref.dtype)

    @pl.when(c == nc - 1)
    def _():
        c_out_ref[0, 0] = c_new
        s_out_ref[0] = jnp.zeros(s_out_ref.shape[1:], _F32)
        s_out_ref[0, 0:1, :] = n_new
        s_out_ref[0, 1:2, :] = jnp.broadcast_to(m_new, (1, M_QK))


def _mlstm(proj, small, gate_bias, g_mhnorm, row0, nseq, T, L, state, h_prev=None):
    nc = T // L
    rb0 = row0 // L
    has_state = state is not None
    has_alias = h_prev is not None

    def rows(b, h, c):
        return rb0 + b * nc + c

    in_specs = [
        pl.BlockSpec((L, M_QK), lambda b, h, c: (rows(b, h, c), A_MQ // M_QK + h)),
        pl.BlockSpec((L, M_QK), lambda b, h, c: (rows(b, h, c), A_MK // M_QK + h)),
        pl.BlockSpec((L, M_V), lambda b, h, c: (rows(b, h, c), A_MV // M_V + h)),
        pl.BlockSpec((L, M_V), lambda b, h, c: (rows(b, h, c), A_MO // M_V + h)),
        pl.BlockSpec((L, LANES), lambda b, h, c: (rows(b, h, c), 0)),
        pl.BlockSpec((1, LANES), lambda b, h, c: (0, 0)),
        pl.BlockSpec((1, 1, M_V), lambda b, h, c: (h, 0, 0)),
    ]
    args = [proj, proj, proj, proj, small, gate_bias, g_mhnorm.reshape(M_HEADS, 1, M_V)]
    if has_state:
        c0, s0 = state
        in_specs += [pl.BlockSpec((1, 1, M_V, M_QK), lambda b, h, c: (b, h, 0, 0)),
                     pl.BlockSpec((1, SUBLANES, M_QK), lambda b, h, c: (b * M_HEADS + h, 0, 0))]
        args += [c0, s0]
    aliases = {}
    if has_alias:
        aliases = {len(args): 0}
        in_specs.append(pl.BlockSpec(memory_space=pl.ANY))
        args.append(h_prev)
    out_shape = (jax.ShapeDtypeStruct((proj.shape[0], M_HEADS * M_V), _BF16),
                 jax.ShapeDtypeStruct((nseq, M_HEADS, M_V, M_QK), _F32),
                 jax.ShapeDtypeStruct((nseq * M_HEADS, SUBLANES, M_QK), _F32))
    out_specs = (pl.BlockSpec((L, M_V), lambda b, h, c: (rows(b, h, c), h)),
                 pl.BlockSpec((1, 1, M_V, M_QK), lambda b, h, c: (b, h, 0, 0)),
                 pl.BlockSpec((1, SUBLANES, M_QK), lambda b, h, c: (b * M_HEADS + h, 0, 0)))
    hm, c_new, stats = pl.pallas_call(
        functools.partial(_mlstm_kernel, L=L, has_state=has_state, has_alias=has_alias),
        out_shape=out_shape,
        grid=(nseq, M_HEADS, nc),
        in_specs=in_specs,
        out_specs=out_specs,
        scratch_shapes=[pltpu.VMEM((M_V, M_QK), _F32), pltpu.VMEM((1, M_QK), _F32),
                        pltpu.VMEM((1, M_QK), _F32)],
        input_output_aliases=aliases,
        compiler_params=_params(("parallel", "parallel", "arbitrary")),
        name="mlstm",
    )(*args)
    n_new = stats[:, 0, :].reshape(nseq, M_HEADS, M_QK)
    m_new = stats[:, 1, 0].reshape(nseq, M_HEADS)
    return hm, c_new, n_new, m_new


def _rope(x, cos, sin_lo, sin_hi, half):
    n = x.shape[-1]
    return (x * cos + pltpu.roll(x, n - half, 1) * sin_lo + pltpu.roll(x, half, 1) * sin_hi)


def _hi_lo(x):
    hi = x.astype(_BF16).astype(_F32)
    return hi, x - hi


def _rope_kernel(aq_ref, ak_ref, av_ref, iq_ref, sm_ref, ta_ref, ti_ref,
                 q_out, k_out, kb_out, vb_out, iqx_out, ik_out, ikx_out):
    ca, sa_lo, sa_hi = ta_ref[0], ta_ref[1], ta_ref[2]
    ci, si_lo, si_hi = ti_ref[0], ti_ref[1], ti_ref[2]
    for h in range(A_HEADS):
        sl = slice(h * A_HD, (h + 1) * A_HD)
        q_out[:, sl] = _rope(aq_ref[:, sl], ca, sa_lo, sa_hi, A_ROT // 2).astype(q_out.dtype)
    for h in range(A_KV_HEADS):
        sl = slice(h * A_HD, (h + 1) * A_HD)
        kr = _rope(ak_ref[:, sl], ca, sa_lo, sa_hi, A_ROT // 2)
        k_out[:, sl] = kr
        kb_out[:, sl] = kr.astype(kb_out.dtype)
    vb_out[...] = av_ref[...].astype(vb_out.dtype)
    low = lax.broadcasted_iota(jnp.int32, (1, LANES), 1) < IDX_DIM
    for p in range(IDX_HEADS * IDX_DIM // LANES):
        x = _rope(iq_ref[:, p * LANES:(p + 1) * LANES], ci, si_lo, si_hi, IDX_ROT // 2)
        hi, lo = _hi_lo(x)
        hi_sw = pltpu.roll(hi, IDX_DIM, 1)
        lo_sw = pltpu.roll(lo, IDX_DIM, 1)
        c0 = 2 * p * IDX_XW
        iqx_out[:, c0:c0 + LANES] = jnp.where(low, hi, hi_sw).astype(iqx_out.dtype)
        iqx_out[:, c0 + LANES:c0 + 2 * LANES] = jnp.where(low, lo, 0.0).astype(iqx_out.dtype)
        iqx_out[:, c0 + 2 * LANES:c0 + 3 * LANES] = jnp.where(low, hi_sw, hi).astype(iqx_out.dtype)
        iqx_out[:, c0 + 3 * LANES:c0 + 4 * LANES] = jnp.where(low, lo_sw, 0.0).astype(iqx_out.dtype)
    ik = _rope(sm_ref[...], ci, si_lo, si_hi, IDX_ROT // 2)
    ik_out[...] = ik[:, SM_IK:SM_IK + IDX_DIM]
    hi, lo = _hi_lo(ik)
    ikx_out[:, 0:LANES] = jnp.where(low, hi, pltpu.roll(lo, IDX_DIM, 1)).astype(ikx_out.dtype)
    ikx_out[:, LANES:2 * LANES] = jnp.where(low, hi, 0.0).astype(ikx_out.dtype)


def _rope_tables(pos, rot, width, reps_valid):
    half = rot // 2
    inv_freq = jnp.exp(jnp.arange(half, dtype=_F32) * (-2.0 * math.log(ROPE_THETA) / rot))
    ang = pos.astype(_F32)[:, None] * inv_freq[None, :]
    cos, sin = jnp.cos(ang), jnp.sin(ang)
    m = pos.shape[0]
    one = jnp.ones((m, width - rot), _F32)
    zero = jnp.zeros((m, width - rot), _F32)
    zh = jnp.zeros((m, half), _F32)
    c_head = jnp.concatenate([cos, cos, one], axis=1)
    lo_head = jnp.concatenate([-sin, zh, zero], axis=1)
    hi_head = jnp.concatenate([zh, sin, zero], axis=1)
    reps = LANES // width
    ident = (jnp.ones((m, width), _F32), jnp.zeros((m, width), _F32), jnp.zeros((m, width), _F32))
    out = []
    for t, idt in zip((c_head, lo_head, hi_head), ident):
        out.append(jnp.concatenate([t if r < reps_valid else idt for r in range(reps)], axis=1))
    return jnp.stack(out)


def _rope_all(proj, small, pos):
    m = proj.shape[0]
    tm = _tile(m, 256, SUBLANES)
    ta = _rope_tables(pos, A_ROT, A_HD, 1)
    ti = _rope_tables(pos, IDX_ROT, IDX_DIM, LANES // IDX_DIM)
    wq, wk, wi = A_HEADS * A_HD, A_KV_HEADS * A_HD, IDX_HEADS * IDX_DIM
    wix = IDX_HEADS * IDX_XW
    out_shape = (jax.ShapeDtypeStruct((m, wq), _BF16),
                 jax.ShapeDtypeStruct((m, wk), _F32),
                 jax.ShapeDtypeStruct((m, wk), _BF16),
                 jax.ShapeDtypeStruct((m, wk), _BF16),
                 jax.ShapeDtypeStruct((m, wix), _BF16),
                 jax.ShapeDtypeStruct((m, IDX_DIM), _F32),
                 jax.ShapeDtypeStruct((m, IDX_XW), _BF16))
    return pl.pallas_call(
        _rope_kernel,
        out_shape=out_shape,
        grid=(m // tm,),
        in_specs=[pl.BlockSpec((tm, wq), lambda i: (i, B_AQ // wq)),
                  pl.BlockSpec((tm, wk), lambda i: (i, B_AK // wk)),
                  pl.BlockSpec((tm, wk), lambda i: (i, B_AV // wk)),
                  pl.BlockSpec((tm, wi), lambda i: (i, B_IQ // wi)),
                  pl.BlockSpec((tm, LANES), lambda i: (i, 0)),
                  pl.BlockSpec((3, tm, LANES), lambda i: (0, i, 0)),
                  pl.BlockSpec((3, tm, LANES), lambda i: (0, i, 0))],
        out_specs=(pl.BlockSpec((tm, wq), lambda i: (i, 0)),
                   pl.BlockSpec((tm, wk), lambda i: (i, 0)),
                   pl.BlockSpec((tm, wk), lambda i: (i, 0)),
                   pl.BlockSpec((tm, wk), lambda i: (i, 0)),
                   pl.BlockSpec((tm, wix), lambda i: (i, 0)),
                   pl.BlockSpec((tm, IDX_DIM), lambda i: (i, 0)),
                   pl.BlockSpec((tm, IDX_XW), lambda i: (i, 0))),
        compiler_params=_params(("parallel",)),
        name="rope",
    )(proj, proj, proj, proj, small, ta, ti)


def _attn_kernel(*refs, TQ, TK, past_len, l_valid, topk):
    q_ref, iqx_ref, sm_ref, k_ref, v_ref, ikx_ref = refs[:6]
    o_ref, key_scr, bias_scr, w_scr = refs[-4:]
    i = pl.program_id(1)
    q_start = past_len + i * TQ
    qpos = q_start + lax.broadcasted_iota(jnp.int32, (TQ, 1), 0)
    qchunk = qpos // CHUNK
    last_vis = jnp.minimum(((q_start + TQ - 1) // CHUNK + 1) * CHUNK, l_valid)
    nkt = (last_vis + TK - 1) // TK
    lane_pos = lax.broadcasted_iota(jnp.int32, (1, TK), 1)

    w_scale = (IDX_DIM ** -0.5) * (IDX_HEADS ** -0.5)
    for h in range(IDX_HEADS):
        w_scr[h] = jnp.broadcast_to(sm_ref[:, SM_IW + h:SM_IW + h + 1] * w_scale, (TQ, LANES))
    lane128 = lax.broadcasted_iota(jnp.int32, (1, LANES), 1)
    sc = min(TK, 2 * LANES)

    def score_tile(kt, carry):
        for cc in range(TK // sc):
            k0 = pl.multiple_of(kt * TK + cc * sc, sc)
            ik_c = ikx_ref[pl.ds(k0, sc), :]
            accs = [jnp.zeros((TQ, LANES), _F32) for _ in range(sc // LANES)]
            for h in range(IDX_HEADS):
                isc = _dot_nt(iqx_ref[:, h * IDX_XW:(h + 1) * IDX_XW], ik_c)
                w_h = w_scr[h]
                for c in range(sc // LANES):
                    accs[c] = accs[c] + jnp.maximum(isc[:, c * LANES:(c + 1) * LANES], 0.0) * w_h
            for c in range(sc // LANES):
                bits = pltpu.bitcast(accs[c] + 0.0, jnp.int32)
                key = jnp.where(bits < 0, bits ^ 0x7FFFFFFF, bits)
                kpos = k0 + c * LANES + lane128
                vis = ((kpos // CHUNK) <= qchunk) & (kpos < l_valid)
                col = cc * sc + c * LANES
                key_scr[kt, :, col:col + LANES] = jnp.where(vis, key, INT_MIN)
        return carry

    lax.fori_loop(0, nkt, score_tile, 0)

    def lane_fold(x):
        out = x[:, 0:LANES]
        for c in range(1, TK // LANES):
            out = out + x[:, c * LANES:(c + 1) * LANES]
        return out

    def count(pred_fn):
        def body(kt, part):
            return part + lane_fold(pred_fn(key_scr[kt], kt).astype(jnp.int32))
        part = lax.fori_loop(0, nkt, body, jnp.zeros((TQ, LANES), jnp.int32))
        return jnp.sum(part, axis=1, keepdims=True)

    def bit_step(it, thr_u):
        cand_u = thr_u | lax.shift_left(jnp.int32(1), 31 - it)
        cand_s = cand_u ^ INT_MIN
        cnt = count(lambda key, kt: key >= cand_s)
        return jnp.where(cnt >= topk, cand_u, thr_u)

    thr = lax.fori_loop(0, 32, bit_step, jnp.zeros((TQ, 1), jnp.int32)) ^ INT_MIN
    n_ge = count(lambda key, kt: key >= thr)

    def tie_search(_):
        need = topk - count(lambda key, kt: key > thr)
        nbits = (key_scr.shape[0] * TK).bit_length()

        def pos_step(it, p):
            cand = p | lax.shift_left(jnp.int32(1), nbits - 1 - it)
            cnt = count(lambda key, kt: (key == thr) & ((kt * TK + lane_pos) < cand))
            return jnp.where(cnt < need, cand, p)

        return lax.fori_loop(0, nbits, pos_step, jnp.zeros((TQ, 1), jnp.int32))

    has_tie = jnp.max(n_ge.astype(_F32)) > topk
    p_last = lax.cond(has_tie, tie_search, lambda _: jnp.full((TQ, 1), 2 ** 31 - 1, jnp.int32), 0)

    def bias_tile(kt, carry):
        key = key_scr[kt]
        sel = ((key > thr) | ((key == thr) & ((kt * TK + lane_pos) <= p_last))) & (key > KEY_NEG_INF)
        bias_scr[kt] = jnp.where(sel, 0.0, NEG_BIG)
        return carry

    lax.fori_loop(0, nkt, bias_tile, 0)

    scale = (A_HD ** -0.5) * math.log2(math.e)
    rows = A_GROUP * TQ
    n_par = 1

    def q_group(g):
        return jnp.concatenate(
            [q_ref[:, (g * A_GROUP + j) * A_HD:(g * A_GROUP + j + 1) * A_HD] for j in range(A_GROUP)], axis=0)

    for g0 in range(0, A_KV_HEADS, n_par):
        qgs = [q_group(g0 + u) for u in range(n_par)]

        def attn_tile(kt, carry, g0=g0, qgs=qgs):
            k0 = pl.multiple_of(kt * TK, TK)
            bias = bias_scr[kt][None]
            s_u = [_dot_nt(qgs[u], k_ref[pl.ds(k0, TK), (g0 + u) * A_HD:(g0 + u + 1) * A_HD])
                   .reshape(A_GROUP, TQ, TK) * scale + bias for u in range(n_par)]
            new = []
            p_u = []
            for u in range(n_par):
                m_i, l_i, acc = carry[u]
                m_new = jnp.maximum(m_i, jnp.max(s_u[u], axis=-1, keepdims=True))
                alpha = jnp.exp2(m_i - m_new)
                p = jnp.exp2(s_u[u] - m_new)
                p_u.append(p.reshape(rows, TK).astype(_BF16))
                new.append((m_new, alpha * l_i + jnp.sum(p, axis=-1, keepdims=True), alpha * acc))
            out = []
            for u in range(n_par):
                v_t = v_ref[pl.ds(k0, TK), (g0 + u) * A_HD:(g0 + u + 1) * A_HD]
                m_new, l_new, acc = new[u]
                out.append((m_new, l_new, acc + _dot(p_u[u], v_t).reshape(A_GROUP, TQ, A_HD)))
            return tuple(out)

        init = tuple((jnp.full((A_GROUP, TQ, 1), NEG_BIG, _F32), jnp.zeros((A_GROUP, TQ, 1), _F32),
                      jnp.zeros((A_GROUP, TQ, A_HD), _F32)) for _ in range(n_par))
        fin = lax.fori_loop(0, nkt, attn_tile, init)
        for u in range(n_par):
            _, l_f, acc_f = fin[u]
            out = acc_f / l_f
            for j in range(A_GROUP):
                hh = (g0 + u) * A_GROUP + j
                o_ref[:, hh * A_HD:(hh + 1) * A_HD] = out[j].astype(o_ref.dtype)


def _attention(q_rot, iq_rot, small, k_all, v_all, ik_all, row0, nseq, T, TQ, TK, lp, past_len, l_valid, o_prev=None):
    nq = T // TQ
    qb0 = row0 // TQ
    topk = min(TOPK_MAX, l_valid // 4)
    wq, wk, wix = A_HEADS * A_HD, A_KV_HEADS * A_HD, IDX_HEADS * IDX_XW
    in_specs = [pl.BlockSpec((TQ, wq), lambda b, i: (qb0 + b * nq + i, 0)),
                pl.BlockSpec((TQ, wix), lambda b, i: (qb0 + b * nq + i, 0)),
                pl.BlockSpec((TQ, LANES), lambda b, i: (qb0 + b * nq + i, 0)),
                pl.BlockSpec((lp, wk), lambda b, i: (b, 0)),
                pl.BlockSpec((lp, wk), lambda b, i: (b, 0)),
                pl.BlockSpec((lp, IDX_XW), lambda b, i: (b, 0))]
    args = [q_rot, iq_rot, small, k_all, v_all, ik_all]
    aliases = {}
    if o_prev is not None:
        aliases = {len(args): 0}
        in_specs.append(pl.BlockSpec(memory_space=pl.ANY))
        args.append(o_prev)
    return pl.pallas_call(
        functools.partial(_attn_kernel, TQ=TQ, TK=TK, past_len=past_len, l_valid=l_valid, topk=topk),
        out_shape=jax.ShapeDtypeStruct((q_rot.shape[0], wq), _BF16),
        grid=(nseq, nq),
        in_specs=in_specs,
        out_specs=pl.BlockSpec((TQ, wq), lambda b, i: (qb0 + b * nq + i, 0)),
        scratch_shapes=[pltpu.VMEM((lp // TK, TQ, TK), jnp.int32),
                        pltpu.VMEM((lp // TK, TQ, TK), _F32),
                        pltpu.VMEM((IDX_HEADS, TQ, LANES), _F32)],
        input_output_aliases=aliases,
        compiler_params=_params(("parallel", "arbitrary")),
        name="sparse_attention",
    )(*args)


def _merge_kernel(hm_ref, ha_ref, wm_ref, wa_ref, gm_ref, ga_ref, o_ref):
    pm = _dot(hm_ref[...], wm_ref[...])
    pa = _dot(ha_ref[...], wa_ref[...])
    o_ref[...] = (jax.nn.sigmoid(gm_ref[...]) * pm + jax.nn.sigmoid(ga_ref[...]) * pa).astype(o_ref.dtype)


def _merge(hm, ha, wm, wa, proj, d):
    m = hm.shape[0]
    tm = _tile(m, 512, SUBLANES)
    tn = _tile(d, 1024)
    gm0 = 0
    ga0 = d // tn
    km, ka = hm.shape[1], ha.shape[1]
    return pl.pallas_call(
        _merge_kernel,
        out_shape=jax.ShapeDtypeStruct((m, d), _BF16),
        grid=(d // tn, m // tm),
        in_specs=[pl.BlockSpec((tm, km), lambda j, i: (i, 0)),
                  pl.BlockSpec((tm, ka), lambda j, i: (i, 0)),
                  pl.BlockSpec((km, tn), lambda j, i: (0, j)),
                  pl.BlockSpec((ka, tn), lambda j, i: (0, j)),
                  pl.BlockSpec((tm, tn), lambda j, i: (i, gm0 + j)),
                  pl.BlockSpec((tm, tn), lambda j, i: (i, ga0 + j))],
        out_specs=pl.BlockSpec((tm, tn), lambda j, i: (i, j)),
        compiler_params=_params(("parallel", "parallel")),
        name="gated_merge",
    )(hm, ha, wm, wa, proj, proj)


def _resnorm_next_kernel(xa_ref, xb_ref, y_ref, g_ref, g2_ref, x1_ref, h_ref, *, n_first):
    def body(x_ref):
        y = y_ref[...]
        ms = jnp.mean(y * y, axis=-1, keepdims=True)
        x1 = x_ref[...] + y * lax.rsqrt(ms + NORM_EPS) * g_ref[...]
        x1_ref[...] = x1
        ms1 = jnp.mean(x1 * x1, axis=-1, keepdims=True)
        h_ref[...] = (x1 * lax.rsqrt(ms1 + NORM_EPS) * g2_ref[...]).astype(h_ref.dtype)

    i = pl.program_id(0)
    pl.when(i < n_first)(lambda: body(xa_ref))
    pl.when(i >= n_first)(lambda: body(xb_ref))


def _resnorm_next(xa, xb, y, g, g_next):
    m, d = y.shape
    tm = _row_tile(xa.shape[0], xb.shape[0])
    n_first = xa.shape[0] // tm
    row = pl.BlockSpec((tm, d), lambda i: (i, 0))
    vec = pl.BlockSpec((1, d), lambda i: (0, 0))
    return pl.pallas_call(
        functools.partial(_resnorm_next_kernel, n_first=n_first),
        out_shape=(jax.ShapeDtypeStruct((m, d), _F32), jax.ShapeDtypeStruct((m, d), _BF16)),
        grid=(m // tm,), in_specs=[*_two_source_specs(tm, d, n_first), row, vec, vec], out_specs=(row, row),
        compiler_params=_params(("parallel",)), name="resnorm_next",
    )(xa, xb, y, g.reshape(1, d), g_next.reshape(1, d))


def _resnorm_kernel(x_ref, y_ref, g_ref, oa_ref, ob_ref, *, n_first):
    y = y_ref[...]
    ms = jnp.mean(y * y, axis=-1, keepdims=True)
    out = x_ref[...] + y * lax.rsqrt(ms + NORM_EPS) * g_ref[...]
    i = pl.program_id(0)

    @pl.when(i < n_first)
    def _():
        oa_ref[...] = out

    @pl.when(i >= n_first)
    def _():
        ob_ref[...] = out


def _resnorm_split(x, y, g, m_a):
    m, d = x.shape
    m_b = m - m_a
    tm = _row_tile(m_a, m_b)
    n_first = m_a // tm
    row = pl.BlockSpec((tm, d), lambda i: (i, 0))
    vec = pl.BlockSpec((1, d), lambda i: (0, 0))
    return pl.pallas_call(
        functools.partial(_resnorm_kernel, n_first=n_first),
        out_shape=(jax.ShapeDtypeStruct((m_a, d), _F32), jax.ShapeDtypeStruct((m_b, d), _F32)),
        grid=(m // tm,), in_specs=[row, row, vec], out_specs=_two_source_specs(tm, d, n_first),
        compiler_params=_params(("arbitrary",)), name="resnorm",
    )(x, y, g.reshape(1, d))


HALO = SUBLANES


def _conv_gelu_gate(cw_ref, cb_ref, prev2, prev1, gate, lin):
    gc = cw_ref[0:1, :] * prev2 + cw_ref[1:2, :] * prev1 + cw_ref[2:3, :] * gate + cb_ref[...]
    c = -2.0 * math.sqrt(2.0 / math.pi)
    act = gc / (1.0 + jnp.exp(gc * (c + (c * 0.044715) * (gc * gc))))
    return act * lin


def _ffn_up_first_kernel(h_ref, wg_ref, wl_ref, cw_ref, cb_ref, z_ref, tail_ref, wg_bf, wl_bf, g_scr,
                         *, tm, seq_len):
    i = pl.program_id(1)
    tn = z_ref.shape[1]

    @pl.when(i == 0)
    def _():
        wg_bf[...] = wg_ref[...].astype(wg_bf.dtype)
        wl_bf[...] = wl_ref[...].astype(wl_bf.dtype)

    @pl.when((i * tm) % seq_len == 0)
    def _():
        g_scr[0:HALO, :] = jnp.zeros((HALO, tn), _F32)

    h = h_ref[...]
    gate = _dot(h, wg_bf[...])
    lin = _dot(h, wl_bf[...])
    g_scr[HALO:HALO + tm, :] = gate
    zed = _conv_gelu_gate(cw_ref, cb_ref, g_scr[HALO - 2:HALO - 2 + tm, :], g_scr[HALO - 1:HALO - 1 + tm, :],
                          gate, lin)
    z_ref[...] = zed.astype(z_ref.dtype)
    g_scr[0:HALO, :] = gate[tm - HALO:tm, :]
    tail_ref[...] = gate[tm - SUBLANES:tm, :]


def _ffn_up_state_kernel(h_ref, wg_ref, wl_ref, cw_ref, cb_ref, init_ref, z_any, z_ref, tail_ref, g_scr,
                         *, tm, seq_len):
    tn = z_ref.shape[1]
    h = h_ref[...]
    gate = _dot(h, wg_ref[...].astype(_BF16))
    lin = _dot(h, wl_ref[...].astype(_BF16))
    g_scr[0:HALO, :] = jnp.zeros((HALO, tn), _F32)
    g_scr[HALO:HALO + tm, :] = gate
    prev1 = g_scr[HALO - 1:HALO - 1 + tm, :]
    prev2 = g_scr[HALO - 2:HALO - 2 + tm, :]
    nseg = tm // seq_len
    init0 = jnp.concatenate([jnp.broadcast_to(init_ref[s, 0:1, :], (seq_len, tn)) for s in range(nseg)], axis=0)
    init1 = jnp.concatenate([jnp.broadcast_to(init_ref[s, 1:2, :], (seq_len, tn)) for s in range(nseg)], axis=0)
    t = lax.broadcasted_iota(jnp.int32, (tm, 1), 0) % seq_len
    prev1 = jnp.where(t == 0, init1, prev1)
    prev2 = jnp.where(t == 0, init0, jnp.where(t == 1, init1, prev2))
    z_ref[...] = _conv_gelu_gate(cw_ref, cb_ref, prev2, prev1, gate, lin).astype(z_ref.dtype)
    for s in range(nseg):
        tail_ref[s * SUBLANES:(s + 1) * SUBLANES, :] = gate[(s + 1) * seq_len - SUBLANES:(s + 1) * seq_len, :]


def _conv_state(tails, nseq, segs_per_seq, dff):
    tails = tails.reshape(nseq * segs_per_seq, SUBLANES, dff)
    last = tails[segs_per_seq - 1::segs_per_seq, SUBLANES - (CONV_W - 1):, :]
    return last.reshape(nseq, CONV_W - 1, dff)


def _ffn_up_specs(d, tn, nj, h_spec):
    return [h_spec,
            pl.BlockSpec((d, tn), lambda j, i: (0, j)),
            pl.BlockSpec((d, tn), lambda j, i: (0, nj + j)),
            pl.BlockSpec((CONV_W, tn), lambda j, i: (0, j)),
            pl.BlockSpec((1, tn), lambda j, i: (0, j))]


def _ffn_up_first(hf, w_up, conv_w, conv_b, nrows, seq_len):
    m_all, d = hf.shape
    dff = w_up.shape[1] // 2
    tm = _tile(seq_len, 512, SUBLANES)
    tn = _tile(dff, 256)
    nj = dff // tn
    nt = nrows // tm
    specs = _ffn_up_specs(d, tn, nj, pl.BlockSpec((tm, d), lambda j, i: (i, 0)))
    z, tails = pl.pallas_call(
        functools.partial(_ffn_up_first_kernel, tm=tm, seq_len=seq_len),
        out_shape=(jax.ShapeDtypeStruct((m_all, dff), _BF16),
                   jax.ShapeDtypeStruct((nt * SUBLANES, dff), _F32)),
        grid=(nj, nt),
        in_specs=specs,
        out_specs=(pl.BlockSpec((tm, tn), lambda j, i: (i, j)),
                   pl.BlockSpec((SUBLANES, tn), lambda j, i: (i, j))),
        scratch_shapes=[pltpu.VMEM((d, tn), _BF16), pltpu.VMEM((d, tn), _BF16),
                        pltpu.VMEM((HALO + tm, tn), _F32)],
        compiler_params=_params(("parallel", "arbitrary")),
        name="ffn_up_conv_first",
    )(hf, w_up, w_up, conv_w, conv_b.reshape(1, dff))
    return z, _conv_state(tails, nrows // seq_len, seq_len // tm, dff)


def _ffn_up_state(hf, w_up, conv_w, conv_b, row0, nrows, seq_len, init, z_prev):
    m_all, d = hf.shape
    dff = w_up.shape[1] // 2
    tm = _tile(nrows, 512, seq_len)
    tn = _tile(dff, 256)
    nj = dff // tn
    nseg = tm // seq_len
    rb0 = row0 // tm
    specs = _ffn_up_specs(d, tn, nj, pl.BlockSpec((tm, d), lambda j, i: (rb0 + i, 0)))
    specs += [pl.BlockSpec((nseg, CONV_W - 1, tn), lambda j, i: (i, 0, j)), pl.BlockSpec(memory_space=pl.ANY)]
    z, tails = pl.pallas_call(
        functools.partial(_ffn_up_state_kernel, tm=tm, seq_len=seq_len),
        out_shape=(jax.ShapeDtypeStruct((m_all, dff), _BF16),
                   jax.ShapeDtypeStruct((nrows // seq_len * SUBLANES, dff), _F32)),
        grid=(nj, nrows // tm),
        in_specs=specs,
        out_specs=(pl.BlockSpec((tm, tn), lambda j, i: (rb0 + i, j)),
                   pl.BlockSpec((nseg * SUBLANES, tn), lambda j, i: (i, j))),
        scratch_shapes=[pltpu.VMEM((HALO + tm, tn), _F32)],
        input_output_aliases={6: 0},
        compiler_params=_params(("parallel", "arbitrary")),
        name="ffn_up_conv_state",
    )(hf, w_up, w_up, conv_w, conv_b.reshape(1, dff), init, z_prev)
    return z, _conv_state(tails, nrows // seq_len, 1, dff)


def _layer(xa, xb, geom, cache, state, w):
    (B, S, Bd, Td, P) = geom
    (cache_k, cache_v, cache_ik) = cache
    (state_C, state_n, state_m, state_conv) = state
    (w_in, b_igate, b_fgate, g_mhnorm, w_proj_m, w_proj_a, w_out,
     g_pre_mix, g_post_mix, g_pre_ffn, g_post_ffn, w_up, conv_w, conv_b, w_down) = w
    mp, d = xa.shape
    dff = w_down.shape[0]
    wk = A_KV_HEADS * A_HD

    c_mi = A_END
    c_aq = c_mi + 2 * M_HEADS
    c_ik = c_aq + B_END
    c_iw = c_ik + IDX_DIM
    c_gm = c_iw + IDX_HEADS
    w_t = jnp.swapaxes(w_in, 0, 1).astype(_F32)
    w_small_t = jnp.concatenate([w_t[c_ik:c_iw], w_t[c_mi:c_aq], w_t[c_iw:c_gm],
                                 jnp.zeros((LANES - SM_END, d), _F32)], axis=0)
    gate_bias = jnp.concatenate([jnp.zeros((SM_MI,), _F32), b_igate.astype(_F32), b_fgate.astype(_F32),
                                 jnp.zeros((LANES - SM_IW,), _F32)]).reshape(1, LANES)

    hn = _rmsnorm_cast(xa, xb, g_pre_mix)
    proj_a = _matmul_wt(hn, w_t, 0, A_END, _F32, name="in_proj_mlstm")
    proj_b = _matmul_wt(hn, w_t, c_aq, B_END, _F32, name="in_proj_attn")
    proj_c = _matmul_wt(hn, w_t, c_gm, 2 * d, _F32, name="in_proj_gates")
    small = _matmul_wt(hn, w_small_t, 0, LANES, _F32, name="in_proj_small")

    lp_chunk = _tile(S, 256, CHUNK)
    hm_p, c_p, n_p, m_p = _mlstm(proj_a, small, gate_bias, g_mhnorm, 0, B, S, lp_chunk, None)
    s0 = jnp.zeros((Bd * M_HEADS, SUBLANES, M_QK), _F32)
    s0 = s0.at[:, 0, :].set(state_n.reshape(Bd * M_HEADS, M_QK).astype(_F32))
    s0 = s0.at[:, 1, :].set(jnp.broadcast_to(state_m.reshape(Bd * M_HEADS, 1).astype(_F32), (Bd * M_HEADS, M_QK)))
    hm, c_s, n_s, m_s = _mlstm(proj_a, small, gate_bias, g_mhnorm, mp, Bd, Td, min(CHUNK, Td),
                               (state_C.astype(_F32), s0), hm_p)

    pos = jnp.concatenate([jnp.tile(jnp.arange(S, dtype=jnp.int32), B),
                           jnp.tile(P + jnp.arange(Td, dtype=jnp.int32), Bd)])
    q_rot, k_rot, k_bf, v_bf, iqx, ik_rot, ikx = _rope_all(proj_b, small, pos)
    tq_p = _tile(S, 256, CHUNK)
    tk_p = _tile(S, 512)
    ha_p = _attention(q_rot, iqx, small, k_bf, v_bf, ikx, 0, B, S, tq_p, tk_p, S, 0, S)
    l_s = P + Td
    tk_s = 256
    lp_s = -(-l_s // tk_s) * tk_s
    pad = lp_s - l_s
    k_s = jnp.concatenate([cache_k.reshape(Bd, P, wk).astype(_BF16), k_bf[mp:].reshape(Bd, Td, wk),
                           jnp.zeros((Bd, pad, wk), _BF16)], axis=1).reshape(Bd * lp_s, wk)
    v_s = jnp.concatenate([cache_v.reshape(Bd, P, wk).astype(_BF16), v_bf[mp:].reshape(Bd, Td, wk),
                           jnp.zeros((Bd, pad, wk), _BF16)], axis=1).reshape(Bd * lp_s, wk)
    cik = cache_ik.astype(_F32)
    cik_hi = cik.astype(_BF16)
    cik_lo = (cik - cik_hi.astype(_F32)).astype(_BF16)
    cikx = jnp.concatenate([cik_hi, cik_lo, cik_hi, jnp.zeros_like(cik_hi)], axis=-1)
    ikx_s = jnp.concatenate([cikx, ikx[mp:].reshape(Bd, Td, IDX_XW),
                             jnp.zeros((Bd, pad, IDX_XW), _BF16)], axis=1).reshape(Bd * lp_s, IDX_XW)
    ha = _attention(q_rot, iqx, small, k_s, v_s, ikx_s, mp, Bd, Td, Td, tk_s, lp_s, P, l_s, ha_p)

    mix = _merge(hm, ha, w_proj_m.astype(_BF16), w_proj_a.astype(_BF16), proj_c, d)
    y1 = _matmul(mix, w_out.astype(_BF16), _F32, tm=512, tn=1024, name="out_proj")
    x1, hf = _resnorm_next(xa, xb, y1, g_post_mix, g_pre_ffn)

    w_up = w_up.astype(_F32)
    z_p, conv_p = _ffn_up_first(hf, w_up, conv_w, conv_b, mp, S)
    z, conv_s = _ffn_up_state(hf, w_up, conv_w, conv_b, mp, Bd * Td, Td, state_conv.astype(_F32), z_p)
    y2 = _matmul(z, w_down.astype(_BF16), _F32, tm=512, tn=1024, tk=dff // 2, name="down_proj")
    x2 = _resnorm_split(x1, y2, g_post_ffn, mp)

    av = proj_b[:, B_AV:B_AV + wk]
    outs_p = (k_rot[:mp].reshape(B, S, A_KV_HEADS, A_HD), av[:mp].reshape(B, S, A_KV_HEADS, A_HD),
              ik_rot[:mp].reshape(B, S, IDX_DIM), c_p, n_p, m_p, conv_p)
    outs_s = (k_rot[mp:].reshape(Bd, Td, A_KV_HEADS, A_HD), av[mp:].reshape(Bd, Td, A_KV_HEADS, A_HD),
              ik_rot[mp:].reshape(Bd, Td, IDX_DIM), c_s, n_s, m_s, conv_s)
    return x2, outs_p, outs_s


def kernel(x_prompt, x_sample, cache_k, cache_v, cache_idx_k, state_C, state_n, state_m, state_conv,
           w_in, b_igate, b_fgate, g_mhnorm, w_proj_m, w_proj_a, w_out,
           g_pre_mix, g_post_mix, g_pre_ffn, g_post_ffn, w_up, conv_w, conv_b, w_down):
    B, S, d = x_prompt.shape
    Bd, Td, _ = x_sample.shape
    P = cache_k.shape[2]
    depth = w_in.shape[0]
    mp = B * S
    xa, xb = x_prompt.reshape(mp, d), x_sample.reshape(Bd * Td, d)
    all_p, all_s = [], []
    for l in range(depth):
        w = (w_in[l], b_igate[l], b_fgate[l], g_mhnorm[l], w_proj_m[l], w_proj_a[l], w_out[l],
             g_pre_mix[l], g_post_mix[l], g_pre_ffn[l], g_post_ffn[l], w_up[l], conv_w[l], conv_b[l], w_down[l])
        (xa, xb), outs_p, outs_s = _layer(xa, xb, (B, S, Bd, Td, P), (cache_k[l], cache_v[l], cache_idx_k[l]),
                                          (state_C[l], state_n[l], state_m[l], state_conv[l]), w)
        all_p.append(outs_p)
        all_s.append(outs_s)

    def stk(outs, i):
        return jnp.stack([o[i] for o in outs])

    yp = xa.reshape(B, S, d)
    ys = xb.reshape(Bd, Td, d)
    return (yp, ys) + tuple(stk(all_p, i) for i in range(7)) + tuple(stk(all_s, i) for i in range(7))
```

```python
import functools
import math

import jax
import jax.numpy as jnp
from jax import lax
from jax.experimental import pallas as pl
from jax.experimental.pallas import tpu as pltpu

CHUNK = 64
NORM_EPS = 1e-6
ROPE_THETA = 500000.0
M_HEADS = 8
M_QK = 128
M_V = 256
A_HEADS = 16
A_KV_HEADS = 4
A_GROUP = A_HEADS // A_KV_HEADS
A_HD = 128
A_ROT = A_HD // 4
IDX_HEADS = 16
IDX_DIM = 64
IDX_ROT = IDX_DIM // 4
TOPK_MAX = 256
CONV_W = 3

LANES = 128
SUBLANES = 8
VMEM_LIMIT_BYTES = 52 * 1024 * 1024

A_MQ = 0
A_MK = A_MQ + M_HEADS * M_QK
A_MV = A_MK + M_HEADS * M_QK
A_MO = A_MV + M_HEADS * M_V
A_END = A_MO + M_HEADS * M_V
B_AQ = 0
B_AK = B_AQ + A_HEADS * A_HD
B_AV = B_AK + A_KV_HEADS * A_HD
B_IQ = B_AV + A_KV_HEADS * A_HD
B_END = B_IQ + IDX_HEADS * IDX_DIM
SM_IK = 0
SM_MI = SM_IK + IDX_DIM
SM_MF = SM_MI + M_HEADS
SM_IW = SM_MF + M_HEADS
SM_END = SM_IW + IDX_HEADS
IDX_XW = 4 * IDX_DIM

INT_MIN = -2 ** 31
KEY_NEG_INF = INT_MIN + 0x7FFFFF
NEG_BIG = -1e30

_BF16 = jnp.bfloat16
_F32 = jnp.float32


def _tile(dim, target, quantum=LANES):
    if dim <= target:
        return dim
    t = (target // quantum) * quantum
    while t >= quantum:
        if dim % t == 0:
            return t
        t -= quantum
    return dim


def _params(sem):
    return pltpu.CompilerParams(dimension_semantics=sem, vmem_limit_bytes=VMEM_LIMIT_BYTES)


def _dot(a, b):
    return jnp.dot(a, b, preferred_element_type=_F32)


def _dot_nt(a, b):
    return lax.dot_general(a, b, (((1,), (1,)), ((), ())), preferred_element_type=_F32)


def _dot_tn(a, b):
    return lax.dot_general(a, b, (((0,), (0,)), ((), ())), preferred_element_type=_F32)


def _mm_wt_kernel(a_ref, w_ref, o_ref, w_bf):
    @pl.when(pl.program_id(1) == 0)
    def _():
        tn, kd = w_ref.shape
        sq = math.gcd(tn, kd, 512)
        for r in range(0, tn, sq):
            for c in range(0, kd, sq):
                w_bf[c:c + sq, r:r + sq] = w_ref[r:r + sq, c:c + sq].T.astype(w_bf.dtype)

    o_ref[...] = _dot(a_ref[...], w_bf[...]).astype(o_ref.dtype)


def _matmul_wt(a, w_t, row0, n, out_dtype, *, tm=512, tn=512, name="matmul_wt"):
    m, kd = a.shape
    tm = _tile(m, tm, SUBLANES)
    tn = _tile(n, tn)
    return pl.pallas_call(
        _mm_wt_kernel,
        out_shape=jax.ShapeDtypeStruct((m, n), out_dtype),
        grid=(n // tn, m // tm),
        in_specs=[pl.BlockSpec((tm, kd), lambda j, i: (i, 0)),
                  pl.BlockSpec((pl.Element(tn), pl.Element(kd)),
                               lambda j, i: (pl.multiple_of(row0 + j * tn, SUBLANES), 0))],
        out_specs=pl.BlockSpec((tm, tn), lambda j, i: (i, j)),
        scratch_shapes=[pltpu.VMEM((kd, tn), _BF16)],
        compiler_params=_params(("parallel", "arbitrary")),
        name=name,
    )(a, w_t)


def _two_source_specs(tm, d, n_first):
    return (pl.BlockSpec((tm, d), lambda i: (jnp.minimum(i, n_first - 1), 0)),
            pl.BlockSpec((tm, d), lambda i: (jnp.maximum(i - n_first, 0), 0)))


def _rmsnorm_kernel(xa_ref, xb_ref, g_ref, o_ref, *, n_first):
    def body(x_ref):
        x = x_ref[...]
        ms = jnp.mean(x * x, axis=-1, keepdims=True)
        o_ref[...] = (x * lax.rsqrt(ms + NORM_EPS) * g_ref[...]).astype(o_ref.dtype)

    i = pl.program_id(0)
    pl.when(i < n_first)(lambda: body(xa_ref))
    pl.when(i >= n_first)(lambda: body(xb_ref))


def _row_tile(m_a, m_b):
    return _tile(math.gcd(m_a, m_b), 256, SUBLANES)


def _rmsnorm_cast(xa, xb, g):
    (m_a, d), m_b = xa.shape, xb.shape[0]
    tm = _row_tile(m_a, m_b)
    n_first = m_a // tm
    return pl.pallas_call(
        functools.partial(_rmsnorm_kernel, n_first=n_first),
        out_shape=jax.ShapeDtypeStruct((m_a + m_b, d), _BF16),
        grid=((m_a + m_b) // tm,),
        in_specs=[*_two_source_specs(tm, d, n_first), pl.BlockSpec((1, d), lambda i: (0, 0))],
        out_specs=pl.BlockSpec((tm, d), lambda i: (i, 0)),
        compiler_params=_params(("parallel",)),
        name="rmsnorm_cast",
    )(xa, xb, g.reshape(1, d))


def _mm_kernel(a_ref, b_ref, o_ref, acc_ref, *, nk):
    k = pl.program_id(2)

    @pl.when(k == 0)
    def _():
        acc_ref[...] = jnp.zeros_like(acc_ref)

    acc_ref[...] += _dot(a_ref[...], b_ref[...])

    @pl.when(k == nk - 1)
    def _():
        o_ref[...] = acc_ref[...].astype(o_ref.dtype)


def _mm1_kernel(a_ref, b_ref, o_ref):
    o_ref[...] = _dot(a_ref[...], b_ref[...]).astype(o_ref.dtype)


def _matmul(a, b, out_dtype, *, tm=512, tn=1024, tk=None, name="matmul"):
    m, kd = a.shape
    _, n = b.shape
    tm = _tile(m, tm, SUBLANES)
    tn = _tile(n, tn)
    tk = kd if tk is None else _tile(kd, tk)
    nk = kd // tk
    if nk == 1:
        return pl.pallas_call(
            _mm1_kernel,
            out_shape=jax.ShapeDtypeStruct((m, n), out_dtype),
            grid=(n // tn, m // tm),
            in_specs=[pl.BlockSpec((tm, kd), lambda j, i: (i, 0)),
                      pl.BlockSpec((kd, tn), lambda j, i: (0, j))],
            out_specs=pl.BlockSpec((tm, tn), lambda j, i: (i, j)),
            compiler_params=_params(("parallel", "parallel")),
            name=name,
        )(a, b)
    return pl.pallas_call(
        functools.partial(_mm_kernel, nk=nk),
        out_shape=jax.ShapeDtypeStruct((m, n), out_dtype),
        grid=(n // tn, m // tm, nk),
        in_specs=[pl.BlockSpec((tm, tk), lambda j, i, k: (i, k)),
                  pl.BlockSpec((tk, tn), lambda j, i, k: (k, j))],
        out_specs=pl.BlockSpec((tm, tn), lambda j, i, k: (i, j)),
        scratch_shapes=[pltpu.VMEM((tm, tn), _F32)],
        compiler_params=_params(("parallel", "parallel", "arbitrary")),
        name=name,
    )(a, b)


def _mlstm_kernel(*refs, L, has_state, has_alias):
    q_ref, k_ref, v_ref, o_ref, g_ref, gb_ref, gn_ref = refs[:7]
    if has_state:
        c0_ref, s0_ref = refs[7:9]
    h_ref, c_out_ref, s_out_ref, c_scr, n_scr, m_scr = refs[7 + 2 * has_state + has_alias:]
    head = pl.program_id(1)
    c = pl.program_id(2)
    nc = pl.num_programs(2)

    @pl.when(c == 0)
    def _():
        if has_state:
            c_scr[...] = c0_ref[0, 0]
            n_scr[...] = s0_ref[0, 0:1, :]
            m_scr[...] = s0_ref[0, 1:2, :]
        else:
            c_scr[...] = jnp.zeros_like(c_scr)
            n_scr[...] = jnp.zeros_like(n_scr)
            m_scr[...] = jnp.zeros_like(m_scr)

    gates = g_ref[...] + gb_ref[...]
    lane = lax.broadcasted_iota(jnp.int32, gates.shape, 1)
    ig_col = jnp.sum(jnp.where(lane == SM_MI + head, gates, 0.0), axis=1, keepdims=True)
    mf_col = jnp.sum(jnp.where(lane == SM_MF + head, gates, 0.0), axis=1, keepdims=True)
    lf_col = jnp.minimum(mf_col, 0.0) - jnp.log1p(jnp.exp(-jnp.abs(mf_col)))

    ri = lax.broadcasted_iota(jnp.int32, (L, L), 0)
    ci = lax.broadcasted_iota(jnp.int32, (L, L), 1)
    eye = ri == ci
    tril = ci <= ri
    lf_row = jnp.sum(jnp.where(eye, lf_col, 0.0), axis=0, keepdims=True)
    ig_row = jnp.sum(jnp.where(eye, ig_col, 0.0), axis=0, keepdims=True)
    b_col = jnp.sum(jnp.where(tril, lf_row, 0.0), axis=1, keepdims=True)
    b_row = jnp.sum(jnp.where(ri <= ci, lf_col, 0.0), axis=0, keepdims=True)
    logw = jnp.where(tril, b_col - b_row + ig_row, -jnp.inf)

    m_prev = m_scr[:, 0:1]
    inter = b_col + m_prev
    m_t = jnp.maximum(inter, jnp.max(logw, axis=1, keepdims=True))
    a = jnp.exp(inter - m_t)
    sw = jnp.exp(logw - m_t)

    q = q_ref[...]
    k = k_ref[...] * (M_QK ** -0.5)
    v = v_ref[...]
    qb = q.astype(_BF16)
    kb = k.astype(_BF16)
    s = _dot_nt(qb, kb) * sw
    c_old = c_scr[...]
    n_old = n_scr[...]
    num = a * _dot_nt(qb, c_old.astype(_BF16)) + _dot(s.astype(_BF16), v.astype(_BF16))
    den = a * jnp.sum(q * n_old, axis=1, keepdims=True) + jnp.sum(s, axis=1, keepdims=True)
    h = num / jnp.maximum(jnp.abs(den), jnp.exp(-m_t))

    m_new = m_t[L - 1:L, :]
    b_last = b_col[L - 1:L, :]
    g_col = jnp.exp(b_last - b_col + ig_col - m_new)
    decay = jnp.exp(b_last + m_prev - m_new)
    c_new = decay * c_old + _dot_tn((v * g_col).astype(_BF16), kb)
    n_new = decay * n_old + jnp.sum(g_col * k, axis=0, keepdims=True)
    c_scr[...] = c_new
    n_scr[...] = n_new
    m_scr[...] = jnp.broadcast_to(m_new, m_scr.shape)

    ms = jnp.mean(h * h, axis=1, keepdims=True)
    y = h * lax.rsqrt(ms + NORM_EPS) * gn_ref[0]
    h_ref[...] = (y * jax.nn.sigmoid(o_ref[...])).astype(h_ref.dtype)

    @pl.when(c == nc - 1)
    def _():
        c_out_ref[0, 0] = c_new
        s_out_ref[0] = jnp.zeros(s_out_ref.shape[1:], _F32)
        s_out_ref[0, 0:1, :] = n_new
        s_out_ref[0, 1:2, :] = jnp.broadcast_to(m_new, (1, M_QK))


def _mlstm(proj, small, gate_bias, g_mhnorm, row0, nseq, T, L, state, h_prev=None):
    nc = T // L
    rb0 = row0 // L
    has_state = state is not None
    has_alias = h_prev is not None

    def rows(b, h, c):
        return rb0 + b * nc + c

    in_specs = [
        pl.BlockSpec((L, M_QK), lambda b, h, c: (rows(b, h, c), A_MQ // M_QK + h)),
        pl.BlockSpec((L, M_QK), lambda b, h, c: (rows(b, h, c), A_MK // M_QK + h)),
        pl.BlockSpec((L, M_V), lambda b, h, c: (rows(b, h, c), A_MV // M_V + h)),
        pl.BlockSpec((L, M_V), lambda b, h, c: (rows(b, h, c), A_MO // M_V + h)),
        pl.BlockSpec((L, LANES), lambda b, h, c: (rows(b, h, c), 0)),
        pl.BlockSpec((1, LANES), lambda b, h, c: (0, 0)),
        pl.BlockSpec((1, 1, M_V), lambda b, h, c: (h, 0, 0)),
    ]
    args = [proj, proj, proj, proj, small, gate_bias, g_mhnorm.reshape(M_HEADS, 1, M_V)]
    if has_state:
        c0, s0 = state
        in_specs += [pl.BlockSpec((1, 1, M_V, M_QK), lambda b, h, c: (b, h, 0, 0)),
                     pl.BlockSpec((1, SUBLANES, M_QK), lambda b, h, c: (b * M_HEADS + h, 0, 0))]
        args += [c0, s0]
    aliases = {}
    if has_alias:
        aliases = {len(args): 0}
        in_specs.append(pl.BlockSpec(memory_space=pl.ANY))
        args.append(h_prev)
    out_shape = (jax.ShapeDtypeStruct((proj.shape[0], M_HEADS * M_V), _BF16),
                 jax.ShapeDtypeStruct((nseq, M_HEADS, M_V, M_QK), _F32),
                 jax.ShapeDtypeStruct((nseq * M_HEADS, SUBLANES, M_QK), _F32))
    out_specs = (pl.BlockSpec((L, M_V), lambda b, h, c: (rows(b, h, c), h)),
                 pl.BlockSpec((1, 1, M_V, M_QK), lambda b, h, c: (b, h, 0, 0)),
                 pl.BlockSpec((1, SUBLANES, M_QK), lambda b, h, c: (b * M_HEADS + h, 0, 0)))
    hm, c_new, stats = pl.pallas_call(
        functools.partial(_mlstm_kernel, L=L, has_state=has_state, has_alias=has_alias),
        out_shape=out_shape,
        grid=(nseq, M_HEADS, nc),
        in_specs=in_specs,
        out_specs=out_specs,
        scratch_shapes=[pltpu.VMEM((M_V, M_QK), _F32), pltpu.VMEM((1, M_QK), _F32),
                        pltpu.VMEM((1, M_QK), _F32)],
        input_output_aliases=aliases,
        compiler_params=_params(("parallel", "parallel", "arbitrary")),
        name="mlstm",
    )(*args)
    n_new = stats[:, 0, :].reshape(nseq, M_HEADS, M_QK)
    m_new = stats[:, 1, 0].reshape(nseq, M_HEADS)
    return hm, c_new, n_new, m_new


def _rope(x, cos, sin_lo, sin_hi, half):
    n = x.shape[-1]
    return (x * cos + pltpu.roll(x, n - half, 1) * sin_lo + pltpu.roll(x, half, 1) * sin_hi)


def _hi_lo(x):
    hi = x.astype(_BF16).astype(_F32)
    return hi, x - hi


def _rope_kernel(aq_ref, ak_ref, av_ref, iq_ref, sm_ref, ta_ref, ti_ref,
                 q_out, k_out, kb_out, vb_out, iqx_out, ik_out, ikx_out):
    ca, sa_lo, sa_hi = ta_ref[0], ta_ref[1], ta_ref[2]
    ci, si_lo, si_hi = ti_ref[0], ti_ref[1], ti_ref[2]
    for h in range(A_HEADS):
        sl = slice(h * A_HD, (h + 1) * A_HD)
        q_out[:, sl] = _rope(aq_ref[:, sl], ca, sa_lo, sa_hi, A_ROT // 2).astype(q_out.dtype)
    for h in range(A_KV_HEADS):
        sl = slice(h * A_HD, (h + 1) * A_HD)
        kr = _rope(ak_ref[:, sl], ca, sa_lo, sa_hi, A_ROT // 2)
        k_out[:, sl] = kr
        kb_out[:, sl] = kr.astype(kb_out.dtype)
    vb_out[...] = av_ref[...].astype(vb_out.dtype)
    low = lax.broadcasted_iota(jnp.int32, (1, LANES), 1) < IDX_DIM
    for p in range(IDX_HEADS * IDX_DIM // LANES):
        x = _rope(iq_ref[:, p * LANES:(p + 1) * LANES], ci, si_lo, si_hi, IDX_ROT // 2)
        hi, lo = _hi_lo(x)
        hi_sw = pltpu.roll(hi, IDX_DIM, 1)
        lo_sw = pltpu.roll(lo, IDX_DIM, 1)
        c0 = 2 * p * IDX_XW
        iqx_out[:, c0:c0 + LANES] = jnp.where(low, hi, hi_sw).astype(iqx_out.dtype)
        iqx_out[:, c0 + LANES:c0 + 2 * LANES] = jnp.where(low, lo, 0.0).astype(iqx_out.dtype)
        iqx_out[:, c0 + 2 * LANES:c0 + 3 * LANES] = jnp.where(low, hi_sw, hi).astype(iqx_out.dtype)
        iqx_out[:, c0 + 3 * LANES:c0 + 4 * LANES] = jnp.where(low, lo_sw, 0.0).astype(iqx_out.dtype)
    ik = _rope(sm_ref[...], ci, si_lo, si_hi, IDX_ROT // 2)
    ik_out[...] = ik[:, SM_IK:SM_IK + IDX_DIM]
    hi, lo = _hi_lo(ik)
    ikx_out[:, 0:LANES] = jnp.where(low, hi, pltpu.roll(lo, IDX_DIM, 1)).astype(ikx_out.dtype)
    ikx_out[:, LANES:2 * LANES] = jnp.where(low, hi, 0.0).astype(ikx_out.dtype)


def _rope_tables(pos, rot, width, reps_valid):
    half = rot // 2
    inv_freq = jnp.exp(jnp.arange(half, dtype=_F32) * (-2.0 * math.log(ROPE_THETA) / rot))
    ang = pos.astype(_F32)[:, None] * inv_freq[None, :]
    cos, sin = jnp.cos(ang), jnp.sin(ang)
    m = pos.shape[0]
    one = jnp.ones((m, width - rot), _F32)
    zero = jnp.zeros((m, width - rot), _F32)
    zh = jnp.zeros((m, half), _F32)
    c_head = jnp.concatenate([cos, cos, one], axis=1)
    lo_head = jnp.concatenate([-sin, zh, zero], axis=1)
    hi_head = jnp.concatenate([zh, sin, zero], axis=1)
    reps = LANES // width
    ident = (jnp.ones((m, width), _F32), jnp.zeros((m, width), _F32), jnp.zeros((m, width), _F32))
    out = []
    for t, idt in zip((c_head, lo_head, hi_head), ident):
        out.append(jnp.concatenate([t if r < reps_valid else idt for r in range(reps)], axis=1))
    return jnp.stack(out)


def _rope_all(proj, small, pos):
    m = proj.shape[0]
    tm = _tile(m, 256, SUBLANES)
    ta = _rope_tables(pos, A_ROT, A_HD, 1)
    ti = _rope_tables(pos, IDX_ROT, IDX_DIM, LANES // IDX_DIM)
    wq, wk, wi = A_HEADS * A_HD, A_KV_HEADS * A_HD, IDX_HEADS * IDX_DIM
    wix = IDX_HEADS * IDX_XW
    out_shape = (jax.ShapeDtypeStruct((m, wq), _BF16),
                 jax.ShapeDtypeStruct((m, wk), _F32),
                 jax.ShapeDtypeStruct((m, wk), _BF16),
                 jax.ShapeDtypeStruct((m, wk), _BF16),
                 jax.ShapeDtypeStruct((m, wix), _BF16),
                 jax.ShapeDtypeStruct((m, IDX_DIM), _F32),
                 jax.ShapeDtypeStruct((m, IDX_XW), _BF16))
    return pl.pallas_call(
        _rope_kernel,
        out_shape=out_shape,
        grid=(m // tm,),
        in_specs=[pl.BlockSpec((tm, wq), lambda i: (i, B_AQ // wq)),
                  pl.BlockSpec((tm, wk), lambda i: (i, B_AK // wk)),
                  pl.BlockSpec((tm, wk), lambda i: (i, B_AV // wk)),
                  pl.BlockSpec((tm, wi), lambda i: (i, B_IQ // wi)),
                  pl.BlockSpec((tm, LANES), lambda i: (i, 0)),
                  pl.BlockSpec((3, tm, LANES), lambda i: (0, i, 0)),
                  pl.BlockSpec((3, tm, LANES), lambda i: (0, i, 0))],
        out_specs=(pl.BlockSpec((tm, wq), lambda i: (i, 0)),
                   pl.BlockSpec((tm, wk), lambda i: (i, 0)),
                   pl.BlockSpec((tm, wk), lambda i: (i, 0)),
                   pl.BlockSpec((tm, wk), lambda i: (i, 0)),
                   pl.BlockSpec((tm, wix), lambda i: (i, 0)),
                   pl.BlockSpec((tm, IDX_DIM), lambda i: (i, 0)),
                   pl.BlockSpec((tm, IDX_XW), lambda i: (i, 0))),
        compiler_params=_params(("parallel",)),
        name="rope",
    )(proj, proj, proj, proj, small, ta, ti)


def _attn_kernel(*refs, TQ, TK, past_len, l_valid, topk):
    q_ref, iqx_ref, sm_ref, k_ref, v_ref, ikx_ref = refs[:6]
    o_ref, key_scr, bias_scr, w_scr = refs[-4:]
    i = pl.program_id(1)
    q_start = past_len + i * TQ
    qpos = q_start + lax.broadcasted_iota(jnp.int32, (TQ, 1), 0)
    qchunk = qpos // CHUNK
    last_vis = jnp.minimum(((q_start + TQ - 1) // CHUNK + 1) * CHUNK, l_valid)
    nkt = (last_vis + TK - 1) // TK
    lane_pos = lax.broadcasted_iota(jnp.int32, (1, TK), 1)

    w_scale = (IDX_DIM ** -0.5) * (IDX_HEADS ** -0.5)
    for h in range(IDX_HEADS):
        w_scr[h] = jnp.broadcast_to(sm_ref[:, SM_IW + h:SM_IW + h + 1] * w_scale, (TQ, LANES))
    lane128 = lax.broadcasted_iota(jnp.int32, (1, LANES), 1)
    sc = min(TK, 2 * LANES)

    def score_tile(kt, carry):
        for cc in range(TK // sc):
            k0 = pl.multiple_of(kt * TK + cc * sc, sc)
            ik_c = ikx_ref[pl.ds(k0, sc), :]
            accs = [jnp.zeros((TQ, LANES), _F32) for _ in range(sc // LANES)]
            for h in range(IDX_HEADS):
                isc = _dot_nt(iqx_ref[:, h * IDX_XW:(h + 1) * IDX_XW], ik_c)
                w_h = w_scr[h]
                for c in range(sc // LANES):
                    accs[c] = accs[c] + jnp.maximum(isc[:, c * LANES:(c + 1) * LANES], 0.0) * w_h
            for c in range(sc // LANES):
                bits = pltpu.bitcast(accs[c] + 0.0, jnp.int32)
                key = jnp.where(bits < 0, bits ^ 0x7FFFFFFF, bits)
                kpos = k0 + c * LANES + lane128
                vis = ((kpos // CHUNK) <= qchunk) & (kpos < l_valid)
                col = cc * sc + c * LANES
                key_scr[kt, :, col:col + LANES] = jnp.where(vis, key, INT_MIN)
        return carry

    lax.fori_loop(0, nkt, score_tile, 0)

    def lane_fold(x):
        out = x[:, 0:LANES]
        for c in range(1, TK // LANES):
            out = out + x[:, c * LANES:(c + 1) * LANES]
        return out

    def count(pred_fn):
        def body(kt, part):
            return part + lane_fold(pred_fn(key_scr[kt], kt).astype(jnp.int32))
        part = lax.fori_loop(0, nkt, body, jnp.zeros((TQ, LANES), jnp.int32))
        return jnp.sum(part, axis=1, keepdims=True)

    def bit_step(it, thr_u):
        cand_u = thr_u | lax.shift_left(jnp.int32(1), 31 - it)
        cand_s = cand_u ^ INT_MIN
        cnt = count(lambda key, kt: key >= cand_s)
        return jnp.where(cnt >= topk, cand_u, thr_u)

    thr = lax.fori_loop(0, 32, bit_step, jnp.zeros((TQ, 1), jnp.int32)) ^ INT_MIN
    n_ge = count(lambda key, kt: key >= thr)

    def tie_search(_):
        need = topk - count(lambda key, kt: key > thr)
        nbits = (key_scr.shape[0] * TK).bit_length()

        def pos_step(it, p):
            cand = p | lax.shift_left(jnp.int32(1), nbits - 1 - it)
            cnt = count(lambda key, kt: (key == thr) & ((kt * TK + lane_pos) < cand))
            return jnp.where(cnt < need, cand, p)

        return lax.fori_loop(0, nbits, pos_step, jnp.zeros((TQ, 1), jnp.int32))

    has_tie = jnp.max(n_ge.astype(_F32)) > topk
    p_last = lax.cond(has_tie, tie_search, lambda _: jnp.full((TQ, 1), 2 ** 31 - 1, jnp.int32), 0)

    def bias_tile(kt, carry):
        key = key_scr[kt]
        sel = ((key > thr) | ((key == thr) & ((kt * TK + lane_pos) <= p_last))) & (key > KEY_NEG_INF)
        bias_scr[kt] = jnp.where(sel, 0.0, NEG_BIG)
        return carry

    lax.fori_loop(0, nkt, bias_tile, 0)

    scale = (A_HD ** -0.5) * math.log2(math.e)
    rows = A_GROUP * TQ
    n_par = 1

    def q_group(g):
        return jnp.concatenate(
            [q_ref[:, (g * A_GROUP + j) * A_HD:(g * A_GROUP + j + 1) * A_HD] for j in range(A_GROUP)], axis=0)

    for g0 in range(0, A_KV_HEADS, n_par):
        qgs = [q_group(g0 + u) for u in range(n_par)]

        def attn_tile(kt, carry, g0=g0, qgs=qgs):
            k0 = pl.multiple_of(kt * TK, TK)
            bias = bias_scr[kt][None]
            s_u = [_dot_nt(qgs[u], k_ref[pl.ds(k0, TK), (g0 + u) * A_HD:(g0 + u + 1) * A_HD])
                   .reshape(A_GROUP, TQ, TK) * scale + bias for u in range(n_par)]
            new = []
            p_u = []
            for u in range(n_par):
                m_i, l_i, acc = carry[u]
                m_new = jnp.maximum(m_i, jnp.max(s_u[u], axis=-1, keepdims=True))
                alpha = jnp.exp2(m_i - m_new)
                p = jnp.exp2(s_u[u] - m_new)
                p_u.append(p.reshape(rows, TK).astype(_BF16))
                new.append((m_new, alpha * l_i + jnp.sum(p, axis=-1, keepdims=True), alpha * acc))
            out = []
            for u in range(n_par):
                v_t = v_ref[pl.ds(k0, TK), (g0 + u) * A_HD:(g0 + u + 1) * A_HD]
                m_new, l_new, acc = new[u]
                out.append((m_new, l_new, acc + _dot(p_u[u], v_t).reshape(A_GROUP, TQ, A_HD)))
            return tuple(out)

        init = tuple((jnp.full((A_GROUP, TQ, 1), NEG_BIG, _F32), jnp.zeros((A_GROUP, TQ, 1), _F32),
                      jnp.zeros((A_GROUP, TQ, A_HD), _F32)) for _ in range(n_par))
        fin = lax.fori_loop(0, nkt, attn_tile, init)
        for u in range(n_par):
            _, l_f, acc_f = fin[u]
            out = acc_f / l_f
            for j in range(A_GROUP):
                hh = (g0 + u) * A_GROUP + j
                o_ref[:, hh * A_HD:(hh + 1) * A_HD] = out[j].astype(o_ref.dtype)


def _attention(q_rot, iq_rot, small, k_all, v_all, ik_all, row0, nseq, T, TQ, TK, lp, past_len, l_valid, o_prev=None):
    nq = T // TQ
    qb0 = row0 // TQ
    topk = min(TOPK_MAX, l_valid // 4)
    wq, wk, wix = A_HEADS * A_HD, A_KV_HEADS * A_HD, IDX_HEADS * IDX_XW
    in_specs = [pl.BlockSpec((TQ, wq), lambda b, i: (qb0 + b * nq + i, 0)),
                pl.BlockSpec((TQ, wix), lambda b, i: (qb0 + b * nq + i, 0)),
                pl.BlockSpec((TQ, LANES), lambda b, i: (qb0 + b * nq + i, 0)),
                pl.BlockSpec((lp, wk), lambda b, i: (b, 0)),
                pl.BlockSpec((lp, wk), lambda b, i: (b, 0)),
                pl.BlockSpec((lp, IDX_XW), lambda b, i: (b, 0))]
    args = [q_rot, iq_rot, small, k_all, v_all, ik_all]
    aliases = {}
    if o_prev is not None:
        aliases = {len(args): 0}
        in_specs.append(pl.BlockSpec(memory_space=pl.ANY))
        args.append(o_prev)
    return pl.pallas_call(
        functools.partial(_attn_kernel, TQ=TQ, TK=TK, past_len=past_len, l_valid=l_valid, topk=topk),
        out_shape=jax.ShapeDtypeStruct((q_rot.shape[0], wq), _BF16),
        grid=(nseq, nq),
        in_specs=in_specs,
        out_specs=pl.BlockSpec((TQ, wq), lambda b, i: (qb0 + b * nq + i, 0)),
        scratch_shapes=[pltpu.VMEM((lp // TK, TQ, TK), jnp.int32),
                        pltpu.VMEM((lp // TK, TQ, TK), _F32),
                        pltpu.VMEM((IDX_HEADS, TQ, LANES), _F32)],
        input_output_aliases=aliases,
        compiler_params=_params(("parallel", "arbitrary")),
        name="sparse_attention",
    )(*args)


def _merge_kernel(hm_ref, ha_ref, wm_ref, wa_ref, gm_ref, ga_ref, o_ref):
    pm = _dot(hm_ref[...], wm_ref[...])
    pa = _dot(ha_ref[...], wa_ref[...])
    o_ref[...] = (jax.nn.sigmoid(gm_ref[...]) * pm + jax.nn.sigmoid(ga_ref[...]) * pa).astype(o_ref.dtype)


def _merge(hm, ha, wm, wa, proj, d):
    m = hm.shape[0]
    tm = _tile(m, 512, SUBLANES)
    tn = _tile(d, 1024)
    gm0 = 0
    ga0 = d // tn
    km, ka = hm.shape[1], ha.shape[1]
    return pl.pallas_call(
        _merge_kernel,
        out_shape=jax.ShapeDtypeStruct((m, d), _BF16),
        grid=(d // tn, m // tm),
        in_specs=[pl.BlockSpec((tm, km), lambda j, i: (i, 0)),
                  pl.BlockSpec((tm, ka), lambda j, i: (i, 0)),
                  pl.BlockSpec((km, tn), lambda j, i: (0, j)),
                  pl.BlockSpec((ka, tn), lambda j, i: (0, j)),
                  pl.BlockSpec((tm, tn), lambda j, i: (i, gm0 + j)),
                  pl.BlockSpec((tm, tn), lambda j, i: (i, ga0 + j))],
        out_specs=pl.BlockSpec((tm, tn), lambda j, i: (i, j)),
        compiler_params=_params(("parallel", "parallel")),
        name="gated_merge",
    )(hm, ha, wm, wa, proj, proj)


def _resnorm_next_kernel(xa_ref, xb_ref, y_ref, g_ref, g2_ref, x1_ref, h_ref, *, n_first):
    def body(x_ref):
        y = y_ref[...]
        ms = jnp.mean(y * y, axis=-1, keepdims=True)
        x1 = x_ref[...] + y * lax.rsqrt(ms + NORM_EPS) * g_ref[...]
        x1_ref[...] = x1
        ms1 = jnp.mean(x1 * x1, axis=-1, keepdims=True)
        h_ref[...] = (x1 * lax.rsqrt(ms1 + NORM_EPS) * g2_ref[...]).astype(h_ref.dtype)

    i = pl.program_id(0)
    pl.when(i < n_first)(lambda: body(xa_ref))
    pl.when(i >= n_first)(lambda: body(xb_ref))


def _resnorm_next(xa, xb, y, g, g_next):
    m, d = y.shape
    tm = _row_tile(xa.shape[0], xb.shape[0])
    n_first = xa.shape[0] // tm
    row = pl.BlockSpec((tm, d), lambda i: (i, 0))
    vec = pl.BlockSpec((1, d), lambda i: (0, 0))
    return pl.pallas_call(
        functools.partial(_resnorm_next_kernel, n_first=n_first),
        out_shape=(jax.ShapeDtypeStruct((m, d), _F32), jax.ShapeDtypeStruct((m, d), _BF16)),
        grid=(m // tm,), in_specs=[*_two_source_specs(tm, d, n_first), row, vec, vec], out_specs=(row, row),
        compiler_params=_params(("parallel",)), name="resnorm_next",
    )(xa, xb, y, g.reshape(1, d), g_next.reshape(1, d))


def _resnorm_kernel(x_ref, y_ref, g_ref, oa_ref, ob_ref, *, n_first):
    y = y_ref[...]
    ms = jnp.mean(y * y, axis=-1, keepdims=True)
    out = x_ref[...] + y * lax.rsqrt(ms + NORM_EPS) * g_ref[...]
    i = pl.program_id(0)

    @pl.when(i < n_first)
    def _():
        oa_ref[...] = out

    @pl.when(i >= n_first)
    def _():
        ob_ref[...] = out


def _resnorm_split(x, y, g, m_a):
    m, d = x.shape
    m_b = m - m_a
    tm = _row_tile(m_a, m_b)
    n_first = m_a // tm
    row = pl.BlockSpec((tm, d), lambda i: (i, 0))
    vec = pl.BlockSpec((1, d), lambda i: (0, 0))
    return pl.pallas_call(
        functools.partial(_resnorm_kernel, n_first=n_first),
        out_shape=(jax.ShapeDtypeStruct((m_a, d), _F32), jax.ShapeDtypeStruct((m_b, d), _F32)),
        grid=(m // tm,), in_specs=[row, row, vec], out_specs=_two_source_specs(tm, d, n_first),
        compiler_params=_params(("arbitrary",)), name="resnorm",
    )(x, y, g.reshape(1, d))


HALO = SUBLANES


def _conv_gelu_gate(cw_ref, cb_ref, prev2, prev1, gate, lin):
    gc = cw_ref[0:1, :] * prev2 + cw_ref[1:2, :] * prev1 + cw_ref[2:3, :] * gate + cb_ref[...]
    c = -2.0 * math.sqrt(2.0 / math.pi)
    act = gc / (1.0 + jnp.exp(gc * (c + (c * 0.044715) * (gc * gc))))
    return act * lin


def _ffn_up_first_kernel(h_ref, wg_ref, wl_ref, cw_ref, cb_ref, z_ref, tail_ref, wg_bf, wl_bf, g_scr,
                         *, tm, seq_len):
    i = pl.program_id(1)
    tn = z_ref.shape[1]

    @pl.when(i == 0)
    def _():
        wg_bf[...] = wg_ref[...].astype(wg_bf.dtype)
        wl_bf[...] = wl_ref[...].astype(wl_bf.dtype)

    @pl.when((i * tm) % seq_len == 0)
    def _():
        g_scr[0:HALO, :] = jnp.zeros((HALO, tn), _F32)

    h = h_ref[...]
    gate = _dot(h, wg_bf[...])
    lin = _dot(h, wl_bf[...])
    g_scr[HALO:HALO + tm, :] = gate
    zed = _conv_gelu_gate(cw_ref, cb_ref, g_scr[HALO - 2:HALO - 2 + tm, :], g_scr[HALO - 1:HALO - 1 + tm, :],
                          gate, lin)
    z_ref[...] = zed.astype(z_ref.dtype)
    g_scr[0:HALO, :] = gate[tm - HALO:tm, :]
    tail_ref[...] = gate[tm - SUBLANES:tm, :]


def _ffn_up_state_kernel(h_ref, wg_ref, wl_ref, cw_ref, cb_ref, init_ref, z_any, z_ref, tail_ref, g_scr,
                         *, tm, seq_len):
    tn = z_ref.shape[1]
    h = h_ref[...]
    gate = _dot(h, wg_ref[...].astype(_BF16))
    lin = _dot(h, wl_ref[...].astype(_BF16))
    g_scr[0:HALO, :] = jnp.zeros((HALO, tn), _F32)
    g_scr[HALO:HALO + tm, :] = gate
    prev1 = g_scr[HALO - 1:HALO - 1 + tm, :]
    prev2 = g_scr[HALO - 2:HALO - 2 + tm, :]
    nseg = tm // seq_len
    init0 = jnp.concatenate([jnp.broadcast_to(init_ref[s, 0:1, :], (seq_len, tn)) for s in range(nseg)], axis=0)
    init1 = jnp.concatenate([jnp.broadcast_to(init_ref[s, 1:2, :], (seq_len, tn)) for s in range(nseg)], axis=0)
    t = lax.broadcasted_iota(jnp.int32, (tm, 1), 0) % seq_len
    prev1 = jnp.where(t == 0, init1, prev1)
    prev2 = jnp.where(t == 0, init0, jnp.where(t == 1, init1, prev2))
    z_ref[...] = _conv_gelu_gate(cw_ref, cb_ref, prev2, prev1, gate, lin).astype(z_ref.dtype)
    for s in range(nseg):
        tail_ref[s * SUBLANES:(s + 1) * SUBLANES, :] = gate[(s + 1) * seq_len - SUBLANES:(s + 1) * seq_len, :]


def _conv_state(tails, nseq, segs_per_seq, dff):
    tails = tails.reshape(nseq * segs_per_seq, SUBLANES, dff)
    last = tails[segs_per_seq - 1::segs_per_seq, SUBLANES - (CONV_W - 1):, :]
    return last.reshape(nseq, CONV_W - 1, dff)


def _ffn_up_specs(d, tn, nj, h_spec):
    return [h_spec,
            pl.BlockSpec((d, tn), lambda j, i: (0, j)),
            pl.BlockSpec((d, tn), lambda j, i: (0, nj + j)),
            pl.BlockSpec((CONV_W, tn), lambda j, i: (0, j)),
            pl.BlockSpec((1, tn), lambda j, i: (0, j))]


def _ffn_up_first(hf, w_up, conv_w, conv_b, nrows, seq_len):
    m_all, d = hf.shape
    dff = w_up.shape[1] // 2
    tm = _tile(seq_len, 512, SUBLANES)
    tn = _tile(dff, 256)
    nj = dff // tn
    nt = nrows // tm
    specs = _ffn_up_specs(d, tn, nj, pl.BlockSpec((tm, d), lambda j, i: (i, 0)))
    z, tails = pl.pallas_call(
        functools.partial(_ffn_up_first_kernel, tm=tm, seq_len=seq_len),
        out_shape=(jax.ShapeDtypeStruct((m_all, dff), _BF16),
                   jax.ShapeDtypeStruct((nt * SUBLANES, dff), _F32)),
        grid=(nj, nt),
        in_specs=specs,
        out_specs=(pl.BlockSpec((tm, tn), lambda j, i: (i, j)),
                   pl.BlockSpec((SUBLANES, tn), lambda j, i: (i, j))),
        scratch_shapes=[pltpu.VMEM((d, tn), _BF16), pltpu.VMEM((d, tn), _BF16),
                        pltpu.VMEM((HALO + tm, tn), _F32)],
        compiler_params=_params(("parallel", "arbitrary")),
        name="ffn_up_conv_first",
    )(hf, w_up, w_up, conv_w, conv_b.reshape(1, dff))
    return z, _conv_state(tails, nrows // seq_len, seq_len // tm, dff)


def _ffn_up_state(hf, w_up, conv_w, conv_b, row0, nrows, seq_len, init, z_prev):
    m_all, d = hf.shape
    dff = w_up.shape[1] // 2
    tm = _tile(nrows, 512, seq_len)
    tn = _tile(dff, 256)
    nj = dff // tn
    nseg = tm // seq_len
    rb0 = row0 // tm
    specs = _ffn_up_specs(d, tn, nj, pl.BlockSpec((tm, d), lambda j, i: (rb0 + i, 0)))
    specs += [pl.BlockSpec((nseg, CONV_W - 1, tn), lambda j, i: (i, 0, j)), pl.BlockSpec(memory_space=pl.ANY)]
    z, tails = pl.pallas_call(
        functools.partial(_ffn_up_state_kernel, tm=tm, seq_len=seq_len),
        out_shape=(jax.ShapeDtypeStruct((m_all, dff), _BF16),
                   jax.ShapeDtypeStruct((nrows // seq_len * SUBLANES, dff), _F32)),
        grid=(nj, nrows // tm),
        in_specs=specs,
        out_specs=(pl.BlockSpec((tm, tn), lambda j, i: (rb0 + i, j)),
                   pl.BlockSpec((nseg * SUBLANES, tn), lambda j, i: (i, j))),
        scratch_shapes=[pltpu.VMEM((HALO + tm, tn), _F32)],
        input_output_aliases={6: 0},
        compiler_params=_params(("parallel", "arbitrary")),
        name="ffn_up_conv_state",
    )(hf, w_up, w_up, conv_w, conv_b.reshape(1, dff), init, z_prev)
    return z, _conv_state(tails, nrows // seq_len, 1, dff)


def _layer(xa, xb, geom, cache, state, w):
    (B, S, Bd, Td, P) = geom
    (cache_k, cache_v, cache_ik) = cache
    (state_C, state_n, state_m, state_conv) = state
    (w_in, b_igate, b_fgate, g_mhnorm, w_proj_m, w_proj_a, w_out,
     g_pre_mix, g_post_mix, g_pre_ffn, g_post_ffn, w_up, conv_w, conv_b, w_down) = w
    mp, d = xa.shape
    dff = w_down.shape[0]
    wk = A_KV_HEADS * A_HD

    c_mi = A_END
    c_aq = c_mi + 2 * M_HEADS
    c_ik = c_aq + B_END
    c_iw = c_ik + IDX_DIM
    c_gm = c_iw + IDX_HEADS
    w_t = jnp.swapaxes(w_in, 0, 1).astype(_F32)
    w_small_t = jnp.concatenate([w_t[c_ik:c_iw], w_t[c_mi:c_aq], w_t[c_iw:c_gm],
                                 jnp.zeros((LANES - SM_END, d), _F32)], axis=0)
    gate_bias = jnp.concatenate([jnp.zeros((SM_MI,), _F32), b_igate.astype(_F32), b_fgate.astype(_F32),
                                 jnp.zeros((LANES - SM_IW,), _F32)]).reshape(1, LANES)

    hn = _rmsnorm_cast(xa, xb, g_pre_mix)
    proj_a = _matmul_wt(hn, w_t, 0, A_END, _F32, name="in_proj_mlstm")
    proj_b = _matmul_wt(hn, w_t, c_aq, B_END, _F32, name="in_proj_attn")
    proj_c = _matmul_wt(hn, w_t, c_gm, 2 * d, _F32, name="in_proj_gates")
    small = _matmul_wt(hn, w_small_t, 0, LANES, _F32, name="in_proj_small")

    lp_chunk = _tile(S, 256, CHUNK)
    hm_p, c_p, n_p, m_p = _mlstm(proj_a, small, gate_bias, g_mhnorm, 0, B, S, lp_chunk, None)
    s0 = jnp.zeros((Bd * M_HEADS, SUBLANES, M_QK), _F32)
    s0 = s0.at[:, 0, :].set(state_n.reshape(Bd * M_HEADS, M_QK).astype(_F32))
    s0 = s0.at[:, 1, :].set(jnp.broadcast_to(state_m.reshape(Bd * M_HEADS, 1).astype(_F32), (Bd * M_HEADS, M_QK)))
    hm, c_s, n_s, m_s = _mlstm(proj_a, small, gate_bias, g_mhnorm, mp, Bd, Td, min(CHUNK, Td),
                               (state_C.astype(_F32), s0), hm_p)

    pos = jnp.concatenate([jnp.tile(jnp.arange(S, dtype=jnp.int32), B),
                           jnp.tile(P + jnp.arange(Td, dtype=jnp.int32), Bd)])
    q_rot, k_rot, k_bf, v_bf, iqx, ik_rot, ikx = _rope_all(proj_b, small, pos)
    tq_p = _tile(S, 256, CHUNK)
    tk_p = _tile(S, 512)
    ha_p = _attention(q_rot, iqx, small, k_bf, v_bf, ikx, 0, B, S, tq_p, tk_p, S, 0, S)
    l_s = P + Td
    tk_s = 256
    lp_s = -(-l_s // tk_s) * tk_s
    pad = lp_s - l_s
    k_s = jnp.concatenate([cache_k.reshape(Bd, P, wk).astype(_BF16), k_bf[mp:].reshape(Bd, Td, wk),
                           jnp.zeros((Bd, pad, wk), _BF16)], axis=1).reshape(Bd * lp_s, wk)
    v_s = jnp.concatenate([cache_v.reshape(Bd, P, wk).astype(_BF16), v_bf[mp:].reshape(Bd, Td, wk),
                           jnp.zeros((Bd, pad, wk), _BF16)], axis=1).reshape(Bd * lp_s, wk)
    cik = cache_ik.astype(_F32)
    cik_hi = cik.astype(_BF16)
    cik_lo = (cik - cik_hi.astype(_F32)).astype(_BF16)
    cikx = jnp.concatenate([cik_hi, cik_lo, cik_hi, jnp.zeros_like(cik_hi)], axis=-1)
    ikx_s = jnp.concatenate([cikx, ikx[mp:].reshape(Bd, Td, IDX_XW),
                             jnp.zeros((Bd, pad, IDX_XW), _BF16)], axis=1).reshape(Bd * lp_s, IDX_XW)
    ha = _attention(q_rot, iqx, small, k_s, v_s, ikx_s, mp, Bd, Td, Td, tk_s, lp_s, P, l_s, ha_p)

    mix = _merge(hm, ha, w_proj_m.astype(_BF16), w_proj_a.astype(_BF16), proj_c, d)
    y1 = _matmul(mix, w_out.astype(_BF16), _F32, tm=512, tn=1024, name="out_proj")
    x1, hf = _resnorm_next(xa, xb, y1, g_post_mix, g_pre_ffn)

    w_up = w_up.astype(_F32)
    z_p, conv_p = _ffn_up_first(hf, w_up, conv_w, conv_b, mp, S)
    z, conv_s = _ffn_up_state(hf, w_up, conv_w, conv_b, mp, Bd * Td, Td, state_conv.astype(_F32), z_p)
    y2 = _matmul(z, w_down.astype(_BF16), _F32, tm=512, tn=1024, tk=dff // 2, name="down_proj")
    x2 = _resnorm_split(x1, y2, g_post_ffn, mp)

    av = proj_b[:, B_AV:B_AV + wk]
    outs_p = (k_rot[:mp].reshape(B, S, A_KV_HEADS, A_HD), av[:mp].reshape(B, S, A_KV_HEADS, A_HD),
              ik_rot[:mp].reshape(B, S, IDX_DIM), c_p, n_p, m_p, conv_p)
    outs_s = (k_rot[mp:].reshape(Bd, Td, A_KV_HEADS, A_HD), av[mp:].reshape(Bd, Td, A_KV_HEADS, A_HD),
              ik_rot[mp:].reshape(Bd, Td, IDX_DIM), c_s, n_s, m_s, conv_s)
    return x2, outs_p, outs_s


def kernel(x_prompt, x_sample, cache_k, cache_v, cache_idx_k, state_C, state_n, state_m, state_conv,
           w_in, b_igate, b_fgate, g_mhnorm, w_proj_m, w_proj_a, w_out,
           g_pre_mix, g_post_mix, g_pre_ffn, g_post_ffn, w_up, conv_w, conv_b, w_down):
    B, S, d = x_prompt.shape
    Bd, Td, _ = x_sample.shape
    P = cache_k.shape[2]
    depth = w_in.shape[0]
    mp = B * S
    xa, xb = x_prompt.reshape(mp, d), x_sample.reshape(Bd * Td, d)
    all_p, all_s = [], []
    for l in range(depth):
        w = (w_in[l], b_igate[l], b_fgate[l], g_mhnorm[l], w_proj_m[l], w_proj_a[l], w_out[l],
             g_pre_mix[l], g_post_mix[l], g_pre_ffn[l], g_post_ffn[l], w_up[l], conv_w[l], conv_b[l], w_down[l])
        (xa, xb), outs_p, outs_s = _layer(xa, xb, (B, S, Bd, Td, P), (cache_k[l], cache_v[l], cache_idx_k[l]),
                                          (state_C[l], state_n[l], state_m[l], state_conv[l]), w)
        all_p.append(outs_p)
        all_s.append(outs_s)

    def stk(outs, i):
        return jnp.stack([o[i] for o in outs])

    yp = xa.reshape(B, S, d)
    ys = xb.reshape(Bd, Td, d)
    return (yp, ys) + tuple(stk(all_p, i) for i in range(7)) + tuple(stk(all_s, i) for i in range(7))
```

```python
import functools
import math

import jax
import jax.numpy as jnp
from jax import lax
from jax.experimental import pallas as pl
from jax.experimental.pallas import tpu as pltpu

CHUNK = 64
NORM_EPS = 1e-6
ROPE_THETA = 500000.0
M_HEADS = 8
M_QK = 128
M_V = 256
A_HEADS = 16
A_KV_HEADS = 4
A_GROUP = A_HEADS // A_KV_HEADS
A_HD = 128
A_ROT = A_HD // 4
IDX_HEADS = 16
IDX_DIM = 64
IDX_ROT = IDX_DIM // 4
TOPK_MAX = 256
CONV_W = 3

LANES = 128
SUBLANES = 8
VMEM_LIMIT_BYTES = 52 * 1024 * 1024

A_MQ = 0
A_MK = A_MQ + M_HEADS * M_QK
A_MV = A_MK + M_HEADS * M_QK
A_MO = A_MV + M_HEADS * M_V
A_END = A_MO + M_HEADS * M_V
B_AQ = 0
B_AK = B_AQ + A_HEADS * A_HD
B_AV = B_AK + A_KV_HEADS * A_HD
B_IQ = B_AV + A_KV_HEADS * A_HD
B_END = B_IQ + IDX_HEADS * IDX_DIM
SM_IK = 0
SM_MI = SM_IK + IDX_DIM
SM_MF = SM_MI + M_HEADS
SM_IW = SM_MF + M_HEADS
SM_END = SM_IW + IDX_HEADS
IDX_XW = 4 * IDX_DIM

INT_MIN = -2 ** 31
KEY_NEG_INF = INT_MIN + 0x7FFFFF
NEG_BIG = -1e30

_BF16 = jnp.bfloat16
_F32 = jnp.float32


def _tile(dim, target, quantum=LANES):
    if dim <= target:
        return dim
    t = (target // quantum) * quantum
    while t >= quantum:
        if dim % t == 0:
            return t
        t -= quantum
    return dim


def _params(sem):
    return pltpu.CompilerParams(dimension_semantics=sem, vmem_limit_bytes=VMEM_LIMIT_BYTES)


def _dot(a, b):
    return jnp.dot(a, b, preferred_element_type=_F32)


def _dot_nt(a, b):
    return lax.dot_general(a, b, (((1,), (1,)), ((), ())), preferred_element_type=_F32)


def _dot_tn(a, b):
    return lax.dot_general(a, b, (((0,), (0,)), ((), ())), preferred_element_type=_F32)


def _mm_wt_kernel(a_ref, w_ref, o_ref, w_bf):
    @pl.when(pl.program_id(1) == 0)
    def _():
        tn, kd = w_ref.shape
        sq = math.gcd(tn, kd, 512)
        for r in range(0, tn, sq):
            for c in range(0, kd, sq):
                w_bf[c:c + sq, r:r + sq] = w_ref[r:r + sq, c:c + sq].T.astype(w_bf.dtype)

    o_ref[...] = _dot(a_ref[...], w_bf[...]).astype(o_ref.dtype)


def _matmul_wt(a, w_t, row0, n, out_dtype, *, tm=512, tn=1024, name="matmul_wt"):
    m, kd = a.shape
    tm = _tile(m, tm, SUBLANES)
    tn = _tile(n, tn)
    return pl.pallas_call(
        _mm_wt_kernel,
        out_shape=jax.ShapeDtypeStruct((m, n), out_dtype),
        grid=(n // tn, m // tm),
        in_specs=[pl.BlockSpec((tm, kd), lambda j, i: (i, 0)),
                  pl.BlockSpec((pl.Element(tn), pl.Element(kd)),
                               lambda j, i: (pl.multiple_of(row0 + j * tn, SUBLANES), 0),
                               pipeline_mode=pl.Buffered(1))],
        out_specs=pl.BlockSpec((tm, tn), lambda j, i: (i, j)),
        scratch_shapes=[pltpu.VMEM((kd, tn), _BF16)],
        compiler_params=_params(("parallel", "arbitrary")),
        name=name,
    )(a, w_t)


def _two_source_specs(tm, d, n_first):
    return (pl.BlockSpec((tm, d), lambda i: (jnp.minimum(i, n_first - 1), 0)),
            pl.BlockSpec((tm, d), lambda i: (jnp.maximum(i - n_first, 0), 0)))


def _rmsnorm_kernel(xa_ref, xb_ref, g_ref, o_ref, *, n_first):
    def body(x_ref):
        x = x_ref[...]
        ms = jnp.mean(x * x, axis=-1, keepdims=True)
        o_ref[...] = (x * lax.rsqrt(ms + NORM_EPS) * g_ref[...]).astype(o_ref.dtype)

    i = pl.program_id(0)
    pl.when(i < n_first)(lambda: body(xa_ref))
    pl.when(i >= n_first)(lambda: body(xb_ref))


def _row_tile(m_a, m_b):
    return _tile(math.gcd(m_a, m_b), 256, SUBLANES)


def _rmsnorm_cast(xa, xb, g):
    (m_a, d), m_b = xa.shape, xb.shape[0]
    tm = _row_tile(m_a, m_b)
    n_first = m_a // tm
    return pl.pallas_call(
        functools.partial(_rmsnorm_kernel, n_first=n_first),
        out_shape=jax.ShapeDtypeStruct((m_a + m_b, d), _BF16),
        grid=((m_a + m_b) // tm,),
        in_specs=[*_two_source_specs(tm, d, n_first), pl.BlockSpec((1, d), lambda i: (0, 0))],
        out_specs=pl.BlockSpec((tm, d), lambda i: (i, 0)),
        compiler_params=_params(("parallel",)),
        name="rmsnorm_cast",
    )(xa, xb, g.reshape(1, d))


def _mm_kernel(a_ref, b_ref, o_ref, acc_ref, *, nk):
    k = pl.program_id(2)

    @pl.when(k == 0)
    def _():
        acc_ref[...] = jnp.zeros_like(acc_ref)

    acc_ref[...] += _dot(a_ref[...], b_ref[...])

    @pl.when(k == nk - 1)
    def _():
        o_ref[...] = acc_ref[...].astype(o_ref.dtype)


def _mm1_kernel(a_ref, b_ref, o_ref):
    o_ref[...] = _dot(a_ref[...], b_ref[...]).astype(o_ref.dtype)


def _matmul(a, b, out_dtype, *, tm=512, tn=1024, tk=None, name="matmul"):
    m, kd = a.shape
    _, n = b.shape
    tm = _tile(m, tm, SUBLANES)
    tn = _tile(n, tn)
    tk = kd if tk is None else _tile(kd, tk)
    nk = kd // tk
    if nk == 1:
        return pl.pallas_call(
            _mm1_kernel,
            out_shape=jax.ShapeDtypeStruct((m, n), out_dtype),
            grid=(n // tn, m // tm),
            in_specs=[pl.BlockSpec((tm, kd), lambda j, i: (i, 0)),
                      pl.BlockSpec((kd, tn), lambda j, i: (0, j))],
            out_specs=pl.BlockSpec((tm, tn), lambda j, i: (i, j)),
            compiler_params=_params(("parallel", "parallel")),
            name=name,
        )(a, b)
    return pl.pallas_call(
        functools.partial(_mm_kernel, nk=nk),
        out_shape=jax.ShapeDtypeStruct((m, n), out_dtype),
        grid=(n // tn, m // tm, nk),
        in_specs=[pl.BlockSpec((tm, tk), lambda j, i, k: (i, k)),
                  pl.BlockSpec((tk, tn), lambda j, i, k: (k, j))],
        out_specs=pl.BlockSpec((tm, tn), lambda j, i, k: (i, j)),
        scratch_shapes=[pltpu.VMEM((tm, tn), _F32)],
        compiler_params=_params(("parallel", "parallel", "arbitrary")),
        name=name,
    )(a, b)


def _mlstm_kernel(*refs, L, has_state, has_alias):
    q_ref, k_ref, v_ref, o_ref, g_ref, gb_ref, gn_ref = refs[:7]
    if has_state:
        c0_ref, s0_ref = refs[7:9]
    h_ref, c_out_ref, s_out_ref, c_scr, n_scr, m_scr = refs[7 + 2 * has_state + has_alias:]
    head = pl.program_id(1)
    c = pl.program_id(2)
    nc = pl.num_programs(2)

    @pl.when(c == 0)
    def _():
        if has_state:
            c_scr[...] = c0_ref[0, 0]
            n_scr[...] = s0_ref[0, 0:1, :]
            m_scr[...] = s0_ref[0, 1:2, :]
        else:
            c_scr[...] = jnp.zeros_like(c_scr)
            n_scr[...] = jnp.zeros_like(n_scr)
            m_scr[...] = jnp.zeros_like(m_scr)

    gates = g_ref[...] + gb_ref[...]
    lane = lax.broadcasted_iota(jnp.int32, gates.shape, 1)
    ig_col = jnp.sum(jnp.where(lane == SM_MI + head, gates, 0.0), axis=1, keepdims=True)
    mf_col = jnp.sum(jnp.where(lane == SM_MF + head, gates, 0.0), axis=1, keepdims=True)
    lf_col = jnp.minimum(mf_col, 0.0) - jnp.log1p(jnp.exp(-jnp.abs(mf_col)))

    ri = lax.broadcasted_iota(jnp.int32, (L, L), 0)
    ci = lax.broadcasted_iota(jnp.int32, (L, L), 1)
    eye = ri == ci
    tril = ci <= ri
    lf_row = jnp.sum(jnp.where(eye, lf_col, 0.0), axis=0, keepdims=True)
    ig_row = jnp.sum(jnp.where(eye, ig_col, 0.0), axis=0, keepdims=True)
    b_col = jnp.sum(jnp.where(tril, lf_row, 0.0), axis=1, keepdims=True)
    b_row = jnp.sum(jnp.where(ri <= ci, lf_col, 0.0), axis=0, keepdims=True)
    logw = jnp.where(tril, b_col - b_row + ig_row, -jnp.inf)

    m_prev = m_scr[:, 0:1]
    inter = b_col + m_prev
    m_t = jnp.maximum(inter, jnp.max(logw, axis=1, keepdims=True))
    a = jnp.exp(inter - m_t)
    sw = jnp.exp(logw - m_t)

    q = q_ref[...]
    k = k_ref[...] * (M_QK ** -0.5)
    v = v_ref[...]
    qb = q.astype(_BF16)
    kb = k.astype(_BF16)
    s = _dot_nt(qb, kb) * sw
    c_old = c_scr[...]
    n_old = n_scr[...]
    num = a * _dot_nt(qb, c_old.astype(_BF16)) + _dot(s.astype(_BF16), v.astype(_BF16))
    den = a * jnp.sum(q * n_old, axis=1, keepdims=True) + jnp.sum(s, axis=1, keepdims=True)
    h = num / jnp.maximum(jnp.abs(den), jnp.exp(-m_t))

    m_new = m_t[L - 1:L, :]
    b_last = b_col[L - 1:L, :]
    g_col = jnp.exp(b_last - b_col + ig_col - m_new)
    decay = jnp.exp(b_last + m_prev - m_new)
    c_new = decay * c_old + _dot_tn((v * g_col).astype(_BF16), kb)
    n_new = decay * n_old + jnp.sum(g_col * k, axis=0, keepdims=True)
    c_scr[...] = c_new
    n_scr[...] = n_new
    m_scr[...] = jnp.broadcast_to(m_new, m_scr.shape)

    ms = jnp.mean(h * h, axis=1, keepdims=True)
    y = h * lax.rsqrt(ms + NORM_EPS) * gn_ref[0]
    h_ref[...] = (y * jax.nn.sigmoid(o_ref[...])).astype(h_ref.dtype)

    @pl.when(c == nc - 1)
    def _():
        c_out_ref[0, 0] = c_new
        s_out_ref[0] = jnp.zeros(s_out_ref.shape[1:], _F32)
        s_out_ref[0, 0:1, :] = n_new
        s_out_ref[0, 1:2, :] = jnp.broadcast_to(m_new, (1, M_QK))


def _mlstm(proj, small, gate_bias, g_mhnorm, row0, nseq, T, L, state, h_prev=None):
    nc = T // L
    rb0 = row0 // L
    has_state = state is not None
    has_alias = h_prev is not None

    def rows(b, h, c):
        return rb0 + b * nc + c

    in_specs = [
        pl.BlockSpec((L, M_QK), lambda b, h, c: (rows(b, h, c), A_MQ // M_QK + h)),
        pl.BlockSpec((L, M_QK), lambda b, h, c: (rows(b, h, c), A_MK // M_QK + h)),
        pl.BlockSpec((L, M_V), lambda b, h, c: (rows(b, h, c), A_MV // M_V + h)),
        pl.BlockSpec((L, M_V), lambda b, h, c: (rows(b, h, c), A_MO // M_V + h)),
        pl.BlockSpec((L, LANES), lambda b, h, c: (rows(b, h, c), 0)),
        pl.BlockSpec((1, LANES), lambda b, h, c: (0, 0)),
        pl.BlockSpec((1, 1, M_V), lambda b, h, c: (h, 0, 0)),
    ]
    args = [proj, proj, proj, proj, small, gate_bias, g_mhnorm.reshape(M_HEADS, 1, M_V)]
    if has_state:
        c0, s0 = state
        in_specs += [pl.BlockSpec((1, 1, M_V, M_QK), lambda b, h, c: (b, h, 0, 0)),
                     pl.BlockSpec((1, SUBLANES, M_QK), lambda b, h, c: (b * M_HEADS + h, 0, 0))]
        args += [c0, s0]
    aliases = {}
    if has_alias:
        aliases = {len(args): 0}
        in_specs.append(pl.BlockSpec(memory_space=pl.ANY))
        args.append(h_prev)
    out_shape = (jax.ShapeDtypeStruct((proj.shape[0], M_HEADS * M_V), _BF16),
                 jax.ShapeDtypeStruct((nseq, M_HEADS, M_V, M_QK), _F32),
                 jax.ShapeDtypeStruct((nseq * M_HEADS, SUBLANES, M_QK), _F32))
    out_specs = (pl.BlockSpec((L, M_V), lambda b, h, c: (rows(b, h, c), h)),
                 pl.BlockSpec((1, 1, M_V, M_QK), lambda b, h, c: (b, h, 0, 0)),
                 pl.BlockSpec((1, SUBLANES, M_QK), lambda b, h, c: (b * M_HEADS + h, 0, 0)))
    hm, c_new, stats = pl.pallas_call(
        functools.partial(_mlstm_kernel, L=L, has_state=has_state, has_alias=has_alias),
        out_shape=out_shape,
        grid=(nseq, M_HEADS, nc),
        in_specs=in_specs,
        out_specs=out_specs,
        scratch_shapes=[pltpu.VMEM((M_V, M_QK), _F32), pltpu.VMEM((1, M_QK), _F32),
                        pltpu.VMEM((1, M_QK), _F32)],
        input_output_aliases=aliases,
        compiler_params=_params(("parallel", "parallel", "arbitrary")),
        name="mlstm",
    )(*args)
    n_new = stats[:, 0, :].reshape(nseq, M_HEADS, M_QK)
    m_new = stats[:, 1, 0].reshape(nseq, M_HEADS)
    return hm, c_new, n_new, m_new


def _rope(x, cos, sin_lo, sin_hi, half):
    n = x.shape[-1]
    return (x * cos + pltpu.roll(x, n - half, 1) * sin_lo + pltpu.roll(x, half, 1) * sin_hi)


def _hi_lo(x):
    hi = x.astype(_BF16).astype(_F32)
    return hi, x - hi


def _rope_kernel(aq_ref, ak_ref, av_ref, iq_ref, sm_ref, ta_ref, ti_ref,
                 q_out, k_out, kb_out, vb_out, iqx_out, ik_out, ikx_out):
    ca, sa_lo, sa_hi = ta_ref[0], ta_ref[1], ta_ref[2]
    ci, si_lo, si_hi = ti_ref[0], ti_ref[1], ti_ref[2]
    for h in range(A_HEADS):
        sl = slice(h * A_HD, (h + 1) * A_HD)
        q_out[:, sl] = _rope(aq_ref[:, sl], ca, sa_lo, sa_hi, A_ROT // 2).astype(q_out.dtype)
    for h in range(A_KV_HEADS):
        sl = slice(h * A_HD, (h + 1) * A_HD)
        kr = _rope(ak_ref[:, sl], ca, sa_lo, sa_hi, A_ROT // 2)
        k_out[:, sl] = kr
        kb_out[:, sl] = kr.astype(kb_out.dtype)
    vb_out[...] = av_ref[...].astype(vb_out.dtype)
    low = lax.broadcasted_iota(jnp.int32, (1, LANES), 1) < IDX_DIM
    for p in range(IDX_HEADS * IDX_DIM // LANES):
        x = _rope(iq_ref[:, p * LANES:(p + 1) * LANES], ci, si_lo, si_hi, IDX_ROT // 2)
        hi, lo = _hi_lo(x)
        hi_sw = pltpu.roll(hi, IDX_DIM, 1)
        lo_sw = pltpu.roll(lo, IDX_DIM, 1)
        c0 = 2 * p * IDX_XW
        iqx_out[:, c0:c0 + LANES] = jnp.where(low, hi, hi_sw).astype(iqx_out.dtype)
        iqx_out[:, c0 + LANES:c0 + 2 * LANES] = jnp.where(low, lo, 0.0).astype(iqx_out.dtype)
        iqx_out[:, c0 + 2 * LANES:c0 + 3 * LANES] = jnp.where(low, hi_sw, hi).astype(iqx_out.dtype)
        iqx_out[:, c0 + 3 * LANES:c0 + 4 * LANES] = jnp.where(low, lo_sw, 0.0).astype(iqx_out.dtype)
    ik = _rope(sm_ref[...], ci, si_lo, si_hi, IDX_ROT // 2)
    ik_out[...] = ik[:, SM_IK:SM_IK + IDX_DIM]
    hi, lo = _hi_lo(ik)
    ikx_out[:, 0:LANES] = jnp.where(low, hi, pltpu.roll(lo, IDX_DIM, 1)).astype(ikx_out.dtype)
    ikx_out[:, LANES:2 * LANES] = jnp.where(low, hi, 0.0).astype(ikx_out.dtype)


def _rope_tables(pos, rot, width, reps_valid):
    half = rot // 2
    inv_freq = jnp.exp(jnp.arange(half, dtype=_F32) * (-2.0 * math.log(ROPE_THETA) / rot))
    ang = pos.astype(_F32)[:, None] * inv_freq[None, :]
    cos, sin = jnp.cos(ang), jnp.sin(ang)
    m = pos.shape[0]
    one = jnp.ones((m, width - rot), _F32)
    zero = jnp.zeros((m, width - rot), _F32)
    zh = jnp.zeros((m, half), _F32)
    c_head = jnp.concatenate([cos, cos, one], axis=1)
    lo_head = jnp.concatenate([-sin, zh, zero], axis=1)
    hi_head = jnp.concatenate([zh, sin, zero], axis=1)
    reps = LANES // width
    ident = (jnp.ones((m, width), _F32), jnp.zeros((m, width), _F32), jnp.zeros((m, width), _F32))
    out = []
    for t, idt in zip((c_head, lo_head, hi_head), ident):
        out.append(jnp.concatenate([t if r < reps_valid else idt for r in range(reps)], axis=1))
    return jnp.stack(out)


def _rope_all(proj, small, pos):
    m = proj.shape[0]
    tm = _tile(m, 256, SUBLANES)
    ta = _rope_tables(pos, A_ROT, A_HD, 1)
    ti = _rope_tables(pos, IDX_ROT, IDX_DIM, LANES // IDX_DIM)
    wq, wk, wi = A_HEADS * A_HD, A_KV_HEADS * A_HD, IDX_HEADS * IDX_DIM
    wix = IDX_HEADS * IDX_XW
    out_shape = (jax.ShapeDtypeStruct((m, wq), _BF16),
                 jax.ShapeDtypeStruct((m, wk), _F32),
                 jax.ShapeDtypeStruct((m, wk), _BF16),
                 jax.ShapeDtypeStruct((m, wk), _BF16),
                 jax.ShapeDtypeStruct((m, wix), _BF16),
                 jax.ShapeDtypeStruct((m, IDX_DIM), _F32),
                 jax.ShapeDtypeStruct((m, IDX_XW), _BF16))
    return pl.pallas_call(
        _rope_kernel,
        out_shape=out_shape,
        grid=(m // tm,),
        in_specs=[pl.BlockSpec((tm, wq), lambda i: (i, B_AQ // wq)),
                  pl.BlockSpec((tm, wk), lambda i: (i, B_AK // wk)),
                  pl.BlockSpec((tm, wk), lambda i: (i, B_AV // wk)),
                  pl.BlockSpec((tm, wi), lambda i: (i, B_IQ // wi)),
                  pl.BlockSpec((tm, LANES), lambda i: (i, 0)),
                  pl.BlockSpec((3, tm, LANES), lambda i: (0, i, 0)),
                  pl.BlockSpec((3, tm, LANES), lambda i: (0, i, 0))],
        out_specs=(pl.BlockSpec((tm, wq), lambda i: (i, 0)),
                   pl.BlockSpec((tm, wk), lambda i: (i, 0)),
                   pl.BlockSpec((tm, wk), lambda i: (i, 0)),
                   pl.BlockSpec((tm, wk), lambda i: (i, 0)),
                   pl.BlockSpec((tm, wix), lambda i: (i, 0)),
                   pl.BlockSpec((tm, IDX_DIM), lambda i: (i, 0)),
                   pl.BlockSpec((tm, IDX_XW), lambda i: (i, 0))),
        compiler_params=_params(("parallel",)),
        name="rope",
    )(proj, proj, proj, proj, small, ta, ti)


def _attn_kernel(*refs, TQ, TK, past_len, l_valid, topk):
    q_ref, iqx_ref, sm_ref, k_ref, v_ref, ikx_ref = refs[:6]
    o_ref, key_scr, bias_scr, w_scr = refs[-4:]
    i = pl.program_id(1)
    q_start = past_len + i * TQ
    qpos = q_start + lax.broadcasted_iota(jnp.int32, (TQ, 1), 0)
    qchunk = qpos // CHUNK
    last_vis = jnp.minimum(((q_start + TQ - 1) // CHUNK + 1) * CHUNK, l_valid)
    nkt = (last_vis + TK - 1) // TK
    lane_pos = lax.broadcasted_iota(jnp.int32, (1, TK), 1)

    w_scale = (IDX_DIM ** -0.5) * (IDX_HEADS ** -0.5)
    for h in range(IDX_HEADS):
        w_scr[h] = jnp.broadcast_to(sm_ref[:, SM_IW + h:SM_IW + h + 1] * w_scale, (TQ, LANES))
    lane128 = lax.broadcasted_iota(jnp.int32, (1, LANES), 1)
    sc = min(TK, 2 * LANES)

    def score_tile(kt, carry):
        for cc in range(TK // sc):
            k0 = pl.multiple_of(kt * TK + cc * sc, sc)
            ik_c = ikx_ref[pl.ds(k0, sc), :]
            accs = [jnp.zeros((TQ, LANES), _F32) for _ in range(sc // LANES)]
            for h in range(IDX_HEADS):
                isc = _dot_nt(iqx_ref[:, h * IDX_XW:(h + 1) * IDX_XW], ik_c)
                w_h = w_scr[h]
                for c in range(sc // LANES):
                    accs[c] = accs[c] + jnp.maximum(isc[:, c * LANES:(c + 1) * LANES], 0.0) * w_h
            for c in range(sc // LANES):
                bits = pltpu.bitcast(accs[c] + 0.0, jnp.int32)
                key = jnp.where(bits < 0, bits ^ 0x7FFFFFFF, bits)
                kpos = k0 + c * LANES + lane128
                vis = ((kpos // CHUNK) <= qchunk) & (kpos < l_valid)
                col = cc * sc + c * LANES
                key_scr[kt, :, col:col + LANES] = jnp.where(vis, key, INT_MIN)
        return carry

    lax.fori_loop(0, nkt, score_tile, 0)

    def lane_fold(x):
        out = x[:, 0:LANES]
        for c in range(1, TK // LANES):
            out = out + x[:, c * LANES:(c + 1) * LANES]
        return out

    def count(pred_fn):
        def body(kt, part):
            return part + lane_fold(pred_fn(key_scr[kt], kt).astype(jnp.int32))
        part = lax.fori_loop(0, nkt, body, jnp.zeros((TQ, LANES), jnp.int32))
        return jnp.sum(part, axis=1, keepdims=True)

    def bit_step(it, thr_u):
        cand_u = thr_u | lax.shift_left(jnp.int32(1), 31 - it)
        cand_s = cand_u ^ INT_MIN
        cnt = count(lambda key, kt: key >= cand_s)
        return jnp.where(cnt >= topk, cand_u, thr_u)

    thr = lax.fori_loop(0, 32, bit_step, jnp.zeros((TQ, 1), jnp.int32)) ^ INT_MIN
    n_ge = count(lambda key, kt: key >= thr)

    def tie_search(_):
        need = topk - count(lambda key, kt: key > thr)
        nbits = (key_scr.shape[0] * TK).bit_length()

        def pos_step(it, p):
            cand = p | lax.shift_left(jnp.int32(1), nbits - 1 - it)
            cnt = count(lambda key, kt: (key == thr) & ((kt * TK + lane_pos) < cand))
            return jnp.where(cnt < need, cand, p)

        return lax.fori_loop(0, nbits, pos_step, jnp.zeros((TQ, 1), jnp.int32))

    has_tie = jnp.max(n_ge.astype(_F32)) > topk
    p_last = lax.cond(has_tie, tie_search, lambda _: jnp.full((TQ, 1), 2 ** 31 - 1, jnp.int32), 0)

    def bias_tile(kt, carry):
        key = key_scr[kt]
        sel = ((key > thr) | ((key == thr) & ((kt * TK + lane_pos) <= p_last))) & (key > KEY_NEG_INF)
        bias_scr[kt] = jnp.where(sel, 0.0, NEG_BIG)
        return carry

    lax.fori_loop(0, nkt, bias_tile, 0)

    scale = (A_HD ** -0.5) * math.log2(math.e)
    rows = A_GROUP * TQ
    n_par = 1

    def q_group(g):
        return jnp.concatenate(
            [q_ref[:, (g * A_GROUP + j) * A_HD:(g * A_GROUP + j + 1) * A_HD] for j in range(A_GROUP)], axis=0)

    for g0 in range(0, A_KV_HEADS, n_par):
        qgs = [q_group(g0 + u) for u in range(n_par)]

        def attn_tile(kt, carry, g0=g0, qgs=qgs):
            k0 = pl.multiple_of(kt * TK, TK)
            bias = bias_scr[kt][None]
            s_u = [_dot_nt(qgs[u], k_ref[pl.ds(k0, TK), (g0 + u) * A_HD:(g0 + u + 1) * A_HD])
                   .reshape(A_GROUP, TQ, TK) * scale + bias for u in range(n_par)]
            new = []
            p_u = []
            for u in range(n_par):
                m_i, l_i, acc = carry[u]
                m_new = jnp.maximum(m_i, jnp.max(s_u[u], axis=-1, keepdims=True))
                alpha = jnp.exp2(m_i - m_new)
                p = jnp.exp2(s_u[u] - m_new)
                p_u.append(p.reshape(rows, TK).astype(_BF16))
                new.append((m_new, alpha * l_i + jnp.sum(p, axis=-1, keepdims=True), alpha * acc))
            out = []
            for u in range(n_par):
                v_t = v_ref[pl.ds(k0, TK), (g0 + u) * A_HD:(g0 + u + 1) * A_HD]
                m_new, l_new, acc = new[u]
                out.append((m_new, l_new, acc + _dot(p_u[u], v_t).reshape(A_GROUP, TQ, A_HD)))
            return tuple(out)

        init = tuple((jnp.full((A_GROUP, TQ, 1), NEG_BIG, _F32), jnp.zeros((A_GROUP, TQ, 1), _F32),
                      jnp.zeros((A_GROUP, TQ, A_HD), _F32)) for _ in range(n_par))
        fin = lax.fori_loop(0, nkt, attn_tile, init)
        for u in range(n_par):
            _, l_f, acc_f = fin[u]
            out = acc_f / l_f
            for j in range(A_GROUP):
                hh = (g0 + u) * A_GROUP + j
                o_ref[:, hh * A_HD:(hh + 1) * A_HD] = out[j].astype(o_ref.dtype)


def _attention(q_rot, iq_rot, small, k_all, v_all, ik_all, row0, nseq, T, TQ, TK, lp, past_len, l_valid, o_prev=None):
    nq = T // TQ
    qb0 = row0 // TQ
    topk = min(TOPK_MAX, l_valid // 4)
    wq, wk, wix = A_HEADS * A_HD, A_KV_HEADS * A_HD, IDX_HEADS * IDX_XW
    in_specs = [pl.BlockSpec((TQ, wq), lambda b, i: (qb0 + b * nq + i, 0)),
                pl.BlockSpec((TQ, wix), lambda b, i: (qb0 + b * nq + i, 0)),
                pl.BlockSpec((TQ, LANES), lambda b, i: (qb0 + b * nq + i, 0)),
                pl.BlockSpec((lp, wk), lambda b, i: (b, 0)),
                pl.BlockSpec((lp, wk), lambda b, i: (b, 0)),
                pl.BlockSpec((lp, IDX_XW), lambda b, i: (b, 0))]
    args = [q_rot, iq_rot, small, k_all, v_all, ik_all]
    aliases = {}
    if o_prev is not None:
        aliases = {len(args): 0}
        in_specs.append(pl.BlockSpec(memory_space=pl.ANY))
        args.append(o_prev)
    return pl.pallas_call(
        functools.partial(_attn_kernel, TQ=TQ, TK=TK, past_len=past_len, l_valid=l_valid, topk=topk),
        out_shape=jax.ShapeDtypeStruct((q_rot.shape[0], wq), _BF16),
        grid=(nseq, nq),
        in_specs=in_specs,
        out_specs=pl.BlockSpec((TQ, wq), lambda b, i: (qb0 + b * nq + i, 0)),
        scratch_shapes=[pltpu.VMEM((lp // TK, TQ, TK), jnp.int32),
                        pltpu.VMEM((lp // TK, TQ, TK), _F32),
                        pltpu.VMEM((IDX_HEADS, TQ, LANES), _F32)],
        input_output_aliases=aliases,
        compiler_params=_params(("parallel", "arbitrary")),
        name="sparse_attention",
    )(*args)


def _merge_kernel(hm_ref, ha_ref, wm_ref, wa_ref, gm_ref, ga_ref, o_ref):
    pm = _dot(hm_ref[...], wm_ref[...])
    pa = _dot(ha_ref[...], wa_ref[...])
    o_ref[...] = (jax.nn.sigmoid(gm_ref[...]) * pm + jax.nn.sigmoid(ga_ref[...]) * pa).astype(o_ref.dtype)


def _merge(hm, ha, wm, wa, proj, d):
    m = hm.shape[0]
    tm = _tile(m, 512, SUBLANES)
    tn = _tile(d, 1024)
    gm0 = 0
    ga0 = d // tn
    km, ka = hm.shape[1], ha.shape[1]
    return pl.pallas_call(
        _merge_kernel,
        out_shape=jax.ShapeDtypeStruct((m, d), _BF16),
        grid=(d // tn, m // tm),
        in_specs=[pl.BlockSpec((tm, km), lambda j, i: (i, 0)),
                  pl.BlockSpec((tm, ka), lambda j, i: (i, 0)),
                  pl.BlockSpec((km, tn), lambda j, i: (0, j)),
                  pl.BlockSpec((ka, tn), lambda j, i: (0, j)),
                  pl.BlockSpec((tm, tn), lambda j, i: (i, gm0 + j)),
                  pl.BlockSpec((tm, tn), lambda j, i: (i, ga0 + j))],
        out_specs=pl.BlockSpec((tm, tn), lambda j, i: (i, j)),
        compiler_params=_params(("parallel", "parallel")),
        name="gated_merge",
    )(hm, ha, wm, wa, proj, proj)


def _resnorm_next_kernel(xa_ref, xb_ref, y_ref, g_ref, g2_ref, x1_ref, h_ref, *, n_first):
    def body(x_ref):
        y = y_ref[...]
        ms = jnp.mean(y * y, axis=-1, keepdims=True)
        x1 = x_ref[...] + y * lax.rsqrt(ms + NORM_EPS) * g_ref[...]
        x1_ref[...] = x1
        ms1 = jnp.mean(x1 * x1, axis=-1, keepdims=True)
        h_ref[...] = (x1 * lax.rsqrt(ms1 + NORM_EPS) * g2_ref[...]).astype(h_ref.dtype)

    i = pl.program_id(0)
    pl.when(i < n_first)(lambda: body(xa_ref))
    pl.when(i >= n_first)(lambda: body(xb_ref))


def _resnorm_next(xa, xb, y, g, g_next):
    m, d = y.shape
    tm = _row_tile(xa.shape[0], xb.shape[0])
    n_first = xa.shape[0] // tm
    row = pl.BlockSpec((tm, d), lambda i: (i, 0))
    vec = pl.BlockSpec((1, d), lambda i: (0, 0))
    return pl.pallas_call(
        functools.partial(_resnorm_next_kernel, n_first=n_first),
        out_shape=(jax.ShapeDtypeStruct((m, d), _F32), jax.ShapeDtypeStruct((m, d), _BF16)),
        grid=(m // tm,), in_specs=[*_two_source_specs(tm, d, n_first), row, vec, vec], out_specs=(row, row),
        compiler_params=_params(("parallel",)), name="resnorm_next",
    )(xa, xb, y, g.reshape(1, d), g_next.reshape(1, d))


def _resnorm_kernel(x_ref, y_ref, g_ref, oa_ref, ob_ref, *, n_first):
    y = y_ref[...]
    ms = jnp.mean(y * y, axis=-1, keepdims=True)
    out = x_ref[...] + y * lax.rsqrt(ms + NORM_EPS) * g_ref[...]
    i = pl.program_id(0)

    @pl.when(i < n_first)
    def _():
        oa_ref[...] = out

    @pl.when(i >= n_first)
    def _():
        ob_ref[...] = out


def _resnorm_split(x, y, g, m_a):
    m, d = x.shape
    m_b = m - m_a
    tm = _row_tile(m_a, m_b)
    n_first = m_a // tm
    row = pl.BlockSpec((tm, d), lambda i: (i, 0))
    vec = pl.BlockSpec((1, d), lambda i: (0, 0))
    return pl.pallas_call(
        functools.partial(_resnorm_kernel, n_first=n_first),
        out_shape=(jax.ShapeDtypeStruct((m_a, d), _F32), jax.ShapeDtypeStruct((m_b, d), _F32)),
        grid=(m // tm,), in_specs=[row, row, vec], out_specs=_two_source_specs(tm, d, n_first),
        compiler_params=_params(("arbitrary",)), name="resnorm",
    )(x, y, g.reshape(1, d))


HALO = SUBLANES


def _conv_gelu_gate(cw_ref, cb_ref, prev2, prev1, gate, lin):
    gc = cw_ref[0:1, :] * prev2 + cw_ref[1:2, :] * prev1 + cw_ref[2:3, :] * gate + cb_ref[...]
    c = -2.0 * math.sqrt(2.0 / math.pi)
    act = gc / (1.0 + jnp.exp(gc * (c + (c * 0.044715) * (gc * gc))))
    return act * lin


def _ffn_up_first_kernel(h_ref, wg_ref, wl_ref, cw_ref, cb_ref, z_ref, tail_ref, wg_bf, wl_bf, g_scr,
                         *, tm, seq_len):
    i = pl.program_id(1)
    tn = z_ref.shape[1]

    @pl.when(i == 0)
    def _():
        wg_bf[...] = wg_ref[...].astype(wg_bf.dtype)
        wl_bf[...] = wl_ref[...].astype(wl_bf.dtype)

    @pl.when((i * tm) % seq_len == 0)
    def _():
        g_scr[0:HALO, :] = jnp.zeros((HALO, tn), _F32)

    h = h_ref[...]
    gate = _dot(h, wg_bf[...])
    lin = _dot(h, wl_bf[...])
    g_scr[HALO:HALO + tm, :] = gate
    zed = _conv_gelu_gate(cw_ref, cb_ref, g_scr[HALO - 2:HALO - 2 + tm, :], g_scr[HALO - 1:HALO - 1 + tm, :],
                          gate, lin)
    z_ref[...] = zed.astype(z_ref.dtype)
    g_scr[0:HALO, :] = gate[tm - HALO:tm, :]
    tail_ref[...] = gate[tm - SUBLANES:tm, :]


def _ffn_up_state_kernel(h_ref, wg_ref, wl_ref, cw_ref, cb_ref, init_ref, z_any, z_ref, tail_ref, g_scr,
                         *, tm, seq_len):
    tn = z_ref.shape[1]
    h = h_ref[...]
    gate = _dot(h, wg_ref[...].astype(_BF16))
    lin = _dot(h, wl_ref[...].astype(_BF16))
    g_scr[0:HALO, :] = jnp.zeros((HALO, tn), _F32)
    g_scr[HALO:HALO + tm, :] = gate
    prev1 = g_scr[HALO - 1:HALO - 1 + tm, :]
    prev2 = g_scr[HALO - 2:HALO - 2 + tm, :]
    nseg = tm // seq_len
    init0 = jnp.concatenate([jnp.broadcast_to(init_ref[s, 0:1, :], (seq_len, tn)) for s in range(nseg)], axis=0)
    init1 = jnp.concatenate([jnp.broadcast_to(init_ref[s, 1:2, :], (seq_len, tn)) for s in range(nseg)], axis=0)
    t = lax.broadcasted_iota(jnp.int32, (tm, 1), 0) % seq_len
    prev1 = jnp.where(t == 0, init1, prev1)
    prev2 = jnp.where(t == 0, init0, jnp.where(t == 1, init1, prev2))
    z_ref[...] = _conv_gelu_gate(cw_ref, cb_ref, prev2, prev1, gate, lin).astype(z_ref.dtype)
    for s in range(nseg):
        tail_ref[s * SUBLANES:(s + 1) * SUBLANES, :] = gate[(s + 1) * seq_len - SUBLANES:(s + 1) * seq_len, :]


def _conv_state(tails, nseq, segs_per_seq, dff):
    tails = tails.reshape(nseq * segs_per_seq, SUBLANES, dff)
    last = tails[segs_per_seq - 1::segs_per_seq, SUBLANES - (CONV_W - 1):, :]
    return last.reshape(nseq, CONV_W - 1, dff)


def _ffn_up_specs(d, tn, nj, h_spec):
    return [h_spec,
            pl.BlockSpec((d, tn), lambda j, i: (0, j)),
            pl.BlockSpec((d, tn), lambda j, i: (0, nj + j)),
            pl.BlockSpec((CONV_W, tn), lambda j, i: (0, j)),
            pl.BlockSpec((1, tn), lambda j, i: (0, j))]


def _ffn_up_first(hf, w_up, conv_w, conv_b, nrows, seq_len):
    m_all, d = hf.shape
    dff = w_up.shape[1] // 2
    tm = _tile(seq_len, 512, SUBLANES)
    tn = _tile(dff, 256)
    nj = dff // tn
    nt = nrows // tm
    specs = _ffn_up_specs(d, tn, nj, pl.BlockSpec((tm, d), lambda j, i: (i, 0)))
    z, tails = pl.pallas_call(
        functools.partial(_ffn_up_first_kernel, tm=tm, seq_len=seq_len),
        out_shape=(jax.ShapeDtypeStruct((m_all, dff), _BF16),
                   jax.ShapeDtypeStruct((nt * SUBLANES, dff), _F32)),
        grid=(nj, nt),
        in_specs=specs,
        out_specs=(pl.BlockSpec((tm, tn), lambda j, i: (i, j)),
                   pl.BlockSpec((SUBLANES, tn), lambda j, i: (i, j))),
        scratch_shapes=[pltpu.VMEM((d, tn), _BF16), pltpu.VMEM((d, tn), _BF16),
                        pltpu.VMEM((HALO + tm, tn), _F32)],
        compiler_params=_params(("parallel", "arbitrary")),
        name="ffn_up_conv_first",
    )(hf, w_up, w_up, conv_w, conv_b.reshape(1, dff))
    return z, _conv_state(tails, nrows // seq_len, seq_len // tm, dff)


def _ffn_up_state(hf, w_up, conv_w, conv_b, row0, nrows, seq_len, init, z_prev):
    m_all, d = hf.shape
    dff = w_up.shape[1] // 2
    tm = _tile(nrows, 512, seq_len)
    tn = _tile(dff, 256)
    nj = dff // tn
    nseg = tm // seq_len
    rb0 = row0 // tm
    specs = _ffn_up_specs(d, tn, nj, pl.BlockSpec((tm, d), lambda j, i: (rb0 + i, 0)))
    specs += [pl.BlockSpec((nseg, CONV_W - 1, tn), lambda j, i: (i, 0, j)), pl.BlockSpec(memory_space=pl.ANY)]
    z, tails = pl.pallas_call(
        functools.partial(_ffn_up_state_kernel, tm=tm, seq_len=seq_len),
        out_shape=(jax.ShapeDtypeStruct((m_all, dff), _BF16),
                   jax.ShapeDtypeStruct((nrows // seq_len * SUBLANES, dff), _F32)),
        grid=(nj, nrows // tm),
        in_specs=specs,
        out_specs=(pl.BlockSpec((tm, tn), lambda j, i: (rb0 + i, j)),
                   pl.BlockSpec((nseg * SUBLANES, tn), lambda j, i: (i, j))),
        scratch_shapes=[pltpu.VMEM((HALO + tm, tn), _F32)],
        input_output_aliases={6: 0},
        compiler_params=_params(("parallel", "arbitrary")),
        name="ffn_up_conv_state",
    )(hf, w_up, w_up, conv_w, conv_b.reshape(1, dff), init, z_prev)
    return z, _conv_state(tails, nrows // seq_len, 1, dff)


def _layer(xa, xb, geom, cache, state, w):
    (B, S, Bd, Td, P) = geom
    (cache_k, cache_v, cache_ik) = cache
    (state_C, state_n, state_m, state_conv) = state
    (w_in, b_igate, b_fgate, g_mhnorm, w_proj_m, w_proj_a, w_out,
     g_pre_mix, g_post_mix, g_pre_ffn, g_post_ffn, w_up, conv_w, conv_b, w_down) = w
    mp, d = xa.shape
    dff = w_down.shape[0]
    wk = A_KV_HEADS * A_HD

    c_mi = A_END
    c_aq = c_mi + 2 * M_HEADS
    c_ik = c_aq + B_END
    c_iw = c_ik + IDX_DIM
    c_gm = c_iw + IDX_HEADS
    w_t = jnp.swapaxes(w_in, 0, 1).astype(_F32)
    w_small_t = jnp.concatenate([w_t[c_ik:c_iw], w_t[c_mi:c_aq], w_t[c_iw:c_gm],
                                 jnp.zeros((LANES - SM_END, d), _F32)], axis=0)
    gate_bias = jnp.concatenate([jnp.zeros((SM_MI,), _F32), b_igate.astype(_F32), b_fgate.astype(_F32),
                                 jnp.zeros((LANES - SM_IW,), _F32)]).reshape(1, LANES)

    hn = _rmsnorm_cast(xa, xb, g_pre_mix)
    proj_a = _matmul_wt(hn, w_t, 0, A_END, _F32, name="in_proj_mlstm")
    proj_b = _matmul_wt(hn, w_t, c_aq, B_END, _F32, name="in_proj_attn")
    proj_c = _matmul_wt(hn, w_t, c_gm, 2 * d, _F32, name="in_proj_gates")
    small = _matmul_wt(hn, w_small_t, 0, LANES, _F32, name="in_proj_small")

    lp_chunk = _tile(S, 256, CHUNK)
    hm_p, c_p, n_p, m_p = _mlstm(proj_a, small, gate_bias, g_mhnorm, 0, B, S, lp_chunk, None)
    s0 = jnp.zeros((Bd * M_HEADS, SUBLANES, M_QK), _F32)
    s0 = s0.at[:, 0, :].set(state_n.reshape(Bd * M_HEADS, M_QK).astype(_F32))
    s0 = s0.at[:, 1, :].set(jnp.broadcast_to(state_m.reshape(Bd * M_HEADS, 1).astype(_F32), (Bd * M_HEADS, M_QK)))
    hm, c_s, n_s, m_s = _mlstm(proj_a, small, gate_bias, g_mhnorm, mp, Bd, Td, min(CHUNK, Td),
                               (state_C.astype(_F32), s0), hm_p)

    pos = jnp.concatenate([jnp.tile(jnp.arange(S, dtype=jnp.int32), B),
                           jnp.tile(P + jnp.arange(Td, dtype=jnp.int32), Bd)])
    q_rot, k_rot, k_bf, v_bf, iqx, ik_rot, ikx = _rope_all(proj_b, small, pos)
    tq_p = _tile(S, 256, CHUNK)
    tk_p = _tile(S, 512)
    ha_p = _attention(q_rot, iqx, small, k_bf, v_bf, ikx, 0, B, S, tq_p, tk_p, S, 0, S)
    l_s = P + Td
    tk_s = 256
    lp_s = -(-l_s // tk_s) * tk_s
    pad = lp_s - l_s
    k_s = jnp.concatenate([cache_k.reshape(Bd, P, wk).astype(_BF16), k_bf[mp:].reshape(Bd, Td, wk),
                           jnp.zeros((Bd, pad, wk), _BF16)], axis=1).reshape(Bd * lp_s, wk)
    v_s = jnp.concatenate([cache_v.reshape(Bd, P, wk).astype(_BF16), v_bf[mp:].reshape(Bd, Td, wk),
                           jnp.zeros((Bd, pad, wk), _BF16)], axis=1).reshape(Bd * lp_s, wk)
    cik = cache_ik.astype(_F32)
    cik_hi = cik.astype(_BF16)
    cik_lo = (cik - cik_hi.astype(_F32)).astype(_BF16)
    cikx = jnp.concatenate([cik_hi, cik_lo, cik_hi, jnp.zeros_like(cik_hi)], axis=-1)
    ikx_s = jnp.concatenate([cikx, ikx[mp:].reshape(Bd, Td, IDX_XW),
                             jnp.zeros((Bd, pad, IDX_XW), _BF16)], axis=1).reshape(Bd * lp_s, IDX_XW)
    ha = _attention(q_rot, iqx, small, k_s, v_s, ikx_s, mp, Bd, Td, Td, tk_s, lp_s, P, l_s, ha_p)

    mix = _merge(hm, ha, w_proj_m.astype(_BF16), w_proj_a.astype(_BF16), proj_c, d)
    y1 = _matmul(mix, w_out.astype(_BF16), _F32, tm=512, tn=1024, name="out_proj")
    x1, hf = _resnorm_next(xa, xb, y1, g_post_mix, g_pre_ffn)

    w_up = w_up.astype(_F32)
    z_p, conv_p = _ffn_up_first(hf, w_up, conv_w, conv_b, mp, S)
    z, conv_s = _ffn_up_state(hf, w_up, conv_w, conv_b, mp, Bd * Td, Td, state_conv.astype(_F32), z_p)
    y2 = _matmul(z, w_down.astype(_BF16), _F32, tm=512, tn=1024, tk=dff // 2, name="down_proj")
    x2 = _resnorm_split(x1, y2, g_post_ffn, mp)

    av = proj_b[:, B_AV:B_AV + wk]
    outs_p = (k_rot[:mp].reshape(B, S, A_KV_HEADS, A_HD), av[:mp].reshape(B, S, A_KV_HEADS, A_HD),
              ik_rot[:mp].reshape(B, S, IDX_DIM), c_p, n_p, m_p, conv_p)
    outs_s = (k_rot[mp:].reshape(Bd, Td, A_KV_HEADS, A_HD), av[mp:].reshape(Bd, Td, A_KV_HEADS, A_HD),
              ik_rot[mp:].reshape(Bd, Td, IDX_DIM), c_s, n_s, m_s, conv_s)
    return x2, outs_p, outs_s


def kernel(x_prompt, x_sample, cache_k, cache_v, cache_idx_k, state_C, state_n, state_m, state_conv,
           w_in, b_igate, b_fgate, g_mhnorm, w_proj_m, w_proj_a, w_out,
           g_pre_mix, g_post_mix, g_pre_ffn, g_post_ffn, w_up, conv_w, conv_b, w_down):
    B, S, d = x_prompt.shape
    Bd, Td, _ = x_sample.shape
    P = cache_k.shape[2]
    depth = w_in.shape[0]
    mp = B * S
    xa, xb = x_prompt.reshape(mp, d), x_sample.reshape(Bd * Td, d)
    all_p, all_s = [], []
    for l in range(depth):
        w = (w_in[l], b_igate[l], b_fgate[l], g_mhnorm[l], w_proj_m[l], w_proj_a[l], w_out[l],
             g_pre_mix[l], g_post_mix[l], g_pre_ffn[l], g_post_ffn[l], w_up[l], conv_w[l], conv_b[l], w_down[l])
        (xa, xb), outs_p, outs_s = _layer(xa, xb, (B, S, Bd, Td, P), (cache_k[l], cache_v[l], cache_idx_k[l]),
                                          (state_C[l], state_n[l], state_m[l], state_conv[l]), w)
        all_p.append(outs_p)
        all_s.append(outs_s)

    def stk(outs, i):
        return jnp.stack([o[i] for o in outs])

    yp = xa.reshape(B, S, d)
    ys = xb.reshape(Bd, Td, d)
    return (yp, ys) + tuple(stk(all_p, i) for i in range(7)) + tuple(stk(all_s, i) for i in range(7))
```

```python
import functools
import math

import jax
import jax.numpy as jnp
from jax import lax
from jax.experimental import pallas as pl
from jax.experimental.pallas import tpu as pltpu

CHUNK = 64
NORM_EPS = 1e-6
ROPE_THETA = 500000.0
M_HEADS = 8
M_QK = 128
M_V = 256
A_HEADS = 16
A_KV_HEADS = 4
A_GROUP = A_HEADS // A_KV_HEADS
A_HD = 128
A_ROT = A_HD // 4
IDX_HEADS = 16
IDX_DIM = 64
IDX_ROT = IDX_DIM // 4
TOPK_MAX = 256
CONV_W = 3

LANES = 128
SUBLANES = 8
VMEM_LIMIT_BYTES = 52 * 1024 * 1024

A_MQ = 0
A_MK = A_MQ + M_HEADS * M_QK
A_MV = A_MK + M_HEADS * M_QK
A_MO = A_MV + M_HEADS * M_V
A_END = A_MO + M_HEADS * M_V
B_AQ = 0
B_AK = B_AQ + A_HEADS * A_HD
B_AV = B_AK + A_KV_HEADS * A_HD
B_IQ = B_AV + A_KV_HEADS * A_HD
B_END = B_IQ + IDX_HEADS * IDX_DIM
SM_IK = 0
SM_MI = SM_IK + IDX_DIM
SM_MF = SM_MI + M_HEADS
SM_IW = SM_MF + M_HEADS
SM_END = SM_IW + IDX_HEADS
IDX_XW = 4 * IDX_DIM

INT_MIN = -2 ** 31
KEY_NEG_INF = INT_MIN + 0x7FFFFF
NEG_BIG = -1e30

_BF16 = jnp.bfloat16
_F32 = jnp.float32


def _tile(dim, target, quantum=LANES):
    if dim <= target:
        return dim
    t = (target // quantum) * quantum
    while t >= quantum:
        if dim % t == 0:
            return t
        t -= quantum
    return dim


def _params(sem):
    return pltpu.CompilerParams(dimension_semantics=sem, vmem_limit_bytes=VMEM_LIMIT_BYTES)


def _dot(a, b):
    return jnp.dot(a, b, preferred_element_type=_F32)


def _dot_nt(a, b):
    return lax.dot_general(a, b, (((1,), (1,)), ((), ())), preferred_element_type=_F32)


def _dot_tn(a, b):
    return lax.dot_general(a, b, (((0,), (0,)), ((), ())), preferred_element_type=_F32)


def _mm_wt_kernel(a_ref, w_hbm, o_ref, w_f32, w_bf, sem, *, row0, nj):
    j = pl.program_id(0)
    tn, kd = w_f32.shape

    def tile_copy(jj):
        rows = pl.ds(pl.multiple_of(row0 + jj * tn, SUBLANES), tn)
        return pltpu.make_async_copy(w_hbm.at[rows, :], w_f32, sem)

    @pl.when(pl.program_id(1) == 0)
    def _():
        @pl.when(j == 0)
        def _():
            tile_copy(0).start()

        tile_copy(j).wait()
        sq = math.gcd(tn, kd, 512)
        for r in range(0, tn, sq):
            for c in range(0, kd, sq):
                w_bf[c:c + sq, r:r + sq] = w_f32[r:r + sq, c:c + sq].T.astype(w_bf.dtype)

        @pl.when(j + 1 < nj)
        def _():
            tile_copy(j + 1).start()

    o_ref[...] = _dot(a_ref[...], w_bf[...]).astype(o_ref.dtype)


def _matmul_wt(a, w_t, row0, n, out_dtype, *, tm=512, tn=1024, name="matmul_wt"):
    m, kd = a.shape
    tm = _tile(m, tm, SUBLANES)
    tn = _tile(n, tn)
    nj = n // tn
    return pl.pallas_call(
        functools.partial(_mm_wt_kernel, row0=row0, nj=nj),
        out_shape=jax.ShapeDtypeStruct((m, n), out_dtype),
        grid=(nj, m // tm),
        in_specs=[pl.BlockSpec((tm, kd), lambda j, i: (i, 0)),
                  pl.BlockSpec(memory_space=pl.ANY)],
        out_specs=pl.BlockSpec((tm, tn), lambda j, i: (i, j)),
        scratch_shapes=[pltpu.VMEM((tn, kd), _F32), pltpu.VMEM((kd, tn), _BF16), pltpu.SemaphoreType.DMA],
        compiler_params=_params(("arbitrary", "arbitrary")),
        name=name,
    )(a, w_t)


def _two_source_specs(tm, d, n_first):
    return (pl.BlockSpec((tm, d), lambda i: (jnp.minimum(i, n_first - 1), 0)),
            pl.BlockSpec((tm, d), lambda i: (jnp.maximum(i - n_first, 0), 0)))


def _rmsnorm_kernel(xa_ref, xb_ref, g_ref, o_ref, *, n_first):
    def body(x_ref):
        x = x_ref[...]
        ms = jnp.mean(x * x, axis=-1, keepdims=True)
        o_ref[...] = (x * lax.rsqrt(ms + NORM_EPS) * g_ref[...]).astype(o_ref.dtype)

    i = pl.program_id(0)
    pl.when(i < n_first)(lambda: body(xa_ref))
    pl.when(i >= n_first)(lambda: body(xb_ref))


def _row_tile(m_a, m_b):
    return _tile(math.gcd(m_a, m_b), 256, SUBLANES)


def _rmsnorm_cast(xa, xb, g):
    (m_a, d), m_b = xa.shape, xb.shape[0]
    tm = _row_tile(m_a, m_b)
    n_first = m_a // tm
    return pl.pallas_call(
        functools.partial(_rmsnorm_kernel, n_first=n_first),
        out_shape=jax.ShapeDtypeStruct((m_a + m_b, d), _BF16),
        grid=((m_a + m_b) // tm,),
        in_specs=[*_two_source_specs(tm, d, n_first), pl.BlockSpec((1, d), lambda i: (0, 0))],
        out_specs=pl.BlockSpec((tm, d), lambda i: (i, 0)),
        compiler_params=_params(("parallel",)),
        name="rmsnorm_cast",
    )(xa, xb, g.reshape(1, d))


def _mm_kernel(a_ref, b_ref, o_ref, acc_ref, *, nk):
    k = pl.program_id(2)

    @pl.when(k == 0)
    def _():
        acc_ref[...] = jnp.zeros_like(acc_ref)

    acc_ref[...] += _dot(a_ref[...], b_ref[...])

    @pl.when(k == nk - 1)
    def _():
        o_ref[...] = acc_ref[...].astype(o_ref.dtype)


def _mm1_kernel(a_ref, b_ref, o_ref):
    o_ref[...] = _dot(a_ref[...], b_ref[...]).astype(o_ref.dtype)


def _matmul(a, b, out_dtype, *, tm=512, tn=1024, tk=None, name="matmul"):
    m, kd = a.shape
    _, n = b.shape
    tm = _tile(m, tm, SUBLANES)
    tn = _tile(n, tn)
    tk = kd if tk is None else _tile(kd, tk)
    nk = kd // tk
    if nk == 1:
        return pl.pallas_call(
            _mm1_kernel,
            out_shape=jax.ShapeDtypeStruct((m, n), out_dtype),
            grid=(n // tn, m // tm),
            in_specs=[pl.BlockSpec((tm, kd), lambda j, i: (i, 0)),
                      pl.BlockSpec((kd, tn), lambda j, i: (0, j))],
            out_specs=pl.BlockSpec((tm, tn), lambda j, i: (i, j)),
            compiler_params=_params(("parallel", "parallel")),
            name=name,
        )(a, b)
    return pl.pallas_call(
        functools.partial(_mm_kernel, nk=nk),
        out_shape=jax.ShapeDtypeStruct((m, n), out_dtype),
        grid=(n // tn, m // tm, nk),
        in_specs=[pl.BlockSpec((tm, tk), lambda j, i, k: (i, k)),
                  pl.BlockSpec((tk, tn), lambda j, i, k: (k, j))],
        out_specs=pl.BlockSpec((tm, tn), lambda j, i, k: (i, j)),
        scratch_shapes=[pltpu.VMEM((tm, tn), _F32)],
        compiler_params=_params(("parallel", "parallel", "arbitrary")),
        name=name,
    )(a, b)


def _mlstm_kernel(*refs, L, has_state, has_alias):
    q_ref, k_ref, v_ref, o_ref, g_ref, gb_ref, gn_ref = refs[:7]
    if has_state:
        c0_ref, s0_ref = refs[7:9]
    h_ref, c_out_ref, s_out_ref, c_scr, n_scr, m_scr = refs[7 + 2 * has_state + has_alias:]
    head = pl.program_id(1)
    c = pl.program_id(2)
    nc = pl.num_programs(2)

    @pl.when(c == 0)
    def _():
        if has_state:
            c_scr[...] = c0_ref[0, 0]
            n_scr[...] = s0_ref[0, 0:1, :]
            m_scr[...] = s0_ref[0, 1:2, :]
        else:
            c_scr[...] = jnp.zeros_like(c_scr)
            n_scr[...] = jnp.zeros_like(n_scr)
            m_scr[...] = jnp.zeros_like(m_scr)

    gates = g_ref[...] + gb_ref[...]
    lane = lax.broadcasted_iota(jnp.int32, gates.shape, 1)
    ig_col = jnp.sum(jnp.where(lane == SM_MI + head, gates, 0.0), axis=1, keepdims=True)
    mf_col = jnp.sum(jnp.where(lane == SM_MF + head, gates, 0.0), axis=1, keepdims=True)
    lf_col = jnp.minimum(mf_col, 0.0) - jnp.log1p(jnp.exp(-jnp.abs(mf_col)))

    ri = lax.broadcasted_iota(jnp.int32, (L, L), 0)
    ci = lax.broadcasted_iota(jnp.int32, (L, L), 1)
    eye = ri == ci
    tril = ci <= ri
    lf_row = jnp.sum(jnp.where(eye, lf_col, 0.0), axis=0, keepdims=True)
    ig_row = jnp.sum(jnp.where(eye, ig_col, 0.0), axis=0, keepdims=True)
    b_col = jnp.sum(jnp.where(tril, lf_row, 0.0), axis=1, keepdims=True)
    b_row = jnp.sum(jnp.where(ri <= ci, lf_col, 0.0), axis=0, keepdims=True)
    logw = jnp.where(tril, b_col - b_row + ig_row, -jnp.inf)

    m_prev = m_scr[:, 0:1]
    inter = b_col + m_prev
    m_t = jnp.maximum(inter, jnp.max(logw, axis=1, keepdims=True))
    a = jnp.exp(inter - m_t)
    sw = jnp.exp(logw - m_t)

    q = q_ref[...]
    k = k_ref[...] * (M_QK ** -0.5)
    v = v_ref[...]
    qb = q.astype(_BF16)
    kb = k.astype(_BF16)
    s = _dot_nt(qb, kb) * sw
    c_old = c_scr[...]
    n_old = n_scr[...]
    num = a * _dot_nt(qb, c_old.astype(_BF16)) + _dot(s.astype(_BF16), v.astype(_BF16))
    den = a * jnp.sum(q * n_old, axis=1, keepdims=True) + jnp.sum(s, axis=1, keepdims=True)
    h = num / jnp.maximum(jnp.abs(den), jnp.exp(-m_t))

    m_new = m_t[L - 1:L, :]
    b_last = b_col[L - 1:L, :]
    g_col = jnp.exp(b_last - b_col + ig_col - m_new)
    decay = jnp.exp(b_last + m_prev - m_new)
    c_new = decay * c_old + _dot_tn((v * g_col).astype(_BF16), kb)
    n_new = decay * n_old + jnp.sum(g_col * k, axis=0, keepdims=True)
    c_scr[...] = c_new
    n_scr[...] = n_new
    m_scr[...] = jnp.broadcast_to(m_new, m_scr.shape)

    ms = jnp.mean(h * h, axis=1, keepdims=True)
    y = h * lax.rsqrt(ms + NORM_EPS) * gn_ref[0]
    h_ref[...] = (y * jax.nn.sigmoid(o_ref[...])).astype(h_ref.dtype)

    @pl.when(c == nc - 1)
    def _():
        c_out_ref[0, 0] = c_new
        s_out_ref[0] = jnp.zeros(s_out_ref.shape[1:], _F32)
        s_out_ref[0, 0:1, :] = n_new
        s_out_ref[0, 1:2, :] = jnp.broadcast_to(m_new, (1, M_QK))


def _mlstm(proj, small, gate_bias, g_mhnorm, row0, nseq, T, L, state, h_prev=None):
    nc = T // L
    rb0 = row0 // L
    has_state = state is not None
    has_alias = h_prev is not None

    def rows(b, h, c):
        return rb0 + b * nc + c

    in_specs = [
        pl.BlockSpec((L, M_QK), lambda b, h, c: (rows(b, h, c), A_MQ // M_QK + h)),
        pl.BlockSpec((L, M_QK), lambda b, h, c: (rows(b, h, c), A_MK // M_QK + h)),
        pl.BlockSpec((L, M_V), lambda b, h, c: (rows(b, h, c), A_MV // M_V + h)),
        pl.BlockSpec((L, M_V), lambda b, h, c: (rows(b, h, c), A_MO // M_V + h)),
        pl.BlockSpec((L, LANES), lambda b, h, c: (rows(b, h, c), 0)),
        pl.BlockSpec((1, LANES), lambda b, h, c: (0, 0)),
        pl.BlockSpec((1, 1, M_V), lambda b, h, c: (h, 0, 0)),
    ]
    args = [proj, proj, proj, proj, small, gate_bias, g_mhnorm.reshape(M_HEADS, 1, M_V)]
    if has_state:
        c0, s0 = state
        in_specs += [pl.BlockSpec((1, 1, M_V, M_QK), lambda b, h, c: (b, h, 0, 0)),
                     pl.BlockSpec((1, SUBLANES, M_QK), lambda b, h, c: (b * M_HEADS + h, 0, 0))]
        args += [c0, s0]
    aliases = {}
    if has_alias:
        aliases = {len(args): 0}
        in_specs.append(pl.BlockSpec(memory_space=pl.ANY))
        args.append(h_prev)
    out_shape = (jax.ShapeDtypeStruct((proj.shape[0], M_HEADS * M_V), _BF16),
                 jax.ShapeDtypeStruct((nseq, M_HEADS, M_V, M_QK), _F32),
                 jax.ShapeDtypeStruct((nseq * M_HEADS, SUBLANES, M_QK), _F32))
    out_specs = (pl.BlockSpec((L, M_V), lambda b, h, c: (rows(b, h, c), h)),
                 pl.BlockSpec((1, 1, M_V, M_QK), lambda b, h, c: (b, h, 0, 0)),
                 pl.BlockSpec((1, SUBLANES, M_QK), lambda b, h, c: (b * M_HEADS + h, 0, 0)))
    hm, c_new, stats = pl.pallas_call(
        functools.partial(_mlstm_kernel, L=L, has_state=has_state, has_alias=has_alias),
        out_shape=out_shape,
        grid=(nseq, M_HEADS, nc),
        in_specs=in_specs,
        out_specs=out_specs,
        scratch_shapes=[pltpu.VMEM((M_V, M_QK), _F32), pltpu.VMEM((1, M_QK), _F32),
                        pltpu.VMEM((1, M_QK), _F32)],
        input_output_aliases=aliases,
        compiler_params=_params(("parallel", "parallel", "arbitrary")),
        name="mlstm",
    )(*args)
    n_new = stats[:, 0, :].reshape(nseq, M_HEADS, M_QK)
    m_new = stats[:, 1, 0].reshape(nseq, M_HEADS)
    return hm, c_new, n_new, m_new


def _rope(x, cos, sin_lo, sin_hi, half):
    n = x.shape[-1]
    return (x * cos + pltpu.roll(x, n - half, 1) * sin_lo + pltpu.roll(x, half, 1) * sin_hi)


def _hi_lo(x):
    hi = x.astype(_BF16).astype(_F32)
    return hi, x - hi


def _rope_kernel(aq_ref, ak_ref, av_ref, iq_ref, sm_ref, ta_ref, ti_ref,
                 q_out, k_out, kb_out, vb_out, iqx_out, ik_out, ikx_out):
    ca, sa_lo, sa_hi = ta_ref[0], ta_ref[1], ta_ref[2]
    ci, si_lo, si_hi = ti_ref[0], ti_ref[1], ti_ref[2]
    for h in range(A_HEADS):
        sl = slice(h * A_HD, (h + 1) * A_HD)
        q_out[:, sl] = _rope(aq_ref[:, sl], ca, sa_lo, sa_hi, A_ROT // 2).astype(q_out.dtype)
    for h in range(A_KV_HEADS):
        sl = slice(h * A_HD, (h + 1) * A_HD)
        kr = _rope(ak_ref[:, sl], ca, sa_lo, sa_hi, A_ROT // 2)
        k_out[:, sl] = kr
        kb_out[:, sl] = kr.astype(kb_out.dtype)
    vb_out[...] = av_ref[...].astype(vb_out.dtype)
    low = lax.broadcasted_iota(jnp.int32, (1, LANES), 1) < IDX_DIM
    for p in range(IDX_HEADS * IDX_DIM // LANES):
        x = _rope(iq_ref[:, p * LANES:(p + 1) * LANES], ci, si_lo, si_hi, IDX_ROT // 2)
        hi, lo = _hi_lo(x)
        hi_sw = pltpu.roll(hi, IDX_DIM, 1)
        lo_sw = pltpu.roll(lo, IDX_DIM, 1)
        c0 = 2 * p * IDX_XW
        iqx_out[:, c0:c0 + LANES] = jnp.where(low, hi, hi_sw).astype(iqx_out.dtype)
        iqx_out[:, c0 + LANES:c0 + 2 * LANES] = jnp.where(low, lo, 0.0).astype(iqx_out.dtype)
        iqx_out[:, c0 + 2 * LANES:c0 + 3 * LANES] = jnp.where(low, hi_sw, hi).astype(iqx_out.dtype)
        iqx_out[:, c0 + 3 * LANES:c0 + 4 * LANES] = jnp.where(low, lo_sw, 0.0).astype(iqx_out.dtype)
    ik = _rope(sm_ref[...], ci, si_lo, si_hi, IDX_ROT // 2)
    ik_out[...] = ik[:, SM_IK:SM_IK + IDX_DIM]
    hi, lo = _hi_lo(ik)
    ikx_out[:, 0:LANES] = jnp.where(low, hi, pltpu.roll(lo, IDX_DIM, 1)).astype(ikx_out.dtype)
    ikx_out[:, LANES:2 * LANES] = jnp.where(low, hi, 0.0).astype(ikx_out.dtype)


def _rope_tables(pos, rot, width, reps_valid):
    half = rot // 2
    inv_freq = jnp.exp(jnp.arange(half, dtype=_F32) * (-2.0 * math.log(ROPE_THETA) / rot))
    ang = pos.astype(_F32)[:, None] * inv_freq[None, :]
    cos, sin = jnp.cos(ang), jnp.sin(ang)
    m = pos.shape[0]
    one = jnp.ones((m, width - rot), _F32)
    zero = jnp.zeros((m, width - rot), _F32)
    zh = jnp.zeros((m, half), _F32)
    c_head = jnp.concatenate([cos, cos, one], axis=1)
    lo_head = jnp.concatenate([-sin, zh, zero], axis=1)
    hi_head = jnp.concatenate([zh, sin, zero], axis=1)
    reps = LANES // width
    ident = (jnp.ones((m, width), _F32), jnp.zeros((m, width), _F32), jnp.zeros((m, width), _F32))
    out = []
    for t, idt in zip((c_head, lo_head, hi_head), ident):
        out.append(jnp.concatenate([t if r < reps_valid else idt for r in range(reps)], axis=1))
    return jnp.stack(out)


def _rope_all(proj, small, pos):
    m = proj.shape[0]
    tm = _tile(m, 256, SUBLANES)
    ta = _rope_tables(pos, A_ROT, A_HD, 1)
    ti = _rope_tables(pos, IDX_ROT, IDX_DIM, LANES // IDX_DIM)
    wq, wk, wi = A_HEADS * A_HD, A_KV_HEADS * A_HD, IDX_HEADS * IDX_DIM
    wix = IDX_HEADS * IDX_XW
    out_shape = (jax.ShapeDtypeStruct((m, wq), _BF16),
                 jax.ShapeDtypeStruct((m, wk), _F32),
                 jax.ShapeDtypeStruct((m, wk), _BF16),
                 jax.ShapeDtypeStruct((m, wk), _BF16),
                 jax.ShapeDtypeStruct((m, wix), _BF16),
                 jax.ShapeDtypeStruct((m, IDX_DIM), _F32),
                 jax.ShapeDtypeStruct((m, IDX_XW), _BF16))
    return pl.pallas_call(
        _rope_kernel,
        out_shape=out_shape,
        grid=(m // tm,),
        in_specs=[pl.BlockSpec((tm, wq), lambda i: (i, B_AQ // wq)),
                  pl.BlockSpec((tm, wk), lambda i: (i, B_AK // wk)),
                  pl.BlockSpec((tm, wk), lambda i: (i, B_AV // wk)),
                  pl.BlockSpec((tm, wi), lambda i: (i, B_IQ // wi)),
                  pl.BlockSpec((tm, LANES), lambda i: (i, 0)),
                  pl.BlockSpec((3, tm, LANES), lambda i: (0, i, 0)),
                  pl.BlockSpec((3, tm, LANES), lambda i: (0, i, 0))],
        out_specs=(pl.BlockSpec((tm, wq), lambda i: (i, 0)),
                   pl.BlockSpec((tm, wk), lambda i: (i, 0)),
                   pl.BlockSpec((tm, wk), lambda i: (i, 0)),
                   pl.BlockSpec((tm, wk), lambda i: (i, 0)),
                   pl.BlockSpec((tm, wix), lambda i: (i, 0)),
                   pl.BlockSpec((tm, IDX_DIM), lambda i: (i, 0)),
                   pl.BlockSpec((tm, IDX_XW), lambda i: (i, 0))),
        compiler_params=_params(("parallel",)),
        name="rope",
    )(proj, proj, proj, proj, small, ta, ti)


def _attn_kernel(*refs, TQ, TK, past_len, l_valid, topk):
    q_ref, iqx_ref, sm_ref, k_ref, v_ref, ikx_ref = refs[:6]
    o_ref, key_scr, bias_scr, w_scr = refs[-4:]
    i = pl.program_id(1)
    q_start = past_len + i * TQ
    qpos = q_start + lax.broadcasted_iota(jnp.int32, (TQ, 1), 0)
    qchunk = qpos // CHUNK
    last_vis = jnp.minimum(((q_start + TQ - 1) // CHUNK + 1) * CHUNK, l_valid)
    nkt = (last_vis + TK - 1) // TK
    lane_pos = lax.broadcasted_iota(jnp.int32, (1, TK), 1)

    w_scale = (IDX_DIM ** -0.5) * (IDX_HEADS ** -0.5)
    for h in range(IDX_HEADS):
        w_scr[h] = jnp.broadcast_to(sm_ref[:, SM_IW + h:SM_IW + h + 1] * w_scale, (TQ, LANES))
    lane128 = lax.broadcasted_iota(jnp.int32, (1, LANES), 1)
    sc = min(TK, 2 * LANES)

    def score_tile(kt, carry):
        for cc in range(TK // sc):
            k0 = pl.multiple_of(kt * TK + cc * sc, sc)
            ik_c = ikx_ref[pl.ds(k0, sc), :]
            accs = [jnp.zeros((TQ, LANES), _F32) for _ in range(sc // LANES)]
            for h in range(IDX_HEADS):
                isc = _dot_nt(iqx_ref[:, h * IDX_XW:(h + 1) * IDX_XW], ik_c)
                w_h = w_scr[h]
                for c in range(sc // LANES):
                    accs[c] = accs[c] + jnp.maximum(isc[:, c * LANES:(c + 1) * LANES], 0.0) * w_h
            for c in range(sc // LANES):
                bits = pltpu.bitcast(accs[c] + 0.0, jnp.int32)
                key = jnp.where(bits < 0, bits ^ 0x7FFFFFFF, bits)
                kpos = k0 + c * LANES + lane128
                vis = ((kpos // CHUNK) <= qchunk) & (kpos < l_valid)
                col = cc * sc + c * LANES
                key_scr[kt, :, col:col + LANES] = jnp.where(vis, key, INT_MIN)
        return carry

    lax.fori_loop(0, nkt, score_tile, 0)

    def lane_fold(x):
        out = x[:, 0:LANES]
        for c in range(1, TK // LANES):
            out = out + x[:, c * LANES:(c + 1) * LANES]
        return out

    def count(pred_fn):
        def body(kt, part):
            return part + lane_fold(pred_fn(key_scr[kt], kt).astype(jnp.int32))
        part = lax.fori_loop(0, nkt, body, jnp.zeros((TQ, LANES), jnp.int32))
        return jnp.sum(part, axis=1, keepdims=True)

    def bit_step(it, thr_u):
        cand_u = thr_u | lax.shift_left(jnp.int32(1), 31 - it)
        cand_s = cand_u ^ INT_MIN
        cnt = count(lambda key, kt: key >= cand_s)
        return jnp.where(cnt >= topk, cand_u, thr_u)

    thr = lax.fori_loop(0, 32, bit_step, jnp.zeros((TQ, 1), jnp.int32)) ^ INT_MIN
    n_ge = count(lambda key, kt: key >= thr)

    def tie_search(_):
        need = topk - count(lambda key, kt: key > thr)
        nbits = (key_scr.shape[0] * TK).bit_length()

        def pos_step(it, p):
            cand = p | lax.shift_left(jnp.int32(1), nbits - 1 - it)
            cnt = count(lambda key, kt: (key == thr) & ((kt * TK + lane_pos) < cand))
            return jnp.where(cnt < need, cand, p)

        return lax.fori_loop(0, nbits, pos_step, jnp.zeros((TQ, 1), jnp.int32))

    has_tie = jnp.max(n_ge.astype(_F32)) > topk
    p_last = lax.cond(has_tie, tie_search, lambda _: jnp.full((TQ, 1), 2 ** 31 - 1, jnp.int32), 0)

    def bias_tile(kt, carry):
        key = key_scr[kt]
        sel = ((key > thr) | ((key == thr) & ((kt * TK + lane_pos) <= p_last))) & (key > KEY_NEG_INF)
        bias_scr[kt] = jnp.where(sel, 0.0, NEG_BIG)
        return carry

    lax.fori_loop(0, nkt, bias_tile, 0)

    scale = (A_HD ** -0.5) * math.log2(math.e)
    rows = A_GROUP * TQ
    n_par = 1

    def q_group(g):
        return jnp.concatenate(
            [q_ref[:, (g * A_GROUP + j) * A_HD:(g * A_GROUP + j + 1) * A_HD] for j in range(A_GROUP)], axis=0)

    for g0 in range(0, A_KV_HEADS, n_par):
        qgs = [q_group(g0 + u) for u in range(n_par)]

        def attn_tile(kt, carry, g0=g0, qgs=qgs):
            k0 = pl.multiple_of(kt * TK, TK)
            bias = bias_scr[kt][None]
            s_u = [_dot_nt(qgs[u], k_ref[pl.ds(k0, TK), (g0 + u) * A_HD:(g0 + u + 1) * A_HD])
                   .reshape(A_GROUP, TQ, TK) * scale + bias for u in range(n_par)]
            new = []
            p_u = []
            for u in range(n_par):
                m_i, l_i, acc = carry[u]
                m_new = jnp.maximum(m_i, jnp.max(s_u[u], axis=-1, keepdims=True))
                alpha = jnp.exp2(m_i - m_new)
                p = jnp.exp2(s_u[u] - m_new)
                p_u.append(p.reshape(rows, TK).astype(_BF16))
                new.append((m_new, alpha * l_i + jnp.sum(p, axis=-1, keepdims=True), alpha * acc))
            out = []
            for u in range(n_par):
                v_t = v_ref[pl.ds(k0, TK), (g0 + u) * A_HD:(g0 + u + 1) * A_HD]
                m_new, l_new, acc = new[u]
                out.append((m_new, l_new, acc + _dot(p_u[u], v_t).reshape(A_GROUP, TQ, A_HD)))
            return tuple(out)

        init = tuple((jnp.full((A_GROUP, TQ, 1), NEG_BIG, _F32), jnp.zeros((A_GROUP, TQ, 1), _F32),
                      jnp.zeros((A_GROUP, TQ, A_HD), _F32)) for _ in range(n_par))
        fin = lax.fori_loop(0, nkt, attn_tile, init)
        for u in range(n_par):
            _, l_f, acc_f = fin[u]
            out = acc_f / l_f
            for j in range(A_GROUP):
                hh = (g0 + u) * A_GROUP + j
                o_ref[:, hh * A_HD:(hh + 1) * A_HD] = out[j].astype(o_ref.dtype)


def _attention(q_rot, iq_rot, small, k_all, v_all, ik_all, row0, nseq, T, TQ, TK, lp, past_len, l_valid, o_prev=None):
    nq = T // TQ
    qb0 = row0 // TQ
    topk = min(TOPK_MAX, l_valid // 4)
    wq, wk, wix = A_HEADS * A_HD, A_KV_HEADS * A_HD, IDX_HEADS * IDX_XW
    in_specs = [pl.BlockSpec((TQ, wq), lambda b, i: (qb0 + b * nq + i, 0)),
                pl.BlockSpec((TQ, wix), lambda b, i: (qb0 + b * nq + i, 0)),
                pl.BlockSpec((TQ, LANES), lambda b, i: (qb0 + b * nq + i, 0)),
                pl.BlockSpec((lp, wk), lambda b, i: (b, 0)),
                pl.BlockSpec((lp, wk), lambda b, i: (b, 0)),
                pl.BlockSpec((lp, IDX_XW), lambda b, i: (b, 0))]
    args = [q_rot, iq_rot, small, k_all, v_all, ik_all]
    aliases = {}
    if o_prev is not None:
        aliases = {len(args): 0}
        in_specs.append(pl.BlockSpec(memory_space=pl.ANY))
        args.append(o_prev)
    return pl.pallas_call(
        functools.partial(_attn_kernel, TQ=TQ, TK=TK, past_len=past_len, l_valid=l_valid, topk=topk),
        out_shape=jax.ShapeDtypeStruct((q_rot.shape[0], wq), _BF16),
        grid=(nseq, nq),
        in_specs=in_specs,
        out_specs=pl.BlockSpec((TQ, wq), lambda b, i: (qb0 + b * nq + i, 0)),
        scratch_shapes=[pltpu.VMEM((lp // TK, TQ, TK), jnp.int32),
                        pltpu.VMEM((lp // TK, TQ, TK), _F32),
                        pltpu.VMEM((IDX_HEADS, TQ, LANES), _F32)],
        input_output_aliases=aliases,
        compiler_params=_params(("parallel", "arbitrary")),
        name="sparse_attention",
    )(*args)


def _merge_kernel(hm_ref, ha_ref, wm_ref, wa_ref, gm_ref, ga_ref, o_ref):
    pm = _dot(hm_ref[...], wm_ref[...])
    pa = _dot(ha_ref[...], wa_ref[...])
    o_ref[...] = (jax.nn.sigmoid(gm_ref[...]) * pm + jax.nn.sigmoid(ga_ref[...]) * pa).astype(o_ref.dtype)


def _merge(hm, ha, wm, wa, proj, d):
    m = hm.shape[0]
    tm = _tile(m, 512, SUBLANES)
    tn = _tile(d, 1024)
    gm0 = 0
    ga0 = d // tn
    km, ka = hm.shape[1], ha.shape[1]
    return pl.pallas_call(
        _merge_kernel,
        out_shape=jax.ShapeDtypeStruct((m, d), _BF16),
        grid=(d // tn, m // tm),
        in_specs=[pl.BlockSpec((tm, km), lambda j, i: (i, 0)),
                  pl.BlockSpec((tm, ka), lambda j, i: (i, 0)),
                  pl.BlockSpec((km, tn), lambda j, i: (0, j)),
                  pl.BlockSpec((ka, tn), lambda j, i: (0, j)),
                  pl.BlockSpec((tm, tn), lambda j, i: (i, gm0 + j)),
                  pl.BlockSpec((tm, tn), lambda j, i: (i, ga0 + j))],
        out_specs=pl.BlockSpec((tm, tn), lambda j, i: (i, j)),
        compiler_params=_params(("parallel", "parallel")),
        name="gated_merge",
    )(hm, ha, wm, wa, proj, proj)


def _resnorm_next_kernel(xa_ref, xb_ref, y_ref, g_ref, g2_ref, x1_ref, h_ref, *, n_first):
    def body(x_ref):
        y = y_ref[...]
        ms = jnp.mean(y * y, axis=-1, keepdims=True)
        x1 = x_ref[...] + y * lax.rsqrt(ms + NORM_EPS) * g_ref[...]
        x1_ref[...] = x1
        ms1 = jnp.mean(x1 * x1, axis=-1, keepdims=True)
        h_ref[...] = (x1 * lax.rsqrt(ms1 + NORM_EPS) * g2_ref[...]).astype(h_ref.dtype)

    i = pl.program_id(0)
    pl.when(i < n_first)(lambda: body(xa_ref))
    pl.when(i >= n_first)(lambda: body(xb_ref))


def _resnorm_next(xa, xb, y, g, g_next):
    m, d = y.shape
    tm = _row_tile(xa.shape[0], xb.shape[0])
    n_first = xa.shape[0] // tm
    row = pl.BlockSpec((tm, d), lambda i: (i, 0))
    vec = pl.BlockSpec((1, d), lambda i: (0, 0))
    return pl.pallas_call(
        functools.partial(_resnorm_next_kernel, n_first=n_first),
        out_shape=(jax.ShapeDtypeStruct((m, d), _F32), jax.ShapeDtypeStruct((m, d), _BF16)),
        grid=(m // tm,), in_specs=[*_two_source_specs(tm, d, n_first), row, vec, vec], out_specs=(row, row),
        compiler_params=_params(("parallel",)), name="resnorm_next",
    )(xa, xb, y, g.reshape(1, d), g_next.reshape(1, d))


def _resnorm_kernel(x_ref, y_ref, g_ref, oa_ref, ob_ref, *, n_first):
    y = y_ref[...]
    ms = jnp.mean(y * y, axis=-1, keepdims=True)
    out = x_ref[...] + y * lax.rsqrt(ms + NORM_EPS) * g_ref[...]
    i = pl.program_id(0)

    @pl.when(i < n_first)
    def _():
        oa_ref[...] = out

    @pl.when(i >= n_first)
    def _():
        ob_ref[...] = out


def _resnorm_split(x, y, g, m_a):
    m, d = x.shape
    m_b = m - m_a
    tm = _row_tile(m_a, m_b)
    n_first = m_a // tm
    row = pl.BlockSpec((tm, d), lambda i: (i, 0))
    vec = pl.BlockSpec((1, d), lambda i: (0, 0))
    return pl.pallas_call(
        functools.partial(_resnorm_kernel, n_first=n_first),
        out_shape=(jax.ShapeDtypeStruct((m_a, d), _F32), jax.ShapeDtypeStruct((m_b, d), _F32)),
        grid=(m // tm,), in_specs=[row, row, vec], out_specs=_two_source_specs(tm, d, n_first),
        compiler_params=_params(("arbitrary",)), name="resnorm",
    )(x, y, g.reshape(1, d))


HALO = SUBLANES


def _conv_gelu_gate(cw_ref, cb_ref, prev2, prev1, gate, lin):
    gc = cw_ref[0:1, :] * prev2 + cw_ref[1:2, :] * prev1 + cw_ref[2:3, :] * gate + cb_ref[...]
    c = -2.0 * math.sqrt(2.0 / math.pi)
    act = gc / (1.0 + jnp.exp(gc * (c + (c * 0.044715) * (gc * gc))))
    return act * lin


def _ffn_up_first_kernel(h_ref, wg_ref, wl_ref, cw_ref, cb_ref, z_ref, tail_ref, wg_bf, wl_bf, g_scr,
                         *, tm, seq_len):
    i = pl.program_id(1)
    tn = z_ref.shape[1]

    @pl.when(i == 0)
    def _():
        wg_bf[...] = wg_ref[...].astype(wg_bf.dtype)
        wl_bf[...] = wl_ref[...].astype(wl_bf.dtype)

    @pl.when((i * tm) % seq_len == 0)
    def _():
        g_scr[0:HALO, :] = jnp.zeros((HALO, tn), _F32)

    h = h_ref[...]
    gate = _dot(h, wg_bf[...])
    lin = _dot(h, wl_bf[...])
    g_scr[HALO:HALO + tm, :] = gate
    zed = _conv_gelu_gate(cw_ref, cb_ref, g_scr[HALO - 2:HALO - 2 + tm, :], g_scr[HALO - 1:HALO - 1 + tm, :],
                          gate, lin)
    z_ref[...] = zed.astype(z_ref.dtype)
    g_scr[0:HALO, :] = gate[tm - HALO:tm, :]
    tail_ref[...] = gate[tm - SUBLANES:tm, :]


def _ffn_up_state_kernel(h_ref, wg_ref, wl_ref, cw_ref, cb_ref, init_ref, z_any, z_ref, tail_ref, g_scr,
                         *, tm, seq_len):
    tn = z_ref.shape[1]
    h = h_ref[...]
    gate = _dot(h, wg_ref[...].astype(_BF16))
    lin = _dot(h, wl_ref[...].astype(_BF16))
    g_scr[0:HALO, :] = jnp.zeros((HALO, tn), _F32)
    g_scr[HALO:HALO + tm, :] = gate
    prev1 = g_scr[HALO - 1:HALO - 1 + tm, :]
    prev2 = g_scr[HALO - 2:HALO - 2 + tm, :]
    nseg = tm // seq_len
    init0 = jnp.concatenate([jnp.broadcast_to(init_ref[s, 0:1, :], (seq_len, tn)) for s in range(nseg)], axis=0)
    init1 = jnp.concatenate([jnp.broadcast_to(init_ref[s, 1:2, :], (seq_len, tn)) for s in range(nseg)], axis=0)
    t = lax.broadcasted_iota(jnp.int32, (tm, 1), 0) % seq_len
    prev1 = jnp.where(t == 0, init1, prev1)
    prev2 = jnp.where(t == 0, init0, jnp.where(t == 1, init1, prev2))
    z_ref[...] = _conv_gelu_gate(cw_ref, cb_ref, prev2, prev1, gate, lin).astype(z_ref.dtype)
    for s in range(nseg):
        tail_ref[s * SUBLANES:(s + 1) * SUBLANES, :] = gate[(s + 1) * seq_len - SUBLANES:(s + 1) * seq_len, :]


def _conv_state(tails, nseq, segs_per_seq, dff):
    tails = tails.reshape(nseq * segs_per_seq, SUBLANES, dff)
    last = tails[segs_per_seq - 1::segs_per_seq, SUBLANES - (CONV_W - 1):, :]
    return last.reshape(nseq, CONV_W - 1, dff)


def _ffn_up_specs(d, tn, nj, h_spec):
    return [h_spec,
            pl.BlockSpec((d, tn), lambda j, i: (0, j)),
            pl.BlockSpec((d, tn), lambda j, i: (0, nj + j)),
            pl.BlockSpec((CONV_W, tn), lambda j, i: (0, j)),
            pl.BlockSpec((1, tn), lambda j, i: (0, j))]


def _ffn_up_first(hf, w_up, conv_w, conv_b, nrows, seq_len):
    m_all, d = hf.shape
    dff = w_up.shape[1] // 2
    tm = _tile(seq_len, 512, SUBLANES)
    tn = _tile(dff, 256)
    nj = dff // tn
    nt = nrows // tm
    specs = _ffn_up_specs(d, tn, nj, pl.BlockSpec((tm, d), lambda j, i: (i, 0)))
    z, tails = pl.pallas_call(
        functools.partial(_ffn_up_first_kernel, tm=tm, seq_len=seq_len),
        out_shape=(jax.ShapeDtypeStruct((m_all, dff), _BF16),
                   jax.ShapeDtypeStruct((nt * SUBLANES, dff), _F32)),
        grid=(nj, nt),
        in_specs=specs,
        out_specs=(pl.BlockSpec((tm, tn), lambda j, i: (i, j)),
                   pl.BlockSpec((SUBLANES, tn), lambda j, i: (i, j))),
        scratch_shapes=[pltpu.VMEM((d, tn), _BF16), pltpu.VMEM((d, tn), _BF16),
                        pltpu.VMEM((HALO + tm, tn), _F32)],
        compiler_params=_params(("parallel", "arbitrary")),
        name="ffn_up_conv_first",
    )(hf, w_up, w_up, conv_w, conv_b.reshape(1, dff))
    return z, _conv_state(tails, nrows // seq_len, seq_len // tm, dff)


def _ffn_up_state(hf, w_up, conv_w, conv_b, row0, nrows, seq_len, init, z_prev):
    m_all, d = hf.shape
    dff = w_up.shape[1] // 2
    tm = _tile(nrows, 512, seq_len)
    tn = _tile(dff, 256)
    nj = dff // tn
    nseg = tm // seq_len
    rb0 = row0 // tm
    specs = _ffn_up_specs(d, tn, nj, pl.BlockSpec((tm, d), lambda j, i: (rb0 + i, 0)))
    specs += [pl.BlockSpec((nseg, CONV_W - 1, tn), lambda j, i: (i, 0, j)), pl.BlockSpec(memory_space=pl.ANY)]
    z, tails = pl.pallas_call(
        functools.partial(_ffn_up_state_kernel, tm=tm, seq_len=seq_len),
        out_shape=(jax.ShapeDtypeStruct((m_all, dff), _BF16),
                   jax.ShapeDtypeStruct((nrows // seq_len * SUBLANES, dff), _F32)),
        grid=(nj, nrows // tm),
        in_specs=specs,
        out_specs=(pl.BlockSpec((tm, tn), lambda j, i: (rb0 + i, j)),
                   pl.BlockSpec((nseg * SUBLANES, tn), lambda j, i: (i, j))),
        scratch_shapes=[pltpu.VMEM((HALO + tm, tn), _F32)],
        input_output_aliases={6: 0},
        compiler_params=_params(("parallel", "arbitrary")),
        name="ffn_up_conv_state",
    )(hf, w_up, w_up, conv_w, conv_b.reshape(1, dff), init, z_prev)
    return z, _conv_state(tails, nrows // seq_len, 1, dff)


def _layer(xa, xb, geom, cache, state, w):
    (B, S, Bd, Td, P) = geom
    (cache_k, cache_v, cache_ik) = cache
    (state_C, state_n, state_m, state_conv) = state
    (w_in, b_igate, b_fgate, g_mhnorm, w_proj_m, w_proj_a, w_out,
     g_pre_mix, g_post_mix, g_pre_ffn, g_post_ffn, w_up, conv_w, conv_b, w_down) = w
    mp, d = xa.shape
    dff = w_down.shape[0]
    wk = A_KV_HEADS * A_HD

    c_mi = A_END
    c_aq = c_mi + 2 * M_HEADS
    c_ik = c_aq + B_END
    c_iw = c_ik + IDX_DIM
    c_gm = c_iw + IDX_HEADS
    w_t = jnp.swapaxes(w_in, 0, 1).astype(_F32)
    w_small_t = jnp.concatenate([w_t[c_ik:c_iw], w_t[c_mi:c_aq], w_t[c_iw:c_gm],
                                 jnp.zeros((LANES - SM_END, d), _F32)], axis=0)
    gate_bias = jnp.concatenate([jnp.zeros((SM_MI,), _F32), b_igate.astype(_F32), b_fgate.astype(_F32),
                                 jnp.zeros((LANES - SM_IW,), _F32)]).reshape(1, LANES)

    hn = _rmsnorm_cast(xa, xb, g_pre_mix)
    proj_a = _matmul_wt(hn, w_t, 0, A_END, _F32, name="in_proj_mlstm")
    proj_b = _matmul_wt(hn, w_t, c_aq, B_END, _F32, name="in_proj_attn")
    proj_c = _matmul_wt(hn, w_t, c_gm, 2 * d, _F32, name="in_proj_gates")
    small = _matmul_wt(hn, w_small_t, 0, LANES, _F32, name="in_proj_small")

    lp_chunk = _tile(S, 256, CHUNK)
    hm_p, c_p, n_p, m_p = _mlstm(proj_a, small, gate_bias, g_mhnorm, 0, B, S, lp_chunk, None)
    s0 = jnp.zeros((Bd * M_HEADS, SUBLANES, M_QK), _F32)
    s0 = s0.at[:, 0, :].set(state_n.reshape(Bd * M_HEADS, M_QK).astype(_F32))
    s0 = s0.at[:, 1, :].set(jnp.broadcast_to(state_m.reshape(Bd * M_HEADS, 1).astype(_F32), (Bd * M_HEADS, M_QK)))
    hm, c_s, n_s, m_s = _mlstm(proj_a, small, gate_bias, g_mhnorm, mp, Bd, Td, min(CHUNK, Td),
                               (state_C.astype(_F32), s0), hm_p)

    pos = jnp.concatenate([jnp.tile(jnp.arange(S, dtype=jnp.int32), B),
                           jnp.tile(P + jnp.arange(Td, dtype=jnp.int32), Bd)])
    q_rot, k_rot, k_bf, v_bf, iqx, ik_rot, ikx = _rope_all(proj_b, small, pos)
    tq_p = _tile(S, 256, CHUNK)
    tk_p = _tile(S, 512)
    ha_p = _attention(q_rot, iqx, small, k_bf, v_bf, ikx, 0, B, S, tq_p, tk_p, S, 0, S)
    l_s = P + Td
    tk_s = 256
    lp_s = -(-l_s // tk_s) * tk_s
    pad = lp_s - l_s
    k_s = jnp.concatenate([cache_k.reshape(Bd, P, wk).astype(_BF16), k_bf[mp:].reshape(Bd, Td, wk),
                           jnp.zeros((Bd, pad, wk), _BF16)], axis=1).reshape(Bd * lp_s, wk)
    v_s = jnp.concatenate([cache_v.reshape(Bd, P, wk).astype(_BF16), v_bf[mp:].reshape(Bd, Td, wk),
                           jnp.zeros((Bd, pad, wk), _BF16)], axis=1).reshape(Bd * lp_s, wk)
    cik = cache_ik.astype(_F32)
    cik_hi = cik.astype(_BF16)
    cik_lo = (cik - cik_hi.astype(_F32)).astype(_BF16)
    cikx = jnp.concatenate([cik_hi, cik_lo, cik_hi, jnp.zeros_like(cik_hi)], axis=-1)
    ikx_s = jnp.concatenate([cikx, ikx[mp:].reshape(Bd, Td, IDX_XW),
                             jnp.zeros((Bd, pad, IDX_XW), _BF16)], axis=1).reshape(Bd * lp_s, IDX_XW)
    ha = _attention(q_rot, iqx, small, k_s, v_s, ikx_s, mp, Bd, Td, Td, tk_s, lp_s, P, l_s, ha_p)

    mix = _merge(hm, ha, w_proj_m.astype(_BF16), w_proj_a.astype(_BF16), proj_c, d)
    y1 = _matmul(mix, w_out.astype(_BF16), _F32, tm=512, tn=1024, name="out_proj")
    x1, hf = _resnorm_next(xa, xb, y1, g_post_mix, g_pre_ffn)

    w_up = w_up.astype(_F32)
    z_p, conv_p = _ffn_up_first(hf, w_up, conv_w, conv_b, mp, S)
    z, conv_s = _ffn_up_state(hf, w_up, conv_w, conv_b, mp, Bd * Td, Td, state_conv.astype(_F32), z_p)
    y2 = _matmul(z, w_down.astype(_BF16), _F32, tm=512, tn=1024, tk=dff // 2, name="down_proj")
    x2 = _resnorm_split(x1, y2, g_post_ffn, mp)

    av = proj_b[:, B_AV:B_AV + wk]
    outs_p = (k_rot[:mp].reshape(B, S, A_KV_HEADS, A_HD), av[:mp].reshape(B, S, A_KV_HEADS, A_HD),
              ik_rot[:mp].reshape(B, S, IDX_DIM), c_p, n_p, m_p, conv_p)
    outs_s = (k_rot[mp:].reshape(Bd, Td, A_KV_HEADS, A_HD), av[mp:].reshape(Bd, Td, A_KV_HEADS, A_HD),
              ik_rot[mp:].reshape(Bd, Td, IDX_DIM), c_s, n_s, m_s, conv_s)
    return x2, outs_p, outs_s


def kernel(x_prompt, x_sample, cache_k, cache_v, cache_idx_k, state_C, state_n, state_m, state_conv,
           w_in, b_igate, b_fgate, g_mhnorm, w_proj_m, w_proj_a, w_out,
           g_pre_mix, g_post_mix, g_pre_ffn, g_post_ffn, w_up, conv_w, conv_b, w_down):
    B, S, d = x_prompt.shape
    Bd, Td, _ = x_sample.shape
    P = cache_k.shape[2]
    depth = w_in.shape[0]
    mp = B * S
    xa, xb = x_prompt.reshape(mp, d), x_sample.reshape(Bd * Td, d)
    all_p, all_s = [], []
    for l in range(depth):
        w = (w_in[l], b_igate[l], b_fgate[l], g_mhnorm[l], w_proj_m[l], w_proj_a[l], w_out[l],
             g_pre_mix[l], g_post_mix[l], g_pre_ffn[l], g_post_ffn[l], w_up[l], conv_w[l], conv_b[l], w_down[l])
        (xa, xb), outs_p, outs_s = _layer(xa, xb, (B, S, Bd, Td, P), (cache_k[l], cache_v[l], cache_idx_k[l]),
                                          (state_C[l], state_n[l], state_m[l], state_conv[l]), w)
        all_p.append(outs_p)
        all_s.append(outs_s)

    def stk(outs, i):
        return jnp.stack([o[i] for o in outs])

    yp = xa.reshape(B, S, d)
    ys = xb.reshape(Bd, Td, d)
    return (yp, ys) + tuple(stk(all_p, i) for i in range(7)) + tuple(stk(all_s, i) for i in range(7))
```

```python
import functools
import math

import jax
import jax.numpy as jnp
from jax import lax
from jax.experimental import pallas as pl
from jax.experimental.pallas import tpu as pltpu

CHUNK = 64
NORM_EPS = 1e-6
ROPE_THETA = 500000.0
M_HEADS = 8
M_QK = 128
M_V = 256
A_HEADS = 16
A_KV_HEADS = 4
A_GROUP = A_HEADS // A_KV_HEADS
A_HD = 128
A_ROT = A_HD // 4
IDX_HEADS = 16
IDX_DIM = 64
IDX_ROT = IDX_DIM // 4
TOPK_MAX = 256
CONV_W = 3

LANES = 128
SUBLANES = 8
VMEM_LIMIT_BYTES = 52 * 1024 * 1024

A_MQ = 0
A_MK = A_MQ + M_HEADS * M_QK
A_MV = A_MK + M_HEADS * M_QK
A_MO = A_MV + M_HEADS * M_V
A_END = A_MO + M_HEADS * M_V
B_AQ = 0
B_AK = B_AQ + A_HEADS * A_HD
B_AV = B_AK + A_KV_HEADS * A_HD
B_IQ = B_AV + A_KV_HEADS * A_HD
B_END = B_IQ + IDX_HEADS * IDX_DIM
SM_IK = 0
SM_MI = SM_IK + IDX_DIM
SM_MF = SM_MI + M_HEADS
SM_IW = SM_MF + M_HEADS
SM_END = SM_IW + IDX_HEADS
IDX_XW = 4 * IDX_DIM

INT_MIN = -2 ** 31
KEY_NEG_INF = INT_MIN + 0x7FFFFF
NEG_BIG = -1e30

_BF16 = jnp.bfloat16
_F32 = jnp.float32


def _tile(dim, target, quantum=LANES):
    if dim <= target:
        return dim
    t = (target // quantum) * quantum
    while t >= quantum:
        if dim % t == 0:
            return t
        t -= quantum
    return dim


def _params(sem):
    return pltpu.CompilerParams(dimension_semantics=sem, vmem_limit_bytes=VMEM_LIMIT_BYTES)


def _dot(a, b):
    return jnp.dot(a, b, preferred_element_type=_F32)


def _dot_nt(a, b):
    return lax.dot_general(a, b, (((1,), (1,)), ((), ())), preferred_element_type=_F32)


def _dot_tn(a, b):
    return lax.dot_general(a, b, (((0,), (0,)), ((), ())), preferred_element_type=_F32)


def _mm_w32_kernel(a_ref, w_hbm, o_ref, w_f32, w_bf, sem, *, off0, nj, transposed):
    j = pl.program_id(0)
    kd, tn = w_bf.shape

    def tile_copy(jj):
        if transposed:
            src = w_hbm.at[pl.ds(pl.multiple_of(off0 + jj * tn, SUBLANES), tn), :]
        else:
            src = w_hbm.at[:, pl.ds(pl.multiple_of(off0 + jj * tn, LANES), tn)]
        return pltpu.make_async_copy(src, w_f32, sem)

    @pl.when(pl.program_id(1) == 0)
    def _():
        @pl.when(j == 0)
        def _():
            tile_copy(0).start()

        tile_copy(j).wait()
        if transposed:
            sq = math.gcd(tn, kd, 512)
            for r in range(0, tn, sq):
                for c in range(0, kd, sq):
                    w_bf[c:c + sq, r:r + sq] = w_f32[r:r + sq, c:c + sq].T.astype(w_bf.dtype)
        else:
            w_bf[...] = w_f32[...].astype(w_bf.dtype)

        @pl.when(j + 1 < nj)
        def _():
            tile_copy(j + 1).start()

    o_ref[...] = _dot(a_ref[...], w_bf[...]).astype(o_ref.dtype)


def _matmul_w32(a, w, off0, n, out_dtype, *, transposed, tm=512, tn=1024, name="matmul_w32"):
    m, kd = a.shape
    tm = _tile(m, tm, SUBLANES)
    tn = _tile(n, tn)
    nj = n // tn
    return pl.pallas_call(
        functools.partial(_mm_w32_kernel, off0=off0, nj=nj, transposed=transposed),
        out_shape=jax.ShapeDtypeStruct((m, n), out_dtype),
        grid=(nj, m // tm),
        in_specs=[pl.BlockSpec((tm, kd), lambda j, i: (i, 0)),
                  pl.BlockSpec(memory_space=pl.ANY)],
        out_specs=pl.BlockSpec((tm, tn), lambda j, i: (i, j)),
        scratch_shapes=[pltpu.VMEM((tn, kd) if transposed else (kd, tn), _F32), pltpu.VMEM((kd, tn), _BF16),
                        pltpu.SemaphoreType.DMA],
        compiler_params=_params(("arbitrary", "arbitrary")),
        name=name,
    )(a, w)


def _two_source_specs(tm, d, n_first):
    return (pl.BlockSpec((tm, d), lambda i: (jnp.minimum(i, n_first - 1), 0)),
            pl.BlockSpec((tm, d), lambda i: (jnp.maximum(i - n_first, 0), 0)))


def _rmsnorm_kernel(xa_ref, xb_ref, g_ref, o_ref, *, n_first):
    def body(x_ref):
        x = x_ref[...]
        ms = jnp.mean(x * x, axis=-1, keepdims=True)
        o_ref[...] = (x * lax.rsqrt(ms + NORM_EPS) * g_ref[...]).astype(o_ref.dtype)

    i = pl.program_id(0)
    pl.when(i < n_first)(lambda: body(xa_ref))
    pl.when(i >= n_first)(lambda: body(xb_ref))


def _row_tile(m_a, m_b):
    return _tile(math.gcd(m_a, m_b), 256, SUBLANES)


def _rmsnorm_cast(xa, xb, g):
    (m_a, d), m_b = xa.shape, xb.shape[0]
    tm = _row_tile(m_a, m_b)
    n_first = m_a // tm
    return pl.pallas_call(
        functools.partial(_rmsnorm_kernel, n_first=n_first),
        out_shape=jax.ShapeDtypeStruct((m_a + m_b, d), _BF16),
        grid=((m_a + m_b) // tm,),
        in_specs=[*_two_source_specs(tm, d, n_first), pl.BlockSpec((1, d), lambda i: (0, 0))],
        out_specs=pl.BlockSpec((tm, d), lambda i: (i, 0)),
        compiler_params=_params(("parallel",)),
        name="rmsnorm_cast",
    )(xa, xb, g.reshape(1, d))


def _mm_kernel(a_ref, b_ref, o_ref, acc_ref, *, nk):
    k = pl.program_id(2)

    @pl.when(k == 0)
    def _():
        acc_ref[...] = _dot(a_ref[...], b_ref[...])

    @pl.when((k > 0) & (k < nk - 1))
    def _():
        acc_ref[...] += _dot(a_ref[...], b_ref[...])

    @pl.when(k == nk - 1)
    def _():
        o_ref[...] = (acc_ref[...] + _dot(a_ref[...], b_ref[...])).astype(o_ref.dtype)


def _matmul(a, b, out_dtype, *, tm=512, tn=1024, tk, name="matmul"):
    m, kd = a.shape
    _, n = b.shape
    tm = _tile(m, tm, SUBLANES)
    tn = _tile(n, tn)
    tk = _tile(kd, tk)
    nk = kd // tk
    assert nk >= 2
    return pl.pallas_call(
        functools.partial(_mm_kernel, nk=nk),
        out_shape=jax.ShapeDtypeStruct((m, n), out_dtype),
        grid=(n // tn, m // tm, nk),
        in_specs=[pl.BlockSpec((tm, tk), lambda j, i, k: (i, k)),
                  pl.BlockSpec((tk, tn), lambda j, i, k: (k, j))],
        out_specs=pl.BlockSpec((tm, tn), lambda j, i, k: (i, j)),
        scratch_shapes=[pltpu.VMEM((tm, tn), _F32)],
        compiler_params=_params(("parallel", "parallel", "arbitrary")),
        name=name,
    )(a, b)


def _mlstm_kernel(*refs, L, has_state, has_alias):
    q_ref, k_ref, v_ref, o_ref, g_ref, gb_ref, gn_ref = refs[:7]
    if has_state:
        c0_ref, s0_ref = refs[7:9]
    h_ref, c_out_ref, s_out_ref, c_scr, n_scr, m_scr = refs[7 + 2 * has_state + has_alias:]
    head = pl.program_id(1)
    c = pl.program_id(2)
    nc = pl.num_programs(2)

    @pl.when(c == 0)
    def _():
        if has_state:
            c_scr[...] = c0_ref[0, 0]
            n_scr[...] = s0_ref[0, 0:1, :]
            m_scr[...] = s0_ref[0, 1:2, :]
        else:
            c_scr[...] = jnp.zeros_like(c_scr)
            n_scr[...] = jnp.zeros_like(n_scr)
            m_scr[...] = jnp.zeros_like(m_scr)

    gates = g_ref[...] + gb_ref[...]
    lane = lax.broadcasted_iota(jnp.int32, gates.shape, 1)
    ig_col = jnp.sum(jnp.where(lane == SM_MI + head, gates, 0.0), axis=1, keepdims=True)
    mf_col = jnp.sum(jnp.where(lane == SM_MF + head, gates, 0.0), axis=1, keepdims=True)
    lf_col = jnp.minimum(mf_col, 0.0) - jnp.log1p(jnp.exp(-jnp.abs(mf_col)))

    ri = lax.broadcasted_iota(jnp.int32, (L, L), 0)
    ci = lax.broadcasted_iota(jnp.int32, (L, L), 1)
    eye = ri == ci
    tril = ci <= ri
    lf_row = jnp.sum(jnp.where(eye, lf_col, 0.0), axis=0, keepdims=True)
    ig_row = jnp.sum(jnp.where(eye, ig_col, 0.0), axis=0, keepdims=True)
    b_col = jnp.sum(jnp.where(tril, lf_row, 0.0), axis=1, keepdims=True)
    b_row = jnp.sum(jnp.where(ri <= ci, lf_col, 0.0), axis=0, keepdims=True)
    logw = jnp.where(tril, b_col - b_row + ig_row, -jnp.inf)

    m_prev = m_scr[:, 0:1]
    inter = b_col + m_prev
    m_t = jnp.maximum(inter, jnp.max(logw, axis=1, keepdims=True))
    a = jnp.exp(inter - m_t)
    sw = jnp.exp(logw - m_t)

    q = q_ref[...]
    k = k_ref[...] * (M_QK ** -0.5)
    v = v_ref[...]
    qb = q.astype(_BF16)
    kb = k.astype(_BF16)
    s = _dot_nt(qb, kb) * sw
    c_old = c_scr[...]
    n_old = n_scr[...]
    num = a * _dot_nt(qb, c_old.astype(_BF16)) + _dot(s.astype(_BF16), v.astype(_BF16))
    den = a * jnp.sum(q * n_old, axis=1, keepdims=True) + jnp.sum(s, axis=1, keepdims=True)
    h = num / jnp.maximum(jnp.abs(den), jnp.exp(-m_t))

    m_new = m_t[L - 1:L, :]
    b_last = b_col[L - 1:L, :]
    g_col = jnp.exp(b_last - b_col + ig_col - m_new)
    decay = jnp.exp(b_last + m_prev - m_new)
    c_new = decay * c_old + _dot_tn((v * g_col).astype(_BF16), kb)
    n_new = decay * n_old + jnp.sum(g_col * k, axis=0, keepdims=True)
    c_scr[...] = c_new
    n_scr[...] = n_new
    m_scr[...] = jnp.broadcast_to(m_new, m_scr.shape)

    ms = jnp.mean(h * h, axis=1, keepdims=True)
    y = h * lax.rsqrt(ms + NORM_EPS) * gn_ref[0]
    h_ref[...] = (y * jax.nn.sigmoid(o_ref[...])).astype(h_ref.dtype)

    @pl.when(c == nc - 1)
    def _():
        c_out_ref[0, 0] = c_new
        s_out_ref[0] = jnp.zeros(s_out_ref.shape[1:], _F32)
        s_out_ref[0, 0:1, :] = n_new
        s_out_ref[0, 1:2, :] = jnp.broadcast_to(m_new, (1, M_QK))


def _mlstm(proj, small, gate_bias, g_mhnorm, row0, nseq, T, L, state, h_prev=None):
    nc = T // L
    rb0 = row0 // L
    has_state = state is not None
    has_alias = h_prev is not None

    def rows(b, h, c):
        return rb0 + b * nc + c

    in_specs = [
        pl.BlockSpec((L, M_QK), lambda b, h, c: (rows(b, h, c), A_MQ // M_QK + h)),
        pl.BlockSpec((L, M_QK), lambda b, h, c: (rows(b, h, c), A_MK // M_QK + h)),
        pl.BlockSpec((L, M_V), lambda b, h, c: (rows(b, h, c), A_MV // M_V + h)),
        pl.BlockSpec((L, M_V), lambda b, h, c: (rows(b, h, c), A_MO // M_V + h)),
        pl.BlockSpec((L, LANES), lambda b, h, c: (rows(b, h, c), 0)),
        pl.BlockSpec((1, LANES), lambda b, h, c: (0, 0)),
        pl.BlockSpec((1, 1, M_V), lambda b, h, c: (h, 0, 0)),
    ]
    args = [proj, proj, proj, proj, small, gate_bias, g_mhnorm.reshape(M_HEADS, 1, M_V)]
    if has_state:
        c0, s0 = state
        in_specs += [pl.BlockSpec((1, 1, M_V, M_QK), lambda b, h, c: (b, h, 0, 0)),
                     pl.BlockSpec((1, SUBLANES, M_QK), lambda b, h, c: (b * M_HEADS + h, 0, 0))]
        args += [c0, s0]
    aliases = {}
    if has_alias:
        aliases = {len(args): 0}
        in_specs.append(pl.BlockSpec(memory_space=pl.ANY))
        args.append(h_prev)
    out_shape = (jax.ShapeDtypeStruct((proj.shape[0], M_HEADS * M_V), _BF16),
                 jax.ShapeDtypeStruct((nseq, M_HEADS, M_V, M_QK), _F32),
                 jax.ShapeDtypeStruct((nseq * M_HEADS, SUBLANES, M_QK), _F32))
    out_specs = (pl.BlockSpec((L, M_V), lambda b, h, c: (rows(b, h, c), h)),
                 pl.BlockSpec((1, 1, M_V, M_QK), lambda b, h, c: (b, h, 0, 0)),
                 pl.BlockSpec((1, SUBLANES, M_QK), lambda b, h, c: (b * M_HEADS + h, 0, 0)))
    hm, c_new, stats = pl.pallas_call(
        functools.partial(_mlstm_kernel, L=L, has_state=has_state, has_alias=has_alias),
        out_shape=out_shape,
        grid=(nseq, M_HEADS, nc),
        in_specs=in_specs,
        out_specs=out_specs,
        scratch_shapes=[pltpu.VMEM((M_V, M_QK), _F32), pltpu.VMEM((1, M_QK), _F32),
                        pltpu.VMEM((1, M_QK), _F32)],
        input_output_aliases=aliases,
        compiler_params=_params(("parallel", "parallel", "arbitrary")),
        name="mlstm",
    )(*args)
    n_new = stats[:, 0, :].reshape(nseq, M_HEADS, M_QK)
    m_new = stats[:, 1, 0].reshape(nseq, M_HEADS)
    return hm, c_new, n_new, m_new


def _rope(x, cos, sin_lo, sin_hi, half):
    n = x.shape[-1]
    return (x * cos + pltpu.roll(x, n - half, 1) * sin_lo + pltpu.roll(x, half, 1) * sin_hi)


def _hi_lo(x):
    hi = x.astype(_BF16).astype(_F32)
    return hi, x - hi


def _rope_kernel(aq_ref, ak_ref, av_ref, iq_ref, sm_ref, ta_ref, ti_ref,
                 q_out, k_out, kb_out, vb_out, iqx_out, ik_out, ikx_out):
    ca, sa_lo, sa_hi = ta_ref[0], ta_ref[1], ta_ref[2]
    ci, si_lo, si_hi = ti_ref[0], ti_ref[1], ti_ref[2]
    for h in range(A_HEADS):
        sl = slice(h * A_HD, (h + 1) * A_HD)
        q_out[:, sl] = _rope(aq_ref[:, sl], ca, sa_lo, sa_hi, A_ROT // 2).astype(q_out.dtype)
    for h in range(A_KV_HEADS):
        sl = slice(h * A_HD, (h + 1) * A_HD)
        kr = _rope(ak_ref[:, sl], ca, sa_lo, sa_hi, A_ROT // 2)
        k_out[:, sl] = kr
        kb_out[:, sl] = kr.astype(kb_out.dtype)
    vb_out[...] = av_ref[...].astype(vb_out.dtype)
    low = lax.broadcasted_iota(jnp.int32, (1, LANES), 1) < IDX_DIM
    for p in range(IDX_HEADS * IDX_DIM // LANES):
        x = _rope(iq_ref[:, p * LANES:(p + 1) * LANES], ci, si_lo, si_hi, IDX_ROT // 2)
        hi, lo = _hi_lo(x)
        hi_sw = pltpu.roll(hi, IDX_DIM, 1)
        lo_sw = pltpu.roll(lo, IDX_DIM, 1)
        c0 = 2 * p * IDX_XW
        iqx_out[:, c0:c0 + LANES] = jnp.where(low, hi, hi_sw).astype(iqx_out.dtype)
        iqx_out[:, c0 + LANES:c0 + 2 * LANES] = jnp.where(low, lo, 0.0).astype(iqx_out.dtype)
        iqx_out[:, c0 + 2 * LANES:c0 + 3 * LANES] = jnp.where(low, hi_sw, hi).astype(iqx_out.dtype)
        iqx_out[:, c0 + 3 * LANES:c0 + 4 * LANES] = jnp.where(low, lo_sw, 0.0).astype(iqx_out.dtype)
    ik = _rope(sm_ref[...], ci, si_lo, si_hi, IDX_ROT // 2)
    ik_out[...] = ik[:, SM_IK:SM_IK + IDX_DIM]
    hi, lo = _hi_lo(ik)
    ikx_out[:, 0:LANES] = jnp.where(low, hi, pltpu.roll(lo, IDX_DIM, 1)).astype(ikx_out.dtype)
    ikx_out[:, LANES:2 * LANES] = jnp.where(low, hi, 0.0).astype(ikx_out.dtype)


def _rope_tables(pos, rot, width, reps_valid):
    half = rot // 2
    inv_freq = jnp.exp(jnp.arange(half, dtype=_F32) * (-2.0 * math.log(ROPE_THETA) / rot))
    ang = pos.astype(_F32)[:, None] * inv_freq[None, :]
    cos, sin = jnp.cos(ang), jnp.sin(ang)
    m = pos.shape[0]
    one = jnp.ones((m, width - rot), _F32)
    zero = jnp.zeros((m, width - rot), _F32)
    zh = jnp.zeros((m, half), _F32)
    c_head = jnp.concatenate([cos, cos, one], axis=1)
    lo_head = jnp.concatenate([-sin, zh, zero], axis=1)
    hi_head = jnp.concatenate([zh, sin, zero], axis=1)
    reps = LANES // width
    ident = (jnp.ones((m, width), _F32), jnp.zeros((m, width), _F32), jnp.zeros((m, width), _F32))
    out = []
    for t, idt in zip((c_head, lo_head, hi_head), ident):
        out.append(jnp.concatenate([t if r < reps_valid else idt for r in range(reps)], axis=1))
    return jnp.stack(out)


def _rope_all(proj, small, pos):
    m = proj.shape[0]
    tm = _tile(m, 256, SUBLANES)
    ta = _rope_tables(pos, A_ROT, A_HD, 1)
    ti = _rope_tables(pos, IDX_ROT, IDX_DIM, LANES // IDX_DIM)
    wq, wk, wi = A_HEADS * A_HD, A_KV_HEADS * A_HD, IDX_HEADS * IDX_DIM
    wix = IDX_HEADS * IDX_XW
    out_shape = (jax.ShapeDtypeStruct((m, wq), _BF16),
                 jax.ShapeDtypeStruct((m, wk), _F32),
                 jax.ShapeDtypeStruct((m, wk), _BF16),
                 jax.ShapeDtypeStruct((m, wk), _BF16),
                 jax.ShapeDtypeStruct((m, wix), _BF16),
                 jax.ShapeDtypeStruct((m, IDX_DIM), _F32),
                 jax.ShapeDtypeStruct((m, IDX_XW), _BF16))
    return pl.pallas_call(
        _rope_kernel,
        out_shape=out_shape,
        grid=(m // tm,),
        in_specs=[pl.BlockSpec((tm, wq), lambda i: (i, B_AQ // wq)),
                  pl.BlockSpec((tm, wk), lambda i: (i, B_AK // wk)),
                  pl.BlockSpec((tm, wk), lambda i: (i, B_AV // wk)),
                  pl.BlockSpec((tm, wi), lambda i: (i, B_IQ // wi)),
                  pl.BlockSpec((tm, LANES), lambda i: (i, 0)),
                  pl.BlockSpec((3, tm, LANES), lambda i: (0, i, 0)),
                  pl.BlockSpec((3, tm, LANES), lambda i: (0, i, 0))],
        out_specs=(pl.BlockSpec((tm, wq), lambda i: (i, 0)),
                   pl.BlockSpec((tm, wk), lambda i: (i, 0)),
                   pl.BlockSpec((tm, wk), lambda i: (i, 0)),
                   pl.BlockSpec((tm, wk), lambda i: (i, 0)),
                   pl.BlockSpec((tm, wix), lambda i: (i, 0)),
                   pl.BlockSpec((tm, IDX_DIM), lambda i: (i, 0)),
                   pl.BlockSpec((tm, IDX_XW), lambda i: (i, 0))),
        compiler_params=_params(("parallel",)),
        name="rope",
    )(proj, proj, proj, proj, small, ta, ti)


def _attn_kernel(*refs, TQ, TK, past_len, l_valid, topk):
    q_ref, iqx_ref, sm_ref, k_ref, v_ref, ikx_ref = refs[:6]
    o_ref, key_scr, bias_scr, w_scr = refs[-4:]
    i = pl.program_id(1)
    q_start = past_len + i * TQ
    qpos = q_start + lax.broadcasted_iota(jnp.int32, (TQ, 1), 0)
    qchunk = qpos // CHUNK
    last_vis = jnp.minimum(((q_start + TQ - 1) // CHUNK + 1) * CHUNK, l_valid)
    nkt = (last_vis + TK - 1) // TK
    lane_pos = lax.broadcasted_iota(jnp.int32, (1, TK), 1)

    w_scale = (IDX_DIM ** -0.5) * (IDX_HEADS ** -0.5)
    for h in range(IDX_HEADS):
        w_scr[h] = jnp.broadcast_to(sm_ref[:, SM_IW + h:SM_IW + h + 1] * w_scale, (TQ, LANES))
    lane128 = lax.broadcasted_iota(jnp.int32, (1, LANES), 1)
    sc = min(TK, 2 * LANES)

    def score_tile(kt, carry):
        for cc in range(TK // sc):
            k0 = pl.multiple_of(kt * TK + cc * sc, sc)
            ik_c = ikx_ref[pl.ds(k0, sc), :]
            accs = [jnp.zeros((TQ, LANES), _F32) for _ in range(sc // LANES)]
            for h in range(IDX_HEADS):
                isc = _dot_nt(iqx_ref[:, h * IDX_XW:(h + 1) * IDX_XW], ik_c)
                w_h = w_scr[h]
                for c in range(sc // LANES):
                    accs[c] = accs[c] + jnp.maximum(isc[:, c * LANES:(c + 1) * LANES], 0.0) * w_h
            for c in range(sc // LANES):
                bits = pltpu.bitcast(accs[c] + 0.0, jnp.int32)
                key = jnp.where(bits < 0, bits ^ 0x7FFFFFFF, bits)
                kpos = k0 + c * LANES + lane128
                vis = ((kpos // CHUNK) <= qchunk) & (kpos < l_valid)
                col = cc * sc + c * LANES
                key_scr[kt, :, col:col + LANES] = jnp.where(vis, key, INT_MIN)
        return carry

    lax.fori_loop(0, nkt, score_tile, 0)

    def lane_fold(x):
        out = x[:, 0:LANES]
        for c in range(1, TK // LANES):
            out = out + x[:, c * LANES:(c + 1) * LANES]
        return out

    def count(pred_fn):
        def body(kt, part):
            return part + lane_fold(pred_fn(key_scr[kt], kt).astype(jnp.int32))
        part = lax.fori_loop(0, nkt, body, jnp.zeros((TQ, LANES), jnp.int32))
        return jnp.sum(part, axis=1, keepdims=True)

    def bit_step(it, thr_u):
        cand_u = thr_u | lax.shift_left(jnp.int32(1), 31 - it)
        cand_s = cand_u ^ INT_MIN
        cnt = count(lambda key, kt: key >= cand_s)
        return jnp.where(cnt >= topk, cand_u, thr_u)

    thr = lax.fori_loop(0, 32, bit_step, jnp.zeros((TQ, 1), jnp.int32)) ^ INT_MIN
    n_ge = count(lambda key, kt: key >= thr)

    def tie_search(_):
        need = topk - count(lambda key, kt: key > thr)
        nbits = (key_scr.shape[0] * TK).bit_length()

        def pos_step(it, p):
            cand = p | lax.shift_left(jnp.int32(1), nbits - 1 - it)
            cnt = count(lambda key, kt: (key == thr) & ((kt * TK + lane_pos) < cand))
            return jnp.where(cnt < need, cand, p)

        return lax.fori_loop(0, nbits, pos_step, jnp.zeros((TQ, 1), jnp.int32))

    has_tie = jnp.max(n_ge.astype(_F32)) > topk
    p_last = lax.cond(has_tie, tie_search, lambda _: jnp.full((TQ, 1), 2 ** 31 - 1, jnp.int32), 0)

    def bias_tile(kt, carry):
        key = key_scr[kt]
        sel = ((key > thr) | ((key == thr) & ((kt * TK + lane_pos) <= p_last))) & (key > KEY_NEG_INF)
        bias_scr[kt] = jnp.where(sel, 0.0, NEG_BIG)
        return carry

    lax.fori_loop(0, nkt, bias_tile, 0)

    scale = (A_HD ** -0.5) * math.log2(math.e)
    rows = A_GROUP * TQ
    n_par = 1

    def q_group(g):
        return jnp.concatenate(
            [q_ref[:, (g * A_GROUP + j) * A_HD:(g * A_GROUP + j + 1) * A_HD] for j in range(A_GROUP)], axis=0)

    for g0 in range(0, A_KV_HEADS, n_par):
        qgs = [q_group(g0 + u) for u in range(n_par)]

        def attn_tile(kt, carry, g0=g0, qgs=qgs):
            k0 = pl.multiple_of(kt * TK, TK)
            bias = bias_scr[kt][None]
            s_u = [_dot_nt(qgs[u], k_ref[pl.ds(k0, TK), (g0 + u) * A_HD:(g0 + u + 1) * A_HD])
                   .reshape(A_GROUP, TQ, TK) * scale + bias for u in range(n_par)]
            new = []
            p_u = []
            for u in range(n_par):
                m_i, l_i, acc = carry[u]
                m_new = jnp.maximum(m_i, jnp.max(s_u[u], axis=-1, keepdims=True))
                alpha = jnp.exp2(m_i - m_new)
                p = jnp.exp2(s_u[u] - m_new)
                p_u.append(p.reshape(rows, TK).astype(_BF16))
                new.append((m_new, alpha * l_i + jnp.sum(p, axis=-1, keepdims=True), alpha * acc))
            out = []
            for u in range(n_par):
                v_t = v_ref[pl.ds(k0, TK), (g0 + u) * A_HD:(g0 + u + 1) * A_HD]
                m_new, l_new, acc = new[u]
                out.append((m_new, l_new, acc + _dot(p_u[u], v_t).reshape(A_GROUP, TQ, A_HD)))
            return tuple(out)

        init = tuple((jnp.full((A_GROUP, TQ, 1), NEG_BIG, _F32), jnp.zeros((A_GROUP, TQ, 1), _F32),
                      jnp.zeros((A_GROUP, TQ, A_HD), _F32)) for _ in range(n_par))
        fin = lax.fori_loop(0, nkt, attn_tile, init)
        for u in range(n_par):
            _, l_f, acc_f = fin[u]
            out = acc_f / l_f
            for j in range(A_GROUP):
                hh = (g0 + u) * A_GROUP + j
                o_ref[:, hh * A_HD:(hh + 1) * A_HD] = out[j].astype(o_ref.dtype)


def _attention(q_rot, iq_rot, small, k_all, v_all, ik_all, row0, nseq, T, TQ, TK, lp, past_len, l_valid, o_prev=None):
    nq = T // TQ
    qb0 = row0 // TQ
    topk = min(TOPK_MAX, l_valid // 4)
    wq, wk, wix = A_HEADS * A_HD, A_KV_HEADS * A_HD, IDX_HEADS * IDX_XW
    in_specs = [pl.BlockSpec((TQ, wq), lambda b, i: (qb0 + b * nq + i, 0)),
                pl.BlockSpec((TQ, wix), lambda b, i: (qb0 + b * nq + i, 0)),
                pl.BlockSpec((TQ, LANES), lambda b, i: (qb0 + b * nq + i, 0)),
                pl.BlockSpec((lp, wk), lambda b, i: (b, 0)),
                pl.BlockSpec((lp, wk), lambda b, i: (b, 0)),
                pl.BlockSpec((lp, IDX_XW), lambda b, i: (b, 0))]
    args = [q_rot, iq_rot, small, k_all, v_all, ik_all]
    aliases = {}
    if o_prev is not None:
        aliases = {len(args): 0}
        in_specs.append(pl.BlockSpec(memory_space=pl.ANY))
        args.append(o_prev)
    return pl.pallas_call(
        functools.partial(_attn_kernel, TQ=TQ, TK=TK, past_len=past_len, l_valid=l_valid, topk=topk),
        out_shape=jax.ShapeDtypeStruct((q_rot.shape[0], wq), _BF16),
        grid=(nseq, nq),
        in_specs=in_specs,
        out_specs=pl.BlockSpec((TQ, wq), lambda b, i: (qb0 + b * nq + i, 0)),
        scratch_shapes=[pltpu.VMEM((lp // TK, TQ, TK), jnp.int32),
                        pltpu.VMEM((lp // TK, TQ, TK), _F32),
                        pltpu.VMEM((IDX_HEADS, TQ, LANES), _F32)],
        input_output_aliases=aliases,
        compiler_params=_params(("parallel", "arbitrary")),
        name="sparse_attention",
    )(*args)


def _merge_kernel(hm_ref, ha_ref, wm_ref, wa_ref, gm_ref, ga_ref, o_ref):
    pm = _dot(hm_ref[...], wm_ref[...])
    pa = _dot(ha_ref[...], wa_ref[...])
    o_ref[...] = (jax.nn.sigmoid(gm_ref[...]) * pm + jax.nn.sigmoid(ga_ref[...]) * pa).astype(o_ref.dtype)


def _merge(hm, ha, wm, wa, proj, d):
    m = hm.shape[0]
    tm = _tile(m, 512, SUBLANES)
    tn = _tile(d, 1024)
    gm0 = 0
    ga0 = d // tn
    km, ka = hm.shape[1], ha.shape[1]
    return pl.pallas_call(
        _merge_kernel,
        out_shape=jax.ShapeDtypeStruct((m, d), _BF16),
        grid=(d // tn, m // tm),
        in_specs=[pl.BlockSpec((tm, km), lambda j, i: (i, 0)),
                  pl.BlockSpec((tm, ka), lambda j, i: (i, 0)),
                  pl.BlockSpec((km, tn), lambda j, i: (0, j)),
                  pl.BlockSpec((ka, tn), lambda j, i: (0, j)),
                  pl.BlockSpec((tm, tn), lambda j, i: (i, gm0 + j)),
                  pl.BlockSpec((tm, tn), lambda j, i: (i, ga0 + j))],
        out_specs=pl.BlockSpec((tm, tn), lambda j, i: (i, j)),
        compiler_params=_params(("parallel", "parallel")),
        name="gated_merge",
    )(hm, ha, wm, wa, proj, proj)


def _resnorm_next_kernel(xa_ref, xb_ref, y_ref, g_ref, g2_ref, x1_ref, h_ref, *, n_first):
    def body(x_ref):
        y = y_ref[...]
        ms = jnp.mean(y * y, axis=-1, keepdims=True)
        x1 = x_ref[...] + y * lax.rsqrt(ms + NORM_EPS) * g_ref[...]
        x1_ref[...] = x1
        ms1 = jnp.mean(x1 * x1, axis=-1, keepdims=True)
        h_ref[...] = (x1 * lax.rsqrt(ms1 + NORM_EPS) * g2_ref[...]).astype(h_ref.dtype)

    i = pl.program_id(0)
    pl.when(i < n_first)(lambda: body(xa_ref))
    pl.when(i >= n_first)(lambda: body(xb_ref))


def _resnorm_next(xa, xb, y, g, g_next):
    m, d = y.shape
    tm = _row_tile(xa.shape[0], xb.shape[0])
    n_first = xa.shape[0] // tm
    row = pl.BlockSpec((tm, d), lambda i: (i, 0))
    vec = pl.BlockSpec((1, d), lambda i: (0, 0))
    return pl.pallas_call(
        functools.partial(_resnorm_next_kernel, n_first=n_first),
        out_shape=(jax.ShapeDtypeStruct((m, d), _F32), jax.ShapeDtypeStruct((m, d), _BF16)),
        grid=(m // tm,), in_specs=[*_two_source_specs(tm, d, n_first), row, vec, vec], out_specs=(row, row),
        compiler_params=_params(("parallel",)), name="resnorm_next",
    )(xa, xb, y, g.reshape(1, d), g_next.reshape(1, d))


def _resnorm_kernel(x_ref, y_ref, g_ref, oa_ref, ob_ref, *, n_first):
    y = y_ref[...]
    ms = jnp.mean(y * y, axis=-1, keepdims=True)
    out = x_ref[...] + y * lax.rsqrt(ms + NORM_EPS) * g_ref[...]
    i = pl.program_id(0)

    @pl.when(i < n_first)
    def _():
        oa_ref[...] = out

    @pl.when(i >= n_first)
    def _():
        ob_ref[...] = out


def _resnorm_split(x, y, g, m_a):
    m, d = x.shape
    m_b = m - m_a
    tm = _row_tile(m_a, m_b)
    n_first = m_a // tm
    row = pl.BlockSpec((tm, d), lambda i: (i, 0))
    vec = pl.BlockSpec((1, d), lambda i: (0, 0))
    return pl.pallas_call(
        functools.partial(_resnorm_kernel, n_first=n_first),
        out_shape=(jax.ShapeDtypeStruct((m_a, d), _F32), jax.ShapeDtypeStruct((m_b, d), _F32)),
        grid=(m // tm,), in_specs=[row, row, vec], out_specs=_two_source_specs(tm, d, n_first),
        compiler_params=_params(("arbitrary",)), name="resnorm",
    )(x, y, g.reshape(1, d))


HALO = SUBLANES


def _conv_gelu_gate(cw_ref, cb_ref, prev2, prev1, gate, lin):
    gc = cw_ref[0:1, :] * prev2 + cw_ref[1:2, :] * prev1 + cw_ref[2:3, :] * gate + cb_ref[...]
    c = -2.0 * math.sqrt(2.0 / math.pi)
    act = gc / (1.0 + jnp.exp(gc * (c + (c * 0.044715) * (gc * gc))))
    return act * lin


def _ffn_up_first_kernel(h_ref, wg_ref, wl_ref, cw_ref, cb_ref, z_ref, tail_ref, wg_bf, wl_bf, g_scr,
                         *, tm, seq_len):
    i = pl.program_id(1)
    tn = z_ref.shape[1]

    @pl.when(i == 0)
    def _():
        wg_bf[...] = wg_ref[...].astype(wg_bf.dtype)
        wl_bf[...] = wl_ref[...].astype(wl_bf.dtype)

    @pl.when((i * tm) % seq_len == 0)
    def _():
        g_scr[0:HALO, :] = jnp.zeros((HALO, tn), _F32)

    h = h_ref[...]
    gate = _dot(h, wg_bf[...])
    lin = _dot(h, wl_bf[...])
    g_scr[HALO:HALO + tm, :] = gate
    zed = _conv_gelu_gate(cw_ref, cb_ref, g_scr[HALO - 2:HALO - 2 + tm, :], g_scr[HALO - 1:HALO - 1 + tm, :],
                          gate, lin)
    z_ref[...] = zed.astype(z_ref.dtype)
    g_scr[0:HALO, :] = gate[tm - HALO:tm, :]
    tail_ref[...] = gate[tm - SUBLANES:tm, :]


def _ffn_up_state_kernel(h_ref, wg_ref, wl_ref, cw_ref, cb_ref, init_ref, z_any, z_ref, tail_ref, g_scr,
                         *, tm, seq_len):
    tn = z_ref.shape[1]
    h = h_ref[...]
    gate = _dot(h, wg_ref[...].astype(_BF16))
    lin = _dot(h, wl_ref[...].astype(_BF16))
    g_scr[0:HALO, :] = jnp.zeros((HALO, tn), _F32)
    g_scr[HALO:HALO + tm, :] = gate
    prev1 = g_scr[HALO - 1:HALO - 1 + tm, :]
    prev2 = g_scr[HALO - 2:HALO - 2 + tm, :]
    nseg = tm // seq_len
    init0 = jnp.concatenate([jnp.broadcast_to(init_ref[s, 0:1, :], (seq_len, tn)) for s in range(nseg)], axis=0)
    init1 = jnp.concatenate([jnp.broadcast_to(init_ref[s, 1:2, :], (seq_len, tn)) for s in range(nseg)], axis=0)
    t = lax.broadcasted_iota(jnp.int32, (tm, 1), 0) % seq_len
    prev1 = jnp.where(t == 0, init1, prev1)
    prev2 = jnp.where(t == 0, init0, jnp.where(t == 1, init1, prev2))
    z_ref[...] = _conv_gelu_gate(cw_ref, cb_ref, prev2, prev1, gate, lin).astype(z_ref.dtype)
    for s in range(nseg):
        tail_ref[s * SUBLANES:(s + 1) * SUBLANES, :] = gate[(s + 1) * seq_len - SUBLANES:(s + 1) * seq_len, :]


def _conv_state(tails, nseq, segs_per_seq, dff):
    tails = tails.reshape(nseq * segs_per_seq, SUBLANES, dff)
    last = tails[segs_per_seq - 1::segs_per_seq, SUBLANES - (CONV_W - 1):, :]
    return last.reshape(nseq, CONV_W - 1, dff)


def _ffn_up_specs(d, tn, nj, h_spec):
    return [h_spec,
            pl.BlockSpec((d, tn), lambda j, i: (0, j)),
            pl.BlockSpec((d, tn), lambda j, i: (0, nj + j)),
            pl.BlockSpec((CONV_W, tn), lambda j, i: (0, j)),
            pl.BlockSpec((1, tn), lambda j, i: (0, j))]


def _ffn_up_first(hf, w_up, conv_w, conv_b, nrows, seq_len):
    m_all, d = hf.shape
    dff = w_up.shape[1] // 2
    tm = _tile(seq_len, 512, SUBLANES)
    tn = _tile(dff, 256)
    nj = dff // tn
    nt = nrows // tm
    specs = _ffn_up_specs(d, tn, nj, pl.BlockSpec((tm, d), lambda j, i: (i, 0)))
    z, tails = pl.pallas_call(
        functools.partial(_ffn_up_first_kernel, tm=tm, seq_len=seq_len),
        out_shape=(jax.ShapeDtypeStruct((m_all, dff), _BF16),
                   jax.ShapeDtypeStruct((nt * SUBLANES, dff), _F32)),
        grid=(nj, nt),
        in_specs=specs,
        out_specs=(pl.BlockSpec((tm, tn), lambda j, i: (i, j)),
                   pl.BlockSpec((SUBLANES, tn), lambda j, i: (i, j))),
        scratch_shapes=[pltpu.VMEM((d, tn), _BF16), pltpu.VMEM((d, tn), _BF16),
                        pltpu.VMEM((HALO + tm, tn), _F32)],
        compiler_params=_params(("parallel", "arbitrary")),
        name="ffn_up_conv_first",
    )(hf, w_up, w_up, conv_w, conv_b.reshape(1, dff))
    return z, _conv_state(tails, nrows // seq_len, seq_len // tm, dff)


def _ffn_up_state(hf, w_up, conv_w, conv_b, row0, nrows, seq_len, init, z_prev):
    m_all, d = hf.shape
    dff = w_up.shape[1] // 2
    tm = _tile(nrows, 512, seq_len)
    tn = _tile(dff, 256)
    nj = dff // tn
    nseg = tm // seq_len
    rb0 = row0 // tm
    specs = _ffn_up_specs(d, tn, nj, pl.BlockSpec((tm, d), lambda j, i: (rb0 + i, 0)))
    specs += [pl.BlockSpec((nseg, CONV_W - 1, tn), lambda j, i: (i, 0, j)), pl.BlockSpec(memory_space=pl.ANY)]
    z, tails = pl.pallas_call(
        functools.partial(_ffn_up_state_kernel, tm=tm, seq_len=seq_len),
        out_shape=(jax.ShapeDtypeStruct((m_all, dff), _BF16),
                   jax.ShapeDtypeStruct((nrows // seq_len * SUBLANES, dff), _F32)),
        grid=(nj, nrows // tm),
        in_specs=specs,
        out_specs=(pl.BlockSpec((tm, tn), lambda j, i: (rb0 + i, j)),
                   pl.BlockSpec((nseg * SUBLANES, tn), lambda j, i: (i, j))),
        scratch_shapes=[pltpu.VMEM((HALO + tm, tn), _F32)],
        input_output_aliases={6: 0},
        compiler_params=_params(("parallel", "arbitrary")),
        name="ffn_up_conv_state",
    )(hf, w_up, w_up, conv_w, conv_b.reshape(1, dff), init, z_prev)
    return z, _conv_state(tails, nrows // seq_len, 1, dff)


def _layer(xa, xb, geom, cache, state, w):
    (B, S, Bd, Td, P) = geom
    (cache_k, cache_v, cache_ik) = cache
    (state_C, state_n, state_m, state_conv) = state
    (w_in, b_igate, b_fgate, g_mhnorm, w_proj_m, w_proj_a, w_out,
     g_pre_mix, g_post_mix, g_pre_ffn, g_post_ffn, w_up, conv_w, conv_b, w_down) = w
    mp, d = xa.shape
    dff = w_down.shape[0]
    wk = A_KV_HEADS * A_HD

    c_mi = A_END
    c_aq = c_mi + 2 * M_HEADS
    c_ik = c_aq + B_END
    c_iw = c_ik + IDX_DIM
    c_gm = c_iw + IDX_HEADS
    w_t = jnp.swapaxes(w_in, 0, 1).astype(_F32)
    w_small_t = jnp.concatenate([w_t[c_ik:c_iw], w_t[c_mi:c_aq], w_t[c_iw:c_gm],
                                 jnp.zeros((LANES - SM_END, d), _F32)], axis=0)
    gate_bias = jnp.concatenate([jnp.zeros((SM_MI,), _F32), b_igate.astype(_F32), b_fgate.astype(_F32),
                                 jnp.zeros((LANES - SM_IW,), _F32)]).reshape(1, LANES)

    hn = _rmsnorm_cast(xa, xb, g_pre_mix)
    proj_a = _matmul_w32(hn, w_t, 0, A_END, _F32, transposed=True, name="in_proj_mlstm")
    proj_b = _matmul_w32(hn, w_t, c_aq, B_END, _F32, transposed=True, name="in_proj_attn")
    proj_c = _matmul_w32(hn, w_t, c_gm, 2 * d, _F32, transposed=True, name="in_proj_gates")
    small = _matmul_w32(hn, w_small_t, 0, LANES, _F32, transposed=True, name="in_proj_small")

    lp_chunk = _tile(S, 256, CHUNK)
    hm_p, c_p, n_p, m_p = _mlstm(proj_a, small, gate_bias, g_mhnorm, 0, B, S, lp_chunk, None)
    s0 = jnp.zeros((Bd * M_HEADS, SUBLANES, M_QK), _F32)
    s0 = s0.at[:, 0, :].set(state_n.reshape(Bd * M_HEADS, M_QK).astype(_F32))
    s0 = s0.at[:, 1, :].set(jnp.broadcast_to(state_m.reshape(Bd * M_HEADS, 1).astype(_F32), (Bd * M_HEADS, M_QK)))
    hm, c_s, n_s, m_s = _mlstm(proj_a, small, gate_bias, g_mhnorm, mp, Bd, Td, min(CHUNK, Td),
                               (state_C.astype(_F32), s0), hm_p)

    pos = jnp.concatenate([jnp.tile(jnp.arange(S, dtype=jnp.int32), B),
                           jnp.tile(P + jnp.arange(Td, dtype=jnp.int32), Bd)])
    q_rot, k_rot, k_bf, v_bf, iqx, ik_rot, ikx = _rope_all(proj_b, small, pos)
    tq_p = _tile(S, 256, CHUNK)
    tk_p = _tile(S, 512)
    ha_p = _attention(q_rot, iqx, small, k_bf, v_bf, ikx, 0, B, S, tq_p, tk_p, S, 0, S)
    l_s = P + Td
    tk_s = 256
    lp_s = -(-l_s // tk_s) * tk_s
    pad = lp_s - l_s
    k_s = jnp.concatenate([cache_k.reshape(Bd, P, wk).astype(_BF16), k_bf[mp:].reshape(Bd, Td, wk),
                           jnp.zeros((Bd, pad, wk), _BF16)], axis=1).reshape(Bd * lp_s, wk)
    v_s = jnp.concatenate([cache_v.reshape(Bd, P, wk).astype(_BF16), v_bf[mp:].reshape(Bd, Td, wk),
                           jnp.zeros((Bd, pad, wk), _BF16)], axis=1).reshape(Bd * lp_s, wk)
    cik = cache_ik.astype(_F32)
    cik_hi = cik.astype(_BF16)
    cik_lo = (cik - cik_hi.astype(_F32)).astype(_BF16)
    cikx = jnp.concatenate([cik_hi, cik_lo, cik_hi, jnp.zeros_like(cik_hi)], axis=-1)
    ikx_s = jnp.concatenate([cikx, ikx[mp:].reshape(Bd, Td, IDX_XW),
                             jnp.zeros((Bd, pad, IDX_XW), _BF16)], axis=1).reshape(Bd * lp_s, IDX_XW)
    ha = _attention(q_rot, iqx, small, k_s, v_s, ikx_s, mp, Bd, Td, Td, tk_s, lp_s, P, l_s, ha_p)

    mix = _merge(hm, ha, w_proj_m.astype(_BF16), w_proj_a.astype(_BF16), proj_c, d)
    y1 = _matmul_w32(mix, w_out.astype(_F32), 0, d, _F32, transposed=False, name="out_proj")
    x1, hf = _resnorm_next(xa, xb, y1, g_post_mix, g_pre_ffn)

    w_up = w_up.astype(_F32)
    z_p, conv_p = _ffn_up_first(hf, w_up, conv_w, conv_b, mp, S)
    z, conv_s = _ffn_up_state(hf, w_up, conv_w, conv_b, mp, Bd * Td, Td, state_conv.astype(_F32), z_p)
    y2 = _matmul(z, w_down.astype(_BF16), _F32, tm=512, tn=1024, tk=dff // 2, name="down_proj")
    x2 = _resnorm_split(x1, y2, g_post_ffn, mp)

    av = proj_b[:, B_AV:B_AV + wk]
    outs_p = (k_rot[:mp].reshape(B, S, A_KV_HEADS, A_HD), av[:mp].reshape(B, S, A_KV_HEADS, A_HD),
              ik_rot[:mp].reshape(B, S, IDX_DIM), c_p, n_p, m_p, conv_p)
    outs_s = (k_rot[mp:].reshape(Bd, Td, A_KV_HEADS, A_HD), av[mp:].reshape(Bd, Td, A_KV_HEADS, A_HD),
              ik_rot[mp:].reshape(Bd, Td, IDX_DIM), c_s, n_s, m_s, conv_s)
    return x2, outs_p, outs_s


def kernel(x_prompt, x_sample, cache_k, cache_v, cache_idx_k, state_C, state_n, state_m, state_conv,
           w_in, b_igate, b_fgate, g_mhnorm, w_proj_m, w_proj_a, w_out,
           g_pre_mix, g_post_mix, g_pre_ffn, g_post_ffn, w_up, conv_w, conv_b, w_down):
    B, S, d = x_prompt.shape
    Bd, Td, _ = x_sample.shape
    P = cache_k.shape[2]
    depth = w_in.shape[0]
    mp = B * S
    xa, xb = x_prompt.reshape(mp, d), x_sample.reshape(Bd * Td, d)
    all_p, all_s = [], []
    for l in range(depth):
        w = (w_in[l], b_igate[l], b_fgate[l], g_mhnorm[l], w_proj_m[l], w_proj_a[l], w_out[l],
             g_pre_mix[l], g_post_mix[l], g_pre_ffn[l], g_post_ffn[l], w_up[l], conv_w[l], conv_b[l], w_down[l])
        (xa, xb), outs_p, outs_s = _layer(xa, xb, (B, S, Bd, Td, P), (cache_k[l], cache_v[l], cache_idx_k[l]),
                                          (state_C[l], state_n[l], state_m[l], state_conv[l]), w)
        all_p.append(outs_p)
        all_s.append(outs_s)

    def stk(outs, i):
        return jnp.stack([o[i] for o in outs])

    yp = xa.reshape(B, S, d)
    ys = xb.reshape(Bd, Td, d)
    return (yp, ys) + tuple(stk(all_p, i) for i in range(7)) + tuple(stk(all_s, i) for i in range(7))
```

```python
import functools
import math

import jax
import jax.numpy as jnp
from jax import lax
from jax.experimental import pallas as pl
from jax.experimental.pallas import tpu as pltpu

CHUNK = 64
NORM_EPS = 1e-6
ROPE_THETA = 500000.0
M_HEADS = 8
M_QK = 128
M_V = 256
A_HEADS = 16
A_KV_HEADS = 4
A_GROUP = A_HEADS // A_KV_HEADS
A_HD = 128
A_ROT = A_HD // 4
IDX_HEADS = 16
IDX_DIM = 64
IDX_ROT = IDX_DIM // 4
TOPK_MAX = 256
CONV_W = 3

LANES = 128
SUBLANES = 8
VMEM_LIMIT_BYTES = 52 * 1024 * 1024

A_MQ = 0
A_MK = A_MQ + M_HEADS * M_QK
A_MV = A_MK + M_HEADS * M_QK
A_MO = A_MV + M_HEADS * M_V
A_END = A_MO + M_HEADS * M_V
B_AQ = 0
B_AK = B_AQ + A_HEADS * A_HD
B_AV = B_AK + A_KV_HEADS * A_HD
B_IQ = B_AV + A_KV_HEADS * A_HD
B_END = B_IQ + IDX_HEADS * IDX_DIM
SM_IK = 0
SM_MI = SM_IK + IDX_DIM
SM_MF = SM_MI + M_HEADS
SM_IW = SM_MF + M_HEADS
SM_END = SM_IW + IDX_HEADS
IDX_XW = 4 * IDX_DIM

INT_MIN = -2 ** 31
KEY_NEG_INF = INT_MIN + 0x7FFFFF
NEG_BIG = -1e30

_BF16 = jnp.bfloat16
_F32 = jnp.float32


def _tile(dim, target, quantum=LANES):
    if dim <= target:
        return dim
    t = (target // quantum) * quantum
    while t >= quantum:
        if dim % t == 0:
            return t
        t -= quantum
    return dim


def _params(sem):
    return pltpu.CompilerParams(dimension_semantics=sem, vmem_limit_bytes=VMEM_LIMIT_BYTES)


def _dot(a, b):
    return jnp.dot(a, b, preferred_element_type=_F32)


def _dot_nt(a, b):
    return lax.dot_general(a, b, (((1,), (1,)), ((), ())), preferred_element_type=_F32)


def _dot_tn(a, b):
    return lax.dot_general(a, b, (((0,), (0,)), ((), ())), preferred_element_type=_F32)


def _mm_w32_kernel(a_ref, w_hbm, o_ref, w_f32, w_bf, sem, *, off0, nj, transposed):
    j = pl.program_id(0)
    kd, tn = w_bf.shape

    def tile_copy(jj):
        if transposed:
            src = w_hbm.at[pl.ds(pl.multiple_of(off0 + jj * tn, SUBLANES), tn), :]
        else:
            src = w_hbm.at[:, pl.ds(pl.multiple_of(off0 + jj * tn, LANES), tn)]
        return pltpu.make_async_copy(src, w_f32, sem)

    @pl.when(pl.program_id(1) == 0)
    def _():
        @pl.when(j == 0)
        def _():
            tile_copy(0).start()

        tile_copy(j).wait()
        if transposed:
            sq = math.gcd(tn, kd, 512)
            for r in range(0, tn, sq):
                for c in range(0, kd, sq):
                    w_bf[c:c + sq, r:r + sq] = w_f32[r:r + sq, c:c + sq].T.astype(w_bf.dtype)
        else:
            w_bf[...] = w_f32[...].astype(w_bf.dtype)

        @pl.when(j + 1 < nj)
        def _():
            tile_copy(j + 1).start()

    o_ref[...] = _dot(a_ref[...], w_bf[...]).astype(o_ref.dtype)


def _matmul_w32(a, w, off0, n, out_dtype, *, transposed, tm=512, tn=1024, name="matmul_w32"):
    m, kd = a.shape
    tm = _tile(m, tm, SUBLANES)
    tn = _tile(n, tn)
    nj = n // tn
    return pl.pallas_call(
        functools.partial(_mm_w32_kernel, off0=off0, nj=nj, transposed=transposed),
        out_shape=jax.ShapeDtypeStruct((m, n), out_dtype),
        grid=(nj, m // tm),
        in_specs=[pl.BlockSpec((tm, kd), lambda j, i: (i, 0)),
                  pl.BlockSpec(memory_space=pl.ANY)],
        out_specs=pl.BlockSpec((tm, tn), lambda j, i: (i, j)),
        scratch_shapes=[pltpu.VMEM((tn, kd) if transposed else (kd, tn), _F32), pltpu.VMEM((kd, tn), _BF16),
                        pltpu.SemaphoreType.DMA],
        compiler_params=_params(("arbitrary", "arbitrary")),
        name=name,
    )(a, w)


def _two_source_specs(tm, d, n_first):
    return (pl.BlockSpec((tm, d), lambda i: (jnp.minimum(i, n_first - 1), 0)),
            pl.BlockSpec((tm, d), lambda i: (jnp.maximum(i - n_first, 0), 0)))


def _rmsnorm_kernel(xa_ref, xb_ref, g_ref, o_ref, *, n_first):
    def body(x_ref):
        x = x_ref[...]
        ms = jnp.mean(x * x, axis=-1, keepdims=True)
        o_ref[...] = (x * lax.rsqrt(ms + NORM_EPS) * g_ref[...]).astype(o_ref.dtype)

    i = pl.program_id(0)
    pl.when(i < n_first)(lambda: body(xa_ref))
    pl.when(i >= n_first)(lambda: body(xb_ref))


def _row_tile(m_a, m_b):
    return _tile(math.gcd(m_a, m_b), 256, SUBLANES)


def _rmsnorm_cast(xa, xb, g):
    (m_a, d), m_b = xa.shape, xb.shape[0]
    tm = _row_tile(m_a, m_b)
    n_first = m_a // tm
    return pl.pallas_call(
        functools.partial(_rmsnorm_kernel, n_first=n_first),
        out_shape=jax.ShapeDtypeStruct((m_a + m_b, d), _BF16),
        grid=((m_a + m_b) // tm,),
        in_specs=[*_two_source_specs(tm, d, n_first), pl.BlockSpec((1, d), lambda i: (0, 0))],
        out_specs=pl.BlockSpec((tm, d), lambda i: (i, 0)),
        compiler_params=_params(("parallel",)),
        name="rmsnorm_cast",
    )(xa, xb, g.reshape(1, d))


def _mm_kernel(a_ref, b_ref, o_ref):
    o_ref[...] = _dot(a_ref[...], b_ref[...]).astype(o_ref.dtype)


def _matmul(a, b, out_dtype, *, tm=256, tn=1024, name="matmul"):
    m, kd = a.shape
    _, n = b.shape
    tm = _tile(m, tm, SUBLANES)
    tn = _tile(n, tn)
    return pl.pallas_call(
        _mm_kernel,
        out_shape=jax.ShapeDtypeStruct((m, n), out_dtype),
        grid=(n // tn, m // tm),
        in_specs=[pl.BlockSpec((tm, kd), lambda j, i: (i, 0)),
                  pl.BlockSpec((kd, tn), lambda j, i: (0, j), pipeline_mode=pl.Buffered(1))],
        out_specs=pl.BlockSpec((tm, tn), lambda j, i: (i, j)),
        compiler_params=_params(("parallel", "parallel")),
        name=name,
    )(a, b)


def _mlstm_kernel(*refs, L, has_state, has_alias):
    q_ref, k_ref, v_ref, o_ref, g_ref, gb_ref, gn_ref = refs[:7]
    if has_state:
        c0_ref, s0_ref = refs[7:9]
    h_ref, c_out_ref, s_out_ref, c_scr, n_scr, m_scr = refs[7 + 2 * has_state + has_alias:]
    head = pl.program_id(1)
    c = pl.program_id(2)
    nc = pl.num_programs(2)

    @pl.when(c == 0)
    def _():
        if has_state:
            c_scr[...] = c0_ref[0, 0]
            n_scr[...] = s0_ref[0, 0:1, :]
            m_scr[...] = s0_ref[0, 1:2, :]
        else:
            c_scr[...] = jnp.zeros_like(c_scr)
            n_scr[...] = jnp.zeros_like(n_scr)
            m_scr[...] = jnp.zeros_like(m_scr)

    gates = g_ref[...] + gb_ref[...]
    lane = lax.broadcasted_iota(jnp.int32, gates.shape, 1)
    ig_col = jnp.sum(jnp.where(lane == SM_MI + head, gates, 0.0), axis=1, keepdims=True)
    mf_col = jnp.sum(jnp.where(lane == SM_MF + head, gates, 0.0), axis=1, keepdims=True)
    lf_col = jnp.minimum(mf_col, 0.0) - jnp.log1p(jnp.exp(-jnp.abs(mf_col)))

    ri = lax.broadcasted_iota(jnp.int32, (L, L), 0)
    ci = lax.broadcasted_iota(jnp.int32, (L, L), 1)
    eye = ri == ci
    tril = ci <= ri
    lf_row = jnp.sum(jnp.where(eye, lf_col, 0.0), axis=0, keepdims=True)
    ig_row = jnp.sum(jnp.where(eye, ig_col, 0.0), axis=0, keepdims=True)
    b_col = jnp.sum(jnp.where(tril, lf_row, 0.0), axis=1, keepdims=True)
    b_row = jnp.sum(jnp.where(ri <= ci, lf_col, 0.0), axis=0, keepdims=True)
    logw = jnp.where(tril, b_col - b_row + ig_row, -jnp.inf)

    m_prev = m_scr[:, 0:1]
    inter = b_col + m_prev
    m_t = jnp.maximum(inter, jnp.max(logw, axis=1, keepdims=True))
    a = jnp.exp(inter - m_t)
    sw = jnp.exp(logw - m_t)

    q = q_ref[...]
    k = k_ref[...] * (M_QK ** -0.5)
    v = v_ref[...]
    qb = q.astype(_BF16)
    kb = k.astype(_BF16)
    s = _dot_nt(qb, kb) * sw
    c_old = c_scr[...]
    n_old = n_scr[...]
    num = a * _dot_nt(qb, c_old.astype(_BF16)) + _dot(s.astype(_BF16), v.astype(_BF16))
    den = a * jnp.sum(q * n_old, axis=1, keepdims=True) + jnp.sum(s, axis=1, keepdims=True)
    h = num / jnp.maximum(jnp.abs(den), jnp.exp(-m_t))

    m_new = m_t[L - 1:L, :]
    b_last = b_col[L - 1:L, :]
    g_col = jnp.exp(b_last - b_col + ig_col - m_new)
    decay = jnp.exp(b_last + m_prev - m_new)
    c_new = decay * c_old + _dot_tn((v * g_col).astype(_BF16), kb)
    n_new = decay * n_old + jnp.sum(g_col * k, axis=0, keepdims=True)
    c_scr[...] = c_new
    n_scr[...] = n_new
    m_scr[...] = jnp.broadcast_to(m_new, m_scr.shape)

    ms = jnp.mean(h * h, axis=1, keepdims=True)
    y = h * lax.rsqrt(ms + NORM_EPS) * gn_ref[0]
    h_ref[...] = (y * jax.nn.sigmoid(o_ref[...])).astype(h_ref.dtype)

    @pl.when(c == nc - 1)
    def _():
        c_out_ref[0, 0] = c_new
        s_out_ref[0] = jnp.zeros(s_out_ref.shape[1:], _F32)
        s_out_ref[0, 0:1, :] = n_new
        s_out_ref[0, 1:2, :] = jnp.broadcast_to(m_new, (1, M_QK))


def _mlstm(proj, small, gate_bias, g_mhnorm, row0, nseq, T, L, state, h_prev=None):
    nc = T // L
    rb0 = row0 // L
    has_state = state is not None
    has_alias = h_prev is not None

    def rows(b, h, c):
        return rb0 + b * nc + c

    in_specs = [
        pl.BlockSpec((L, M_QK), lambda b, h, c: (rows(b, h, c), A_MQ // M_QK + h)),
        pl.BlockSpec((L, M_QK), lambda b, h, c: (rows(b, h, c), A_MK // M_QK + h)),
        pl.BlockSpec((L, M_V), lambda b, h, c: (rows(b, h, c), A_MV // M_V + h)),
        pl.BlockSpec((L, M_V), lambda b, h, c: (rows(b, h, c), A_MO // M_V + h)),
        pl.BlockSpec((L, LANES), lambda b, h, c: (rows(b, h, c), 0)),
        pl.BlockSpec((1, LANES), lambda b, h, c: (0, 0)),
        pl.BlockSpec((1, 1, M_V), lambda b, h, c: (h, 0, 0)),
    ]
    args = [proj, proj, proj, proj, small, gate_bias, g_mhnorm.reshape(M_HEADS, 1, M_V)]
    if has_state:
        c0, s0 = state
        in_specs += [pl.BlockSpec((1, 1, M_V, M_QK), lambda b, h, c: (b, h, 0, 0)),
                     pl.BlockSpec((1, SUBLANES, M_QK), lambda b, h, c: (b * M_HEADS + h, 0, 0))]
        args += [c0, s0]
    aliases = {}
    if has_alias:
        aliases = {len(args): 0}
        in_specs.append(pl.BlockSpec(memory_space=pl.ANY))
        args.append(h_prev)
    out_shape = (jax.ShapeDtypeStruct((proj.shape[0], M_HEADS * M_V), _BF16),
                 jax.ShapeDtypeStruct((nseq, M_HEADS, M_V, M_QK), _F32),
                 jax.ShapeDtypeStruct((nseq * M_HEADS, SUBLANES, M_QK), _F32))
    out_specs = (pl.BlockSpec((L, M_V), lambda b, h, c: (rows(b, h, c), h)),
                 pl.BlockSpec((1, 1, M_V, M_QK), lambda b, h, c: (b, h, 0, 0)),
                 pl.BlockSpec((1, SUBLANES, M_QK), lambda b, h, c: (b * M_HEADS + h, 0, 0)))
    hm, c_new, stats = pl.pallas_call(
        functools.partial(_mlstm_kernel, L=L, has_state=has_state, has_alias=has_alias),
        out_shape=out_shape,
        grid=(nseq, M_HEADS, nc),
        in_specs=in_specs,
        out_specs=out_specs,
        scratch_shapes=[pltpu.VMEM((M_V, M_QK), _F32), pltpu.VMEM((1, M_QK), _F32),
                        pltpu.VMEM((1, M_QK), _F32)],
        input_output_aliases=aliases,
        compiler_params=_params(("parallel", "parallel", "arbitrary")),
        name="mlstm",
    )(*args)
    n_new = stats[:, 0, :].reshape(nseq, M_HEADS, M_QK)
    m_new = stats[:, 1, 0].reshape(nseq, M_HEADS)
    return hm, c_new, n_new, m_new


def _rope(x, cos, sin_lo, sin_hi, half):
    n = x.shape[-1]
    return (x * cos + pltpu.roll(x, n - half, 1) * sin_lo + pltpu.roll(x, half, 1) * sin_hi)


def _hi_lo(x):
    hi = x.astype(_BF16).astype(_F32)
    return hi, x - hi


def _rope_kernel(aq_ref, ak_ref, av_ref, iq_ref, sm_ref, ta_ref, ti_ref,
                 q_out, k_out, kb_out, vb_out, iqx_out, ik_out, ikx_out):
    ca, sa_lo, sa_hi = ta_ref[0], ta_ref[1], ta_ref[2]
    ci, si_lo, si_hi = ti_ref[0], ti_ref[1], ti_ref[2]
    for h in range(A_HEADS):
        sl = slice(h * A_HD, (h + 1) * A_HD)
        q_out[:, sl] = _rope(aq_ref[:, sl], ca, sa_lo, sa_hi, A_ROT // 2).astype(q_out.dtype)
    for h in range(A_KV_HEADS):
        sl = slice(h * A_HD, (h + 1) * A_HD)
        kr = _rope(ak_ref[:, sl], ca, sa_lo, sa_hi, A_ROT // 2)
        k_out[:, sl] = kr
        kb_out[:, sl] = kr.astype(kb_out.dtype)
    vb_out[...] = av_ref[...].astype(vb_out.dtype)
    low = lax.broadcasted_iota(jnp.int32, (1, LANES), 1) < IDX_DIM
    for p in range(IDX_HEADS * IDX_DIM // LANES):
        x = _rope(iq_ref[:, p * LANES:(p + 1) * LANES], ci, si_lo, si_hi, IDX_ROT // 2)
        hi, lo = _hi_lo(x)
        hi_sw = pltpu.roll(hi, IDX_DIM, 1)
        lo_sw = pltpu.roll(lo, IDX_DIM, 1)
        c0 = 2 * p * IDX_XW
        iqx_out[:, c0:c0 + LANES] = jnp.where(low, hi, hi_sw).astype(iqx_out.dtype)
        iqx_out[:, c0 + LANES:c0 + 2 * LANES] = jnp.where(low, lo, 0.0).astype(iqx_out.dtype)
        iqx_out[:, c0 + 2 * LANES:c0 + 3 * LANES] = jnp.where(low, hi_sw, hi).astype(iqx_out.dtype)
        iqx_out[:, c0 + 3 * LANES:c0 + 4 * LANES] = jnp.where(low, lo_sw, 0.0).astype(iqx_out.dtype)
    ik = _rope(sm_ref[...], ci, si_lo, si_hi, IDX_ROT // 2)
    ik_out[...] = ik[:, SM_IK:SM_IK + IDX_DIM]
    hi, lo = _hi_lo(ik)
    ikx_out[:, 0:LANES] = jnp.where(low, hi, pltpu.roll(lo, IDX_DIM, 1)).astype(ikx_out.dtype)
    ikx_out[:, LANES:2 * LANES] = jnp.where(low, hi, 0.0).astype(ikx_out.dtype)


def _rope_tables(pos, rot, width, reps_valid):
    half = rot // 2
    inv_freq = jnp.exp(jnp.arange(half, dtype=_F32) * (-2.0 * math.log(ROPE_THETA) / rot))
    ang = pos.astype(_F32)[:, None] * inv_freq[None, :]
    cos, sin = jnp.cos(ang), jnp.sin(ang)
    m = pos.shape[0]
    one = jnp.ones((m, width - rot), _F32)
    zero = jnp.zeros((m, width - rot), _F32)
    zh = jnp.zeros((m, half), _F32)
    c_head = jnp.concatenate([cos, cos, one], axis=1)
    lo_head = jnp.concatenate([-sin, zh, zero], axis=1)
    hi_head = jnp.concatenate([zh, sin, zero], axis=1)
    reps = LANES // width
    ident = (jnp.ones((m, width), _F32), jnp.zeros((m, width), _F32), jnp.zeros((m, width), _F32))
    out = []
    for t, idt in zip((c_head, lo_head, hi_head), ident):
        out.append(jnp.concatenate([t if r < reps_valid else idt for r in range(reps)], axis=1))
    return jnp.stack(out)


def _rope_all(proj, small, pos):
    m = proj.shape[0]
    tm = _tile(m, 256, SUBLANES)
    ta = _rope_tables(pos, A_ROT, A_HD, 1)
    ti = _rope_tables(pos, IDX_ROT, IDX_DIM, LANES // IDX_DIM)
    wq, wk, wi = A_HEADS * A_HD, A_KV_HEADS * A_HD, IDX_HEADS * IDX_DIM
    wix = IDX_HEADS * IDX_XW
    out_shape = (jax.ShapeDtypeStruct((m, wq), _BF16),
                 jax.ShapeDtypeStruct((m, wk), _F32),
                 jax.ShapeDtypeStruct((m, wk), _BF16),
                 jax.ShapeDtypeStruct((m, wk), _BF16),
                 jax.ShapeDtypeStruct((m, wix), _BF16),
                 jax.ShapeDtypeStruct((m, IDX_DIM), _F32),
                 jax.ShapeDtypeStruct((m, IDX_XW), _BF16))
    return pl.pallas_call(
        _rope_kernel,
        out_shape=out_shape,
        grid=(m // tm,),
        in_specs=[pl.BlockSpec((tm, wq), lambda i: (i, B_AQ // wq)),
                  pl.BlockSpec((tm, wk), lambda i: (i, B_AK // wk)),
                  pl.BlockSpec((tm, wk), lambda i: (i, B_AV // wk)),
                  pl.BlockSpec((tm, wi), lambda i: (i, B_IQ // wi)),
                  pl.BlockSpec((tm, LANES), lambda i: (i, 0)),
                  pl.BlockSpec((3, tm, LANES), lambda i: (0, i, 0)),
                  pl.BlockSpec((3, tm, LANES), lambda i: (0, i, 0))],
        out_specs=(pl.BlockSpec((tm, wq), lambda i: (i, 0)),
                   pl.BlockSpec((tm, wk), lambda i: (i, 0)),
                   pl.BlockSpec((tm, wk), lambda i: (i, 0)),
                   pl.BlockSpec((tm, wk), lambda i: (i, 0)),
                   pl.BlockSpec((tm, wix), lambda i: (i, 0)),
                   pl.BlockSpec((tm, IDX_DIM), lambda i: (i, 0)),
                   pl.BlockSpec((tm, IDX_XW), lambda i: (i, 0))),
        compiler_params=_params(("parallel",)),
        name="rope",
    )(proj, proj, proj, proj, small, ta, ti)


def _attn_kernel(*refs, TQ, TK, past_len, l_valid, topk):
    q_ref, iqx_ref, sm_ref, k_ref, v_ref, ikx_ref = refs[:6]
    o_ref, key_scr, bias_scr, w_scr = refs[-4:]
    i = pl.program_id(1)
    q_start = past_len + i * TQ
    qpos = q_start + lax.broadcasted_iota(jnp.int32, (TQ, 1), 0)
    qchunk = qpos // CHUNK
    last_vis = jnp.minimum(((q_start + TQ - 1) // CHUNK + 1) * CHUNK, l_valid)
    nkt = (last_vis + TK - 1) // TK
    lane_pos = lax.broadcasted_iota(jnp.int32, (1, TK), 1)

    w_scale = (IDX_DIM ** -0.5) * (IDX_HEADS ** -0.5)
    for h in range(IDX_HEADS):
        w_scr[h] = jnp.broadcast_to(sm_ref[:, SM_IW + h:SM_IW + h + 1] * w_scale, (TQ, LANES))
    lane128 = lax.broadcasted_iota(jnp.int32, (1, LANES), 1)
    sc = min(TK, 2 * LANES)

    def score_tile(kt, carry):
        for cc in range(TK // sc):
            k0 = pl.multiple_of(kt * TK + cc * sc, sc)
            ik_c = ikx_ref[pl.ds(k0, sc), :]
            accs = [jnp.zeros((TQ, LANES), _F32) for _ in range(sc // LANES)]
            for h in range(IDX_HEADS):
                isc = _dot_nt(iqx_ref[:, h * IDX_XW:(h + 1) * IDX_XW], ik_c)
                w_h = w_scr[h]
                for c in range(sc // LANES):
                    accs[c] = accs[c] + jnp.maximum(isc[:, c * LANES:(c + 1) * LANES], 0.0) * w_h
            for c in range(sc // LANES):
                bits = pltpu.bitcast(accs[c] + 0.0, jnp.int32)
                key = jnp.where(bits < 0, bits ^ 0x7FFFFFFF, bits)
                kpos = k0 + c * LANES + lane128
                vis = ((kpos // CHUNK) <= qchunk) & (kpos < l_valid)
                col = cc * sc + c * LANES
                key_scr[kt, :, col:col + LANES] = jnp.where(vis, key, INT_MIN)
        return carry

    lax.fori_loop(0, nkt, score_tile, 0)

    def lane_fold(x):
        out = x[:, 0:LANES]
        for c in range(1, TK // LANES):
            out = out + x[:, c * LANES:(c + 1) * LANES]
        return out

    def count(pred_fn):
        def body(kt, part):
            return part + lane_fold(pred_fn(key_scr[kt], kt).astype(jnp.int32))
        part = lax.fori_loop(0, nkt, body, jnp.zeros((TQ, LANES), jnp.int32))
        return jnp.sum(part, axis=1, keepdims=True)

    def bit_step(it, thr_u):
        cand_u = thr_u | lax.shift_left(jnp.int32(1), 31 - it)
        cand_s = cand_u ^ INT_MIN
        cnt = count(lambda key, kt: key >= cand_s)
        return jnp.where(cnt >= topk, cand_u, thr_u)

    thr = lax.fori_loop(0, 32, bit_step, jnp.zeros((TQ, 1), jnp.int32)) ^ INT_MIN
    n_ge = count(lambda key, kt: key >= thr)

    def tie_search(_):
        need = topk - count(lambda key, kt: key > thr)
        nbits = (key_scr.shape[0] * TK).bit_length()

        def pos_step(it, p):
            cand = p | lax.shift_left(jnp.int32(1), nbits - 1 - it)
            cnt = count(lambda key, kt: (key == thr) & ((kt * TK + lane_pos) < cand))
            return jnp.where(cnt < need, cand, p)

        return lax.fori_loop(0, nbits, pos_step, jnp.zeros((TQ, 1), jnp.int32))

    has_tie = jnp.max(n_ge.astype(_F32)) > topk
    p_last = lax.cond(has_tie, tie_search, lambda _: jnp.full((TQ, 1), 2 ** 31 - 1, jnp.int32), 0)

    def bias_tile(kt, carry):
        key = key_scr[kt]
        sel = ((key > thr) | ((key == thr) & ((kt * TK + lane_pos) <= p_last))) & (key > KEY_NEG_INF)
        bias_scr[kt] = jnp.where(sel, 0.0, NEG_BIG)
        return carry

    lax.fori_loop(0, nkt, bias_tile, 0)

    scale = (A_HD ** -0.5) * math.log2(math.e)
    rows = A_GROUP * TQ
    n_par = 1

    def q_group(g):
        return jnp.concatenate(
            [q_ref[:, (g * A_GROUP + j) * A_HD:(g * A_GROUP + j + 1) * A_HD] for j in range(A_GROUP)], axis=0)

    for g0 in range(0, A_KV_HEADS, n_par):
        qgs = [q_group(g0 + u) for u in range(n_par)]

        def attn_tile(kt, carry, g0=g0, qgs=qgs):
            k0 = pl.multiple_of(kt * TK, TK)
            bias = bias_scr[kt][None]
            s_u = [_dot_nt(qgs[u], k_ref[pl.ds(k0, TK), (g0 + u) * A_HD:(g0 + u + 1) * A_HD])
                   .reshape(A_GROUP, TQ, TK) * scale + bias for u in range(n_par)]
            new = []
            p_u = []
            for u in range(n_par):
                m_i, l_i, acc = carry[u]
                m_new = jnp.maximum(m_i, jnp.max(s_u[u], axis=-1, keepdims=True))
                alpha = jnp.exp2(m_i - m_new)
                p = jnp.exp2(s_u[u] - m_new)
                p_u.append(p.reshape(rows, TK).astype(_BF16))
                new.append((m_new, alpha * l_i + jnp.sum(p, axis=-1, keepdims=True), alpha * acc))
            out = []
            for u in range(n_par):
                v_t = v_ref[pl.ds(k0, TK), (g0 + u) * A_HD:(g0 + u + 1) * A_HD]
                m_new, l_new, acc = new[u]
                out.append((m_new, l_new, acc + _dot(p_u[u], v_t).reshape(A_GROUP, TQ, A_HD)))
            return tuple(out)

        init = tuple((jnp.full((A_GROUP, TQ, 1), NEG_BIG, _F32), jnp.zeros((A_GROUP, TQ, 1), _F32),
                      jnp.zeros((A_GROUP, TQ, A_HD), _F32)) for _ in range(n_par))
        fin = lax.fori_loop(0, nkt, attn_tile, init)
        for u in range(n_par):
            _, l_f, acc_f = fin[u]
            out = acc_f / l_f
            for j in range(A_GROUP):
                hh = (g0 + u) * A_GROUP + j
                o_ref[:, hh * A_HD:(hh + 1) * A_HD] = out[j].astype(o_ref.dtype)


def _attention(q_rot, iq_rot, small, k_all, v_all, ik_all, row0, nseq, T, TQ, TK, lp, past_len, l_valid, o_prev=None):
    nq = T // TQ
    qb0 = row0 // TQ
    topk = min(TOPK_MAX, l_valid // 4)
    wq, wk, wix = A_HEADS * A_HD, A_KV_HEADS * A_HD, IDX_HEADS * IDX_XW
    in_specs = [pl.BlockSpec((TQ, wq), lambda b, i: (qb0 + b * nq + i, 0)),
                pl.BlockSpec((TQ, wix), lambda b, i: (qb0 + b * nq + i, 0)),
                pl.BlockSpec((TQ, LANES), lambda b, i: (qb0 + b * nq + i, 0)),
                pl.BlockSpec((lp, wk), lambda b, i: (b, 0)),
                pl.BlockSpec((lp, wk), lambda b, i: (b, 0)),
                pl.BlockSpec((lp, IDX_XW), lambda b, i: (b, 0))]
    args = [q_rot, iq_rot, small, k_all, v_all, ik_all]
    aliases = {}
    if o_prev is not None:
        aliases = {len(args): 0}
        in_specs.append(pl.BlockSpec(memory_space=pl.ANY))
        args.append(o_prev)
    return pl.pallas_call(
        functools.partial(_attn_kernel, TQ=TQ, TK=TK, past_len=past_len, l_valid=l_valid, topk=topk),
        out_shape=jax.ShapeDtypeStruct((q_rot.shape[0], wq), _BF16),
        grid=(nseq, nq),
        in_specs=in_specs,
        out_specs=pl.BlockSpec((TQ, wq), lambda b, i: (qb0 + b * nq + i, 0)),
        scratch_shapes=[pltpu.VMEM((lp // TK, TQ, TK), jnp.int32),
                        pltpu.VMEM((lp // TK, TQ, TK), _F32),
                        pltpu.VMEM((IDX_HEADS, TQ, LANES), _F32)],
        input_output_aliases=aliases,
        compiler_params=_params(("parallel", "arbitrary")),
        name="sparse_attention",
    )(*args)


def _merge_kernel(hm_ref, ha_ref, wm_ref, wa_ref, gm_ref, ga_ref, o_ref):
    pm = _dot(hm_ref[...], wm_ref[...])
    pa = _dot(ha_ref[...], wa_ref[...])
    o_ref[...] = (jax.nn.sigmoid(gm_ref[...]) * pm + jax.nn.sigmoid(ga_ref[...]) * pa).astype(o_ref.dtype)


def _merge(hm, ha, wm, wa, proj, d):
    m = hm.shape[0]
    tm = _tile(m, 512, SUBLANES)
    tn = _tile(d, 1024)
    gm0 = 0
    ga0 = d // tn
    km, ka = hm.shape[1], ha.shape[1]
    return pl.pallas_call(
        _merge_kernel,
        out_shape=jax.ShapeDtypeStruct((m, d), _BF16),
        grid=(d // tn, m // tm),
        in_specs=[pl.BlockSpec((tm, km), lambda j, i: (i, 0)),
                  pl.BlockSpec((tm, ka), lambda j, i: (i, 0)),
                  pl.BlockSpec((km, tn), lambda j, i: (0, j)),
                  pl.BlockSpec((ka, tn), lambda j, i: (0, j)),
                  pl.BlockSpec((tm, tn), lambda j, i: (i, gm0 + j)),
                  pl.BlockSpec((tm, tn), lambda j, i: (i, ga0 + j))],
        out_specs=pl.BlockSpec((tm, tn), lambda j, i: (i, j)),
        compiler_params=_params(("parallel", "parallel")),
        name="gated_merge",
    )(hm, ha, wm, wa, proj, proj)


def _resnorm_next_kernel(xa_ref, xb_ref, y_ref, g_ref, g2_ref, x1_ref, h_ref, *, n_first):
    def body(x_ref):
        y = y_ref[...]
        ms = jnp.mean(y * y, axis=-1, keepdims=True)
        x1 = x_ref[...] + y * lax.rsqrt(ms + NORM_EPS) * g_ref[...]
        x1_ref[...] = x1
        ms1 = jnp.mean(x1 * x1, axis=-1, keepdims=True)
        h_ref[...] = (x1 * lax.rsqrt(ms1 + NORM_EPS) * g2_ref[...]).astype(h_ref.dtype)

    i = pl.program_id(0)
    pl.when(i < n_first)(lambda: body(xa_ref))
    pl.when(i >= n_first)(lambda: body(xb_ref))


def _resnorm_next(xa, xb, y, g, g_next):
    m, d = y.shape
    tm = _row_tile(xa.shape[0], xb.shape[0])
    n_first = xa.shape[0] // tm
    row = pl.BlockSpec((tm, d), lambda i: (i, 0))
    vec = pl.BlockSpec((1, d), lambda i: (0, 0))
    return pl.pallas_call(
        functools.partial(_resnorm_next_kernel, n_first=n_first),
        out_shape=(jax.ShapeDtypeStruct((m, d), _F32), jax.ShapeDtypeStruct((m, d), _BF16)),
        grid=(m // tm,), in_specs=[*_two_source_specs(tm, d, n_first), row, vec, vec], out_specs=(row, row),
        compiler_params=_params(("parallel",)), name="resnorm_next",
    )(xa, xb, y, g.reshape(1, d), g_next.reshape(1, d))


def _resnorm_kernel(x_ref, y_ref, g_ref, oa_ref, ob_ref, *, n_first):
    y = y_ref[...]
    ms = jnp.mean(y * y, axis=-1, keepdims=True)
    out = x_ref[...] + y * lax.rsqrt(ms + NORM_EPS) * g_ref[...]
    i = pl.program_id(0)

    @pl.when(i < n_first)
    def _():
        oa_ref[...] = out

    @pl.when(i >= n_first)
    def _():
        ob_ref[...] = out


def _resnorm_split(x, y, g, m_a):
    m, d = x.shape
    m_b = m - m_a
    tm = _row_tile(m_a, m_b)
    n_first = m_a // tm
    row = pl.BlockSpec((tm, d), lambda i: (i, 0))
    vec = pl.BlockSpec((1, d), lambda i: (0, 0))
    return pl.pallas_call(
        functools.partial(_resnorm_kernel, n_first=n_first),
        out_shape=(jax.ShapeDtypeStruct((m_a, d), _F32), jax.ShapeDtypeStruct((m_b, d), _F32)),
        grid=(m // tm,), in_specs=[row, row, vec], out_specs=_two_source_specs(tm, d, n_first),
        compiler_params=_params(("arbitrary",)), name="resnorm",
    )(x, y, g.reshape(1, d))


HALO = SUBLANES


def _conv_gelu_gate(cw_ref, cb_ref, prev2, prev1, gate, lin):
    gc = cw_ref[0:1, :] * prev2 + cw_ref[1:2, :] * prev1 + cw_ref[2:3, :] * gate + cb_ref[...]
    c = -2.0 * math.sqrt(2.0 / math.pi)
    act = gc / (1.0 + jnp.exp(gc * (c + (c * 0.044715) * (gc * gc))))
    return act * lin


def _ffn_up_first_kernel(h_ref, wg_ref, wl_ref, cw_ref, cb_ref, z_ref, tail_ref, wg_bf, wl_bf, g_scr,
                         *, tm, seq_len):
    i = pl.program_id(1)
    tn = z_ref.shape[1]

    @pl.when(i == 0)
    def _():
        wg_bf[...] = wg_ref[...].astype(wg_bf.dtype)
        wl_bf[...] = wl_ref[...].astype(wl_bf.dtype)

    @pl.when((i * tm) % seq_len == 0)
    def _():
        g_scr[0:HALO, :] = jnp.zeros((HALO, tn), _F32)

    h = h_ref[...]
    gate = _dot(h, wg_bf[...])
    lin = _dot(h, wl_bf[...])
    g_scr[HALO:HALO + tm, :] = gate
    zed = _conv_gelu_gate(cw_ref, cb_ref, g_scr[HALO - 2:HALO - 2 + tm, :], g_scr[HALO - 1:HALO - 1 + tm, :],
                          gate, lin)
    z_ref[...] = zed.astype(z_ref.dtype)
    g_scr[0:HALO, :] = gate[tm - HALO:tm, :]
    tail_ref[...] = gate[tm - SUBLANES:tm, :]


def _ffn_up_state_kernel(h_ref, wg_ref, wl_ref, cw_ref, cb_ref, init_ref, z_any, z_ref, tail_ref, g_scr,
                         *, tm, seq_len):
    tn = z_ref.shape[1]
    h = h_ref[...]
    gate = _dot(h, wg_ref[...].astype(_BF16))
    lin = _dot(h, wl_ref[...].astype(_BF16))
    g_scr[0:HALO, :] = jnp.zeros((HALO, tn), _F32)
    g_scr[HALO:HALO + tm, :] = gate
    prev1 = g_scr[HALO - 1:HALO - 1 + tm, :]
    prev2 = g_scr[HALO - 2:HALO - 2 + tm, :]
    nseg = tm // seq_len
    init0 = jnp.concatenate([jnp.broadcast_to(init_ref[s, 0:1, :], (seq_len, tn)) for s in range(nseg)], axis=0)
    init1 = jnp.concatenate([jnp.broadcast_to(init_ref[s, 1:2, :], (seq_len, tn)) for s in range(nseg)], axis=0)
    t = lax.broadcasted_iota(jnp.int32, (tm, 1), 0) % seq_len
    prev1 = jnp.where(t == 0, init1, prev1)
    prev2 = jnp.where(t == 0, init0, jnp.where(t == 1, init1, prev2))
    z_ref[...] = _conv_gelu_gate(cw_ref, cb_ref, prev2, prev1, gate, lin).astype(z_ref.dtype)
    for s in range(nseg):
        tail_ref[s * SUBLANES:(s + 1) * SUBLANES, :] = gate[(s + 1) * seq_len - SUBLANES:(s + 1) * seq_len, :]


def _conv_state(tails, nseq, segs_per_seq, dff):
    tails = tails.reshape(nseq * segs_per_seq, SUBLANES, dff)
    last = tails[segs_per_seq - 1::segs_per_seq, SUBLANES - (CONV_W - 1):, :]
    return last.reshape(nseq, CONV_W - 1, dff)


def _ffn_up_specs(d, tn, nj, h_spec):
    return [h_spec,
            pl.BlockSpec((d, tn), lambda j, i: (0, j)),
            pl.BlockSpec((d, tn), lambda j, i: (0, nj + j)),
            pl.BlockSpec((CONV_W, tn), lambda j, i: (0, j)),
            pl.BlockSpec((1, tn), lambda j, i: (0, j))]


def _ffn_up_first(hf, w_up, conv_w, conv_b, nrows, seq_len):
    m_all, d = hf.shape
    dff = w_up.shape[1] // 2
    tm = _tile(seq_len, 1024, SUBLANES)
    tn = _tile(dff, 256)
    nj = dff // tn
    nt = nrows // tm
    specs = _ffn_up_specs(d, tn, nj, pl.BlockSpec((tm, d), lambda j, i: (i, 0)))
    z, tails = pl.pallas_call(
        functools.partial(_ffn_up_first_kernel, tm=tm, seq_len=seq_len),
        out_shape=(jax.ShapeDtypeStruct((m_all, dff), _BF16),
                   jax.ShapeDtypeStruct((nt * SUBLANES, dff), _F32)),
        grid=(nj, nt),
        in_specs=specs,
        out_specs=(pl.BlockSpec((tm, tn), lambda j, i: (i, j)),
                   pl.BlockSpec((SUBLANES, tn), lambda j, i: (i, j))),
        scratch_shapes=[pltpu.VMEM((d, tn), _BF16), pltpu.VMEM((d, tn), _BF16),
                        pltpu.VMEM((HALO + tm, tn), _F32)],
        compiler_params=_params(("parallel", "arbitrary")),
        name="ffn_up_conv_first",
    )(hf, w_up, w_up, conv_w, conv_b.reshape(1, dff))
    return z, _conv_state(tails, nrows // seq_len, seq_len // tm, dff)


def _ffn_up_state(hf, w_up, conv_w, conv_b, row0, nrows, seq_len, init, z_prev):
    m_all, d = hf.shape
    dff = w_up.shape[1] // 2
    tm = _tile(nrows, 512, seq_len)
    tn = _tile(dff, 256)
    nj = dff // tn
    nseg = tm // seq_len
    rb0 = row0 // tm
    specs = _ffn_up_specs(d, tn, nj, pl.BlockSpec((tm, d), lambda j, i: (rb0 + i, 0)))
    specs += [pl.BlockSpec((nseg, CONV_W - 1, tn), lambda j, i: (i, 0, j)), pl.BlockSpec(memory_space=pl.ANY)]
    z, tails = pl.pallas_call(
        functools.partial(_ffn_up_state_kernel, tm=tm, seq_len=seq_len),
        out_shape=(jax.ShapeDtypeStruct((m_all, dff), _BF16),
                   jax.ShapeDtypeStruct((nrows // seq_len * SUBLANES, dff), _F32)),
        grid=(nj, nrows // tm),
        in_specs=specs,
        out_specs=(pl.BlockSpec((tm, tn), lambda j, i: (rb0 + i, j)),
                   pl.BlockSpec((nseg * SUBLANES, tn), lambda j, i: (i, j))),
        scratch_shapes=[pltpu.VMEM((HALO + tm, tn), _F32)],
        input_output_aliases={6: 0},
        compiler_params=_params(("parallel", "arbitrary")),
        name="ffn_up_conv_state",
    )(hf, w_up, w_up, conv_w, conv_b.reshape(1, dff), init, z_prev)
    return z, _conv_state(tails, nrows // seq_len, 1, dff)


def _layer(xa, xb, geom, cache, state, w):
    (B, S, Bd, Td, P) = geom
    (cache_k, cache_v, cache_ik) = cache
    (state_C, state_n, state_m, state_conv) = state
    (w_in, b_igate, b_fgate, g_mhnorm, w_proj_m, w_proj_a, w_out,
     g_pre_mix, g_post_mix, g_pre_ffn, g_post_ffn, w_up, conv_w, conv_b, w_down) = w
    mp, d = xa.shape
    dff = w_down.shape[0]
    wk = A_KV_HEADS * A_HD

    c_mi = A_END
    c_aq = c_mi + 2 * M_HEADS
    c_ik = c_aq + B_END
    c_iw = c_ik + IDX_DIM
    c_gm = c_iw + IDX_HEADS
    w_t = jnp.swapaxes(w_in, 0, 1).astype(_F32)
    w_small_t = jnp.concatenate([w_t[c_ik:c_iw], w_t[c_mi:c_aq], w_t[c_iw:c_gm],
                                 jnp.zeros((LANES - SM_END, d), _F32)], axis=0)
    gate_bias = jnp.concatenate([jnp.zeros((SM_MI,), _F32), b_igate.astype(_F32), b_fgate.astype(_F32),
                                 jnp.zeros((LANES - SM_IW,), _F32)]).reshape(1, LANES)

    hn = _rmsnorm_cast(xa, xb, g_pre_mix)
    proj_a = _matmul_w32(hn, w_t, 0, A_END, _F32, transposed=True, name="in_proj_mlstm")
    proj_b = _matmul_w32(hn, w_t, c_aq, B_END, _F32, transposed=True, name="in_proj_attn")
    proj_c = _matmul_w32(hn, w_t, c_gm, 2 * d, _F32, transposed=True, name="in_proj_gates")
    small = _matmul_w32(hn, w_small_t, 0, LANES, _F32, transposed=True, name="in_proj_small")

    lp_chunk = _tile(S, 256, CHUNK)
    hm_p, c_p, n_p, m_p = _mlstm(proj_a, small, gate_bias, g_mhnorm, 0, B, S, lp_chunk, None)
    s0 = jnp.zeros((Bd * M_HEADS, SUBLANES, M_QK), _F32)
    s0 = s0.at[:, 0, :].set(state_n.reshape(Bd * M_HEADS, M_QK).astype(_F32))
    s0 = s0.at[:, 1, :].set(jnp.broadcast_to(state_m.reshape(Bd * M_HEADS, 1).astype(_F32), (Bd * M_HEADS, M_QK)))
    hm, c_s, n_s, m_s = _mlstm(proj_a, small, gate_bias, g_mhnorm, mp, Bd, Td, min(CHUNK, Td),
                               (state_C.astype(_F32), s0), hm_p)

    pos = jnp.concatenate([jnp.tile(jnp.arange(S, dtype=jnp.int32), B),
                           jnp.tile(P + jnp.arange(Td, dtype=jnp.int32), Bd)])
    q_rot, k_rot, k_bf, v_bf, iqx, ik_rot, ikx = _rope_all(proj_b, small, pos)
    tq_p = _tile(S, 256, CHUNK)
    tk_p = _tile(S, 512)
    ha_p = _attention(q_rot, iqx, small, k_bf, v_bf, ikx, 0, B, S, tq_p, tk_p, S, 0, S)
    l_s = P + Td
    tk_s = 256
    lp_s = -(-l_s // tk_s) * tk_s
    pad = lp_s - l_s
    k_s = jnp.concatenate([cache_k.reshape(Bd, P, wk).astype(_BF16), k_bf[mp:].reshape(Bd, Td, wk),
                           jnp.zeros((Bd, pad, wk), _BF16)], axis=1).reshape(Bd * lp_s, wk)
    v_s = jnp.concatenate([cache_v.reshape(Bd, P, wk).astype(_BF16), v_bf[mp:].reshape(Bd, Td, wk),
                           jnp.zeros((Bd, pad, wk), _BF16)], axis=1).reshape(Bd * lp_s, wk)
    cik = cache_ik.astype(_F32)
    cik_hi = cik.astype(_BF16)
    cik_lo = (cik - cik_hi.astype(_F32)).astype(_BF16)
    cikx = jnp.concatenate([cik_hi, cik_lo, cik_hi, jnp.zeros_like(cik_hi)], axis=-1)
    ikx_s = jnp.concatenate([cikx, ikx[mp:].reshape(Bd, Td, IDX_XW),
                             jnp.zeros((Bd, pad, IDX_XW), _BF16)], axis=1).reshape(Bd * lp_s, IDX_XW)
    ha = _attention(q_rot, iqx, small, k_s, v_s, ikx_s, mp, Bd, Td, Td, tk_s, lp_s, P, l_s, ha_p)

    mix = _merge(hm, ha, w_proj_m.astype(_BF16), w_proj_a.astype(_BF16), proj_c, d)
    y1 = _matmul_w32(mix, w_out.astype(_F32), 0, d, _F32, transposed=False, name="out_proj")
    x1, hf = _resnorm_next(xa, xb, y1, g_post_mix, g_pre_ffn)

    w_up = w_up.astype(_F32)
    z_p, conv_p = _ffn_up_first(hf, w_up, conv_w, conv_b, mp, S)
    z, conv_s = _ffn_up_state(hf, w_up, conv_w, conv_b, mp, Bd * Td, Td, state_conv.astype(_F32), z_p)
    y2 = _matmul(z, w_down.astype(_BF16), _F32, name="down_proj")
    x2 = _resnorm_split(x1, y2, g_post_ffn, mp)

    av = proj_b[:, B_AV:B_AV + wk]
    outs_p = (k_rot[:mp].reshape(B, S, A_KV_HEADS, A_HD), av[:mp].reshape(B, S, A_KV_HEADS, A_HD),
              ik_rot[:mp].reshape(B, S, IDX_DIM), c_p, n_p, m_p, conv_p)
    outs_s = (k_rot[mp:].reshape(Bd, Td, A_KV_HEADS, A_HD), av[mp:].reshape(Bd, Td, A_KV_HEADS, A_HD),
              ik_rot[mp:].reshape(Bd, Td, IDX_DIM), c_s, n_s, m_s, conv_s)
    return x2, outs_p, outs_s


def kernel(x_prompt, x_sample, cache_k, cache_v, cache_idx_k, state_C, state_n, state_m, state_conv,
           w_in, b_igate, b_fgate, g_mhnorm, w_proj_m, w_proj_a, w_out,
           g_pre_mix, g_post_mix, g_pre_ffn, g_post_ffn, w_up, conv_w, conv_b, w_down):
    B, S, d = x_prompt.shape
    Bd, Td, _ = x_sample.shape
    P = cache_k.shape[2]
    depth = w_in.shape[0]
    mp = B * S
    xa, xb = x_prompt.reshape(mp, d), x_sample.reshape(Bd * Td, d)
    all_p, all_s = [], []
    for l in range(depth):
        w = (w_in[l], b_igate[l], b_fgate[l], g_mhnorm[l], w_proj_m[l], w_proj_a[l], w_out[l],
             g_pre_mix[l], g_post_mix[l], g_pre_ffn[l], g_post_ffn[l], w_up[l], conv_w[l], conv_b[l], w_down[l])
        (xa, xb), outs_p, outs_s = _layer(xa, xb, (B, S, Bd, Td, P), (cache_k[l], cache_v[l], cache_idx_k[l]),
                                          (state_C[l], state_n[l], state_m[l], state_conv[l]), w)
        all_p.append(outs_p)
        all_s.append(outs_s)

    def stk(outs, i):
        return jnp.stack([o[i] for o in outs])

    yp = xa.reshape(B, S, d)
    ys = xb.reshape(Bd, Td, d)
    return (yp, ys) + tuple(stk(all_p, i) for i in range(7)) + tuple(stk(all_s, i) for i in range(7))
```

```python
import functools
import math

import jax
import jax.numpy as jnp
from jax import lax
from jax.experimental import pallas as pl
from jax.experimental.pallas import tpu as pltpu

CHUNK = 64
NORM_EPS = 1e-6
ROPE_THETA = 500000.0
M_HEADS = 8
M_QK = 128
M_V = 256
A_HEADS = 16
A_KV_HEADS = 4
A_GROUP = A_HEADS // A_KV_HEADS
A_HD = 128
A_ROT = A_HD // 4
IDX_HEADS = 16
IDX_DIM = 64
IDX_ROT = IDX_DIM // 4
TOPK_MAX = 256
CONV_W = 3

LANES = 128
SUBLANES = 8
VMEM_LIMIT_BYTES = 52 * 1024 * 1024

A_MQ = 0
A_MK = A_MQ + M_HEADS * M_QK
A_MV = A_MK + M_HEADS * M_QK
A_MO = A_MV + M_HEADS * M_V
A_END = A_MO + M_HEADS * M_V
B_AQ = 0
B_AK = B_AQ + A_HEADS * A_HD
B_AV = B_AK + A_KV_HEADS * A_HD
B_IQ = B_AV + A_KV_HEADS * A_HD
B_END = B_IQ + IDX_HEADS * IDX_DIM
SM_IK = 0
SM_MI = SM_IK + IDX_DIM
SM_MF = SM_MI + M_HEADS
SM_IW = SM_MF + M_HEADS
SM_END = SM_IW + IDX_HEADS
IDX_XW = 4 * IDX_DIM

INT_MIN = -2 ** 31
KEY_NEG_INF = INT_MIN + 0x7FFFFF
NEG_BIG = -1e30

_BF16 = jnp.bfloat16
_F32 = jnp.float32


def _tile(dim, target, quantum=LANES):
    if dim <= target:
        return dim
    t = (target // quantum) * quantum
    while t >= quantum:
        if dim % t == 0:
            return t
        t -= quantum
    return dim


def _params(sem):
    return pltpu.CompilerParams(dimension_semantics=sem, vmem_limit_bytes=VMEM_LIMIT_BYTES)


def _dot(a, b):
    return jnp.dot(a, b, preferred_element_type=_F32)


def _dot_nt(a, b):
    return lax.dot_general(a, b, (((1,), (1,)), ((), ())), preferred_element_type=_F32)


def _dot_tn(a, b):
    return lax.dot_general(a, b, (((0,), (0,)), ((), ())), preferred_element_type=_F32)


def _mm_w32_kernel(a_ref, w_hbm, o_ref, w_f32, w_bf, sem, *, off0, nj, transposed):
    j = pl.program_id(0)
    kd, tn = w_bf.shape

    def tile_copy(jj):
        if transposed:
            src = w_hbm.at[pl.ds(pl.multiple_of(off0 + jj * tn, SUBLANES), tn), :]
        else:
            src = w_hbm.at[:, pl.ds(pl.multiple_of(off0 + jj * tn, LANES), tn)]
        return pltpu.make_async_copy(src, w_f32, sem)

    @pl.when(pl.program_id(1) == 0)
    def _():
        @pl.when(j == 0)
        def _():
            tile_copy(0).start()

        tile_copy(j).wait()
        if transposed:
            sq = math.gcd(tn, kd, 512)
            for r in range(0, tn, sq):
                for c in range(0, kd, sq):
                    w_bf[c:c + sq, r:r + sq] = w_f32[r:r + sq, c:c + sq].T.astype(w_bf.dtype)
        else:
            w_bf[...] = w_f32[...].astype(w_bf.dtype)

        @pl.when(j + 1 < nj)
        def _():
            tile_copy(j + 1).start()

    o_ref[...] = _dot(a_ref[...], w_bf[...]).astype(o_ref.dtype)


def _matmul_w32(a, w, off0, n, out_dtype, *, transposed, tm=512, tn=1024, name="matmul_w32"):
    m, kd = a.shape
    tm = _tile(m, tm, SUBLANES)
    tn = _tile(n, tn)
    nj = n // tn
    return pl.pallas_call(
        functools.partial(_mm_w32_kernel, off0=off0, nj=nj, transposed=transposed),
        out_shape=jax.ShapeDtypeStruct((m, n), out_dtype),
        grid=(nj, m // tm),
        in_specs=[pl.BlockSpec((tm, kd), lambda j, i: (i, 0)),
                  pl.BlockSpec(memory_space=pl.ANY)],
        out_specs=pl.BlockSpec((tm, tn), lambda j, i: (i, j)),
        scratch_shapes=[pltpu.VMEM((tn, kd) if transposed else (kd, tn), _F32), pltpu.VMEM((kd, tn), _BF16),
                        pltpu.SemaphoreType.DMA],
        compiler_params=_params(("arbitrary", "arbitrary")),
        name=name,
    )(a, w)


def _two_source_specs(tm, d, n_first):
    return (pl.BlockSpec((tm, d), lambda i: (jnp.minimum(i, n_first - 1), 0)),
            pl.BlockSpec((tm, d), lambda i: (jnp.maximum(i - n_first, 0), 0)))


def _rmsnorm_kernel(xa_ref, xb_ref, g_ref, o_ref, *, n_first):
    def body(x_ref):
        x = x_ref[...]
        ms = jnp.mean(x * x, axis=-1, keepdims=True)
        o_ref[...] = (x * lax.rsqrt(ms + NORM_EPS) * g_ref[...]).astype(o_ref.dtype)

    i = pl.program_id(0)
    pl.when(i < n_first)(lambda: body(xa_ref))
    pl.when(i >= n_first)(lambda: body(xb_ref))


def _row_tile(m_a, m_b):
    return _tile(math.gcd(m_a, m_b), 256, SUBLANES)


def _rmsnorm_cast(xa, xb, g):
    (m_a, d), m_b = xa.shape, xb.shape[0]
    tm = _row_tile(m_a, m_b)
    n_first = m_a // tm
    return pl.pallas_call(
        functools.partial(_rmsnorm_kernel, n_first=n_first),
        out_shape=jax.ShapeDtypeStruct((m_a + m_b, d), _BF16),
        grid=((m_a + m_b) // tm,),
        in_specs=[*_two_source_specs(tm, d, n_first), pl.BlockSpec((1, d), lambda i: (0, 0))],
        out_specs=pl.BlockSpec((tm, d), lambda i: (i, 0)),
        compiler_params=_params(("parallel",)),
        name="rmsnorm_cast",
    )(xa, xb, g.reshape(1, d))


def _mm_kernel(a_ref, b_ref, o_ref):
    o_ref[...] = _dot(a_ref[...], b_ref[...]).astype(o_ref.dtype)


def _matmul(a, b, out_dtype, *, tm=256, tn=1024, name="matmul"):
    m, kd = a.shape
    _, n = b.shape
    tm = _tile(m, tm, SUBLANES)
    tn = _tile(n, tn)
    return pl.pallas_call(
        _mm_kernel,
        out_shape=jax.ShapeDtypeStruct((m, n), out_dtype),
        grid=(n // tn, m // tm),
        in_specs=[pl.BlockSpec((tm, kd), lambda j, i: (i, 0)),
                  pl.BlockSpec((kd, tn), lambda j, i: (0, j), pipeline_mode=pl.Buffered(1))],
        out_specs=pl.BlockSpec((tm, tn), lambda j, i: (i, j)),
        compiler_params=_params(("parallel", "parallel")),
        name=name,
    )(a, b)


def _mlstm_kernel(*refs, L, has_state, has_alias):
    q_ref, k_ref, v_ref, o_ref, g_ref, gb_ref, gn_ref = refs[:7]
    if has_state:
        c0_ref, s0_ref = refs[7:9]
    h_ref, c_out_ref, s_out_ref, c_scr, n_scr, m_scr = refs[7 + 2 * has_state + has_alias:]
    head = pl.program_id(1)
    c = pl.program_id(2)
    nc = pl.num_programs(2)

    @pl.when(c == 0)
    def _():
        if has_state:
            c_scr[...] = c0_ref[0, 0]
            n_scr[...] = s0_ref[0, 0:1, :]
            m_scr[...] = s0_ref[0, 1:2, :]
        else:
            c_scr[...] = jnp.zeros_like(c_scr)
            n_scr[...] = jnp.zeros_like(n_scr)
            m_scr[...] = jnp.zeros_like(m_scr)

    gates = g_ref[...] + gb_ref[...]
    lane = lax.broadcasted_iota(jnp.int32, gates.shape, 1)
    ig_col = jnp.sum(jnp.where(lane == SM_MI + head, gates, 0.0), axis=1, keepdims=True)
    mf_col = jnp.sum(jnp.where(lane == SM_MF + head, gates, 0.0), axis=1, keepdims=True)
    lf_col = jnp.minimum(mf_col, 0.0) - jnp.log1p(jnp.exp(-jnp.abs(mf_col)))

    ri = lax.broadcasted_iota(jnp.int32, (L, L), 0)
    ci = lax.broadcasted_iota(jnp.int32, (L, L), 1)
    eye = ri == ci
    tril = ci <= ri
    lf_row = jnp.sum(jnp.where(eye, lf_col, 0.0), axis=0, keepdims=True)
    ig_row = jnp.sum(jnp.where(eye, ig_col, 0.0), axis=0, keepdims=True)
    b_col = jnp.sum(jnp.where(tril, lf_row, 0.0), axis=1, keepdims=True)
    b_row = jnp.sum(jnp.where(ri <= ci, lf_col, 0.0), axis=0, keepdims=True)
    logw = jnp.where(tril, b_col - b_row + ig_row, -jnp.inf)

    m_prev = m_scr[:, 0:1]
    inter = b_col + m_prev
    m_t = jnp.maximum(inter, jnp.max(logw, axis=1, keepdims=True))
    a = jnp.exp(inter - m_t)
    sw = jnp.exp(logw - m_t)

    q = q_ref[...]
    k = k_ref[...] * (M_QK ** -0.5)
    v = v_ref[...]
    qb = q.astype(_BF16)
    kb = k.astype(_BF16)
    s = _dot_nt(qb, kb) * sw
    c_old = c_scr[...]
    n_old = n_scr[...]
    num = a * _dot_nt(qb, c_old.astype(_BF16)) + _dot(s.astype(_BF16), v.astype(_BF16))
    den = a * jnp.sum(q * n_old, axis=1, keepdims=True) + jnp.sum(s, axis=1, keepdims=True)
    h = num / jnp.maximum(jnp.abs(den), jnp.exp(-m_t))

    m_new = m_t[L - 1:L, :]
    b_last = b_col[L - 1:L, :]
    g_col = jnp.exp(b_last - b_col + ig_col - m_new)
    decay = jnp.exp(b_last + m_prev - m_new)
    c_new = decay * c_old + _dot_tn((v * g_col).astype(_BF16), kb)
    n_new = decay * n_old + jnp.sum(g_col * k, axis=0, keepdims=True)
    c_scr[...] = c_new
    n_scr[...] = n_new
    m_scr[...] = jnp.broadcast_to(m_new, m_scr.shape)

    ms = jnp.mean(h * h, axis=1, keepdims=True)
    y = h * lax.rsqrt(ms + NORM_EPS) * gn_ref[0]
    h_ref[...] = (y * jax.nn.sigmoid(o_ref[...])).astype(h_ref.dtype)

    @pl.when(c == nc - 1)
    def _():
        c_out_ref[0, 0] = c_new
        s_out_ref[0] = jnp.zeros(s_out_ref.shape[1:], _F32)
        s_out_ref[0, 0:1, :] = n_new
        s_out_ref[0, 1:2, :] = jnp.broadcast_to(m_new, (1, M_QK))


def _mlstm(proj, small, gate_bias, g_mhnorm, row0, nseq, T, L, state, h_prev=None):
    nc = T // L
    rb0 = row0 // L
    has_state = state is not None
    has_alias = h_prev is not None

    def rows(b, h, c):
        return rb0 + b * nc + c

    in_specs = [
        pl.BlockSpec((L, M_QK), lambda b, h, c: (rows(b, h, c), A_MQ // M_QK + h)),
        pl.BlockSpec((L, M_QK), lambda b, h, c: (rows(b, h, c), A_MK // M_QK + h)),
        pl.BlockSpec((L, M_V), lambda b, h, c: (rows(b, h, c), A_MV // M_V + h)),
        pl.BlockSpec((L, M_V), lambda b, h, c: (rows(b, h, c), A_MO // M_V + h)),
        pl.BlockSpec((L, LANES), lambda b, h, c: (rows(b, h, c), 0)),
        pl.BlockSpec((1, LANES), lambda b, h, c: (0, 0)),
        pl.BlockSpec((1, 1, M_V), lambda b, h, c: (h, 0, 0)),
    ]
    args = [proj, proj, proj, proj, small, gate_bias, g_mhnorm.reshape(M_HEADS, 1, M_V)]
    if has_state:
        c0, s0 = state
        in_specs += [pl.BlockSpec((1, 1, M_V, M_QK), lambda b, h, c: (b, h, 0, 0)),
                     pl.BlockSpec((1, SUBLANES, M_QK), lambda b, h, c: (b * M_HEADS + h, 0, 0))]
        args += [c0, s0]
    aliases = {}
    if has_alias:
        aliases = {len(args): 0}
        in_specs.append(pl.BlockSpec(memory_space=pl.ANY))
        args.append(h_prev)
    out_shape = (jax.ShapeDtypeStruct((proj.shape[0], M_HEADS * M_V), _BF16),
                 jax.ShapeDtypeStruct((nseq, M_HEADS, M_V, M_QK), _F32),
                 jax.ShapeDtypeStruct((nseq * M_HEADS, SUBLANES, M_QK), _F32))
    out_specs = (pl.BlockSpec((L, M_V), lambda b, h, c: (rows(b, h, c), h)),
                 pl.BlockSpec((1, 1, M_V, M_QK), lambda b, h, c: (b, h, 0, 0)),
                 pl.BlockSpec((1, SUBLANES, M_QK), lambda b, h, c: (b * M_HEADS + h, 0, 0)))
    hm, c_new, stats = pl.pallas_call(
        functools.partial(_mlstm_kernel, L=L, has_state=has_state, has_alias=has_alias),
        out_shape=out_shape,
        grid=(nseq, M_HEADS, nc),
        in_specs=in_specs,
        out_specs=out_specs,
        scratch_shapes=[pltpu.VMEM((M_V, M_QK), _F32), pltpu.VMEM((1, M_QK), _F32),
                        pltpu.VMEM((1, M_QK), _F32)],
        input_output_aliases=aliases,
        compiler_params=_params(("parallel", "parallel", "arbitrary")),
        name="mlstm",
    )(*args)
    n_new = stats[:, 0, :].reshape(nseq, M_HEADS, M_QK)
    m_new = stats[:, 1, 0].reshape(nseq, M_HEADS)
    return hm, c_new, n_new, m_new


def _rope(x, cos, sin_lo, sin_hi, half):
    n = x.shape[-1]
    return (x * cos + pltpu.roll(x, n - half, 1) * sin_lo + pltpu.roll(x, half, 1) * sin_hi)


def _hi_lo(x):
    hi = x.astype(_BF16).astype(_F32)
    return hi, x - hi


def _rope_kernel(aq_ref, ak_ref, av_ref, iq_ref, sm_ref, ta_ref, ti_ref,
                 q_out, k_out, kb_out, vb_out, iqx_out, ik_out, ikx_out):
    ca, sa_lo, sa_hi = ta_ref[0], ta_ref[1], ta_ref[2]
    ci, si_lo, si_hi = ti_ref[0], ti_ref[1], ti_ref[2]
    for h in range(A_HEADS):
        sl = slice(h * A_HD, (h + 1) * A_HD)
        q_out[:, sl] = _rope(aq_ref[:, sl], ca, sa_lo, sa_hi, A_ROT // 2).astype(q_out.dtype)
    for h in range(A_KV_HEADS):
        sl = slice(h * A_HD, (h + 1) * A_HD)
        kr = _rope(ak_ref[:, sl], ca, sa_lo, sa_hi, A_ROT // 2)
        k_out[:, sl] = kr
        kb_out[:, sl] = kr.astype(kb_out.dtype)
    vb_out[...] = av_ref[...].astype(vb_out.dtype)
    low = lax.broadcasted_iota(jnp.int32, (1, LANES), 1) < IDX_DIM
    for p in range(IDX_HEADS * IDX_DIM // LANES):
        x = _rope(iq_ref[:, p * LANES:(p + 1) * LANES], ci, si_lo, si_hi, IDX_ROT // 2)
        hi, lo = _hi_lo(x)
        hi_sw = pltpu.roll(hi, IDX_DIM, 1)
        lo_sw = pltpu.roll(lo, IDX_DIM, 1)
        c0 = 2 * p * IDX_XW
        iqx_out[:, c0:c0 + LANES] = jnp.where(low, hi, hi_sw).astype(iqx_out.dtype)
        iqx_out[:, c0 + LANES:c0 + 2 * LANES] = jnp.where(low, lo, 0.0).astype(iqx_out.dtype)
        iqx_out[:, c0 + 2 * LANES:c0 + 3 * LANES] = jnp.where(low, hi_sw, hi).astype(iqx_out.dtype)
        iqx_out[:, c0 + 3 * LANES:c0 + 4 * LANES] = jnp.where(low, lo_sw, 0.0).astype(iqx_out.dtype)
    ik = _rope(sm_ref[...], ci, si_lo, si_hi, IDX_ROT // 2)
    ik_out[...] = ik[:, SM_IK:SM_IK + IDX_DIM]
    hi, lo = _hi_lo(ik)
    ikx_out[:, 0:LANES] = jnp.where(low, hi, pltpu.roll(lo, IDX_DIM, 1)).astype(ikx_out.dtype)
    ikx_out[:, LANES:2 * LANES] = jnp.where(low, hi, 0.0).astype(ikx_out.dtype)


def _rope_tables(pos, rot, width, reps_valid):
    half = rot // 2
    inv_freq = jnp.exp(jnp.arange(half, dtype=_F32) * (-2.0 * math.log(ROPE_THETA) / rot))
    ang = pos.astype(_F32)[:, None] * inv_freq[None, :]
    cos, sin = jnp.cos(ang), jnp.sin(ang)
    m = pos.shape[0]
    one = jnp.ones((m, width - rot), _F32)
    zero = jnp.zeros((m, width - rot), _F32)
    zh = jnp.zeros((m, half), _F32)
    c_head = jnp.concatenate([cos, cos, one], axis=1)
    lo_head = jnp.concatenate([-sin, zh, zero], axis=1)
    hi_head = jnp.concatenate([zh, sin, zero], axis=1)
    reps = LANES // width
    ident = (jnp.ones((m, width), _F32), jnp.zeros((m, width), _F32), jnp.zeros((m, width), _F32))
    out = []
    for t, idt in zip((c_head, lo_head, hi_head), ident):
        out.append(jnp.concatenate([t if r < reps_valid else idt for r in range(reps)], axis=1))
    return jnp.stack(out)


def _rope_all(proj, small, pos):
    m = proj.shape[0]
    tm = _tile(m, 256, SUBLANES)
    ta = _rope_tables(pos, A_ROT, A_HD, 1)
    ti = _rope_tables(pos, IDX_ROT, IDX_DIM, LANES // IDX_DIM)
    wq, wk, wi = A_HEADS * A_HD, A_KV_HEADS * A_HD, IDX_HEADS * IDX_DIM
    wix = IDX_HEADS * IDX_XW
    out_shape = (jax.ShapeDtypeStruct((m, wq), _BF16),
                 jax.ShapeDtypeStruct((m, wk), _F32),
                 jax.ShapeDtypeStruct((m, wk), _BF16),
                 jax.ShapeDtypeStruct((m, wk), _BF16),
                 jax.ShapeDtypeStruct((m, wix), _BF16),
                 jax.ShapeDtypeStruct((m, IDX_DIM), _F32),
                 jax.ShapeDtypeStruct((m, IDX_XW), _BF16))
    return pl.pallas_call(
        _rope_kernel,
        out_shape=out_shape,
        grid=(m // tm,),
        in_specs=[pl.BlockSpec((tm, wq), lambda i: (i, B_AQ // wq)),
                  pl.BlockSpec((tm, wk), lambda i: (i, B_AK // wk)),
                  pl.BlockSpec((tm, wk), lambda i: (i, B_AV // wk)),
                  pl.BlockSpec((tm, wi), lambda i: (i, B_IQ // wi)),
                  pl.BlockSpec((tm, LANES), lambda i: (i, 0)),
                  pl.BlockSpec((3, tm, LANES), lambda i: (0, i, 0)),
                  pl.BlockSpec((3, tm, LANES), lambda i: (0, i, 0))],
        out_specs=(pl.BlockSpec((tm, wq), lambda i: (i, 0)),
                   pl.BlockSpec((tm, wk), lambda i: (i, 0)),
                   pl.BlockSpec((tm, wk), lambda i: (i, 0)),
                   pl.BlockSpec((tm, wk), lambda i: (i, 0)),
                   pl.BlockSpec((tm, wix), lambda i: (i, 0)),
                   pl.BlockSpec((tm, IDX_DIM), lambda i: (i, 0)),
                   pl.BlockSpec((tm, IDX_XW), lambda i: (i, 0))),
        compiler_params=_params(("parallel",)),
        name="rope",
    )(proj, proj, proj, proj, small, ta, ti)


def _attn_kernel(*refs, TQ, TK, past_len, l_valid, topk):
    q_ref, iqx_ref, sm_ref, k_ref, v_ref, ikx_ref = refs[:6]
    o_ref, key_scr, bias_scr, w_scr = refs[-4:]
    i = pl.program_id(1)
    q_start = past_len + i * TQ
    qpos = q_start + lax.broadcasted_iota(jnp.int32, (TQ, 1), 0)
    qchunk = qpos // CHUNK
    last_vis = jnp.minimum(((q_start + TQ - 1) // CHUNK + 1) * CHUNK, l_valid)
    nkt = (last_vis + TK - 1) // TK
    lane_pos = lax.broadcasted_iota(jnp.int32, (1, TK), 1)

    w_scale = (IDX_DIM ** -0.5) * (IDX_HEADS ** -0.5)
    for h in range(IDX_HEADS):
        w_scr[h] = jnp.broadcast_to(sm_ref[:, SM_IW + h:SM_IW + h + 1] * w_scale, (TQ, LANES))
    lane128 = lax.broadcasted_iota(jnp.int32, (1, LANES), 1)
    sc = min(TK, 2 * LANES)

    def score_tile(kt, carry):
        for cc in range(TK // sc):
            k0 = pl.multiple_of(kt * TK + cc * sc, sc)
            ik_c = ikx_ref[pl.ds(k0, sc), :]
            accs = [jnp.zeros((TQ, LANES), _F32) for _ in range(sc // LANES)]
            for h in range(IDX_HEADS):
                isc = _dot_nt(iqx_ref[:, h * IDX_XW:(h + 1) * IDX_XW], ik_c)
                w_h = w_scr[h]
                for c in range(sc // LANES):
                    accs[c] = accs[c] + jnp.maximum(isc[:, c * LANES:(c + 1) * LANES], 0.0) * w_h
            for c in range(sc // LANES):
                bits = pltpu.bitcast(accs[c] + 0.0, jnp.int32)
                key = jnp.where(bits < 0, bits ^ 0x7FFFFFFF, bits)
                kpos = k0 + c * LANES + lane128
                vis = ((kpos // CHUNK) <= qchunk) & (kpos < l_valid)
                col = cc * sc + c * LANES
                key_scr[kt, :, col:col + LANES] = jnp.where(vis, key, INT_MIN)
        return carry

    lax.fori_loop(0, nkt, score_tile, 0)

    def lane_fold(x):
        out = x[:, 0:LANES]
        for c in range(1, TK // LANES):
            out = out + x[:, c * LANES:(c + 1) * LANES]
        return out

    def count(pred_fn):
        def body(kt, part):
            return part + lane_fold(pred_fn(key_scr[kt], kt).astype(jnp.int32))
        part = lax.fori_loop(0, nkt, body, jnp.zeros((TQ, LANES), jnp.int32))
        return jnp.sum(part, axis=1, keepdims=True)

    def bit_step(it, thr_u):
        cand_u = thr_u | lax.shift_left(jnp.int32(1), 31 - it)
        cand_s = cand_u ^ INT_MIN
        cnt = count(lambda key, kt: key >= cand_s)
        return jnp.where(cnt >= topk, cand_u, thr_u)

    thr = lax.fori_loop(0, 32, bit_step, jnp.zeros((TQ, 1), jnp.int32)) ^ INT_MIN
    n_ge = count(lambda key, kt: key >= thr)

    def tie_search(_):
        need = topk - count(lambda key, kt: key > thr)
        nbits = (key_scr.shape[0] * TK).bit_length()

        def pos_step(it, p):
            cand = p | lax.shift_left(jnp.int32(1), nbits - 1 - it)
            cnt = count(lambda key, kt: (key == thr) & ((kt * TK + lane_pos) < cand))
            return jnp.where(cnt < need, cand, p)

        return lax.fori_loop(0, nbits, pos_step, jnp.zeros((TQ, 1), jnp.int32))

    has_tie = jnp.max(n_ge.astype(_F32)) > topk
    p_last = lax.cond(has_tie, tie_search, lambda _: jnp.full((TQ, 1), 2 ** 31 - 1, jnp.int32), 0)

    def bias_tile(kt, carry):
        key = key_scr[kt]
        sel = ((key > thr) | ((key == thr) & ((kt * TK + lane_pos) <= p_last))) & (key > KEY_NEG_INF)
        bias_scr[kt] = jnp.where(sel, 0.0, NEG_BIG)
        return carry

    lax.fori_loop(0, nkt, bias_tile, 0)

    scale = (A_HD ** -0.5) * math.log2(math.e)
    rows = A_GROUP * TQ
    n_par = 1

    def q_group(g):
        return jnp.concatenate(
            [q_ref[:, (g * A_GROUP + j) * A_HD:(g * A_GROUP + j + 1) * A_HD] for j in range(A_GROUP)], axis=0)

    for g0 in range(0, A_KV_HEADS, n_par):
        qgs = [q_group(g0 + u) for u in range(n_par)]

        def attn_tile(kt, carry, g0=g0, qgs=qgs):
            k0 = pl.multiple_of(kt * TK, TK)
            bias = bias_scr[kt][None]
            s_u = [_dot_nt(qgs[u], k_ref[pl.ds(k0, TK), (g0 + u) * A_HD:(g0 + u + 1) * A_HD])
                   .reshape(A_GROUP, TQ, TK) * scale + bias for u in range(n_par)]
            new = []
            p_u = []
            for u in range(n_par):
                m_i, l_i, acc = carry[u]
                m_new = jnp.maximum(m_i, jnp.max(s_u[u], axis=-1, keepdims=True))
                alpha = jnp.exp2(m_i - m_new)
                p = jnp.exp2(s_u[u] - m_new)
                p_u.append(p.reshape(rows, TK).astype(_BF16))
                new.append((m_new, alpha * l_i + jnp.sum(p, axis=-1, keepdims=True), alpha * acc))
            out = []
            for u in range(n_par):
                v_t = v_ref[pl.ds(k0, TK), (g0 + u) * A_HD:(g0 + u + 1) * A_HD]
                m_new, l_new, acc = new[u]
                out.append((m_new, l_new, acc + _dot(p_u[u], v_t).reshape(A_GROUP, TQ, A_HD)))
            return tuple(out)

        init = tuple((jnp.full((A_GROUP, TQ, 1), NEG_BIG, _F32), jnp.zeros((A_GROUP, TQ, 1), _F32),
                      jnp.zeros((A_GROUP, TQ, A_HD), _F32)) for _ in range(n_par))
        fin = lax.fori_loop(0, nkt, attn_tile, init)
        for u in range(n_par):
            _, l_f, acc_f = fin[u]
            out = acc_f / l_f
            for j in range(A_GROUP):
                hh = (g0 + u) * A_GROUP + j
                o_ref[:, hh * A_HD:(hh + 1) * A_HD] = out[j].astype(o_ref.dtype)


def _attention(q_rot, iq_rot, small, k_all, v_all, ik_all, row0, nseq, T, TQ, TK, lp, past_len, l_valid, o_prev=None):
    nq = T // TQ
    qb0 = row0 // TQ
    topk = min(TOPK_MAX, l_valid // 4)
    wq, wk, wix = A_HEADS * A_HD, A_KV_HEADS * A_HD, IDX_HEADS * IDX_XW
    in_specs = [pl.BlockSpec((TQ, wq), lambda b, i: (qb0 + b * nq + i, 0)),
                pl.BlockSpec((TQ, wix), lambda b, i: (qb0 + b * nq + i, 0)),
                pl.BlockSpec((TQ, LANES), lambda b, i: (qb0 + b * nq + i, 0)),
                pl.BlockSpec((lp, wk), lambda b, i: (b, 0)),
                pl.BlockSpec((lp, wk), lambda b, i: (b, 0)),
                pl.BlockSpec((lp, IDX_XW), lambda b, i: (b, 0))]
    args = [q_rot, iq_rot, small, k_all, v_all, ik_all]
    aliases = {}
    if o_prev is not None:
        aliases = {len(args): 0}
        in_specs.append(pl.BlockSpec(memory_space=pl.ANY))
        args.append(o_prev)
    return pl.pallas_call(
        functools.partial(_attn_kernel, TQ=TQ, TK=TK, past_len=past_len, l_valid=l_valid, topk=topk),
        out_shape=jax.ShapeDtypeStruct((q_rot.shape[0], wq), _BF16),
        grid=(nseq, nq),
        in_specs=in_specs,
        out_specs=pl.BlockSpec((TQ, wq), lambda b, i: (qb0 + b * nq + i, 0)),
        scratch_shapes=[pltpu.VMEM((lp // TK, TQ, TK), jnp.int32),
                        pltpu.VMEM((lp // TK, TQ, TK), _F32),
                        pltpu.VMEM((IDX_HEADS, TQ, LANES), _F32)],
        input_output_aliases=aliases,
        compiler_params=_params(("parallel", "arbitrary")),
        name="sparse_attention",
    )(*args)


def _merge_kernel(hm_ref, ha_ref, wm_ref, wa_ref, gm_ref, ga_ref, o_ref):
    pm = _dot(hm_ref[...], wm_ref[...])
    pa = _dot(ha_ref[...], wa_ref[...])
    o_ref[...] = (jax.nn.sigmoid(gm_ref[...]) * pm + jax.nn.sigmoid(ga_ref[...]) * pa).astype(o_ref.dtype)


def _merge(hm, ha, wm, wa, proj, d):
    m = hm.shape[0]
    tm = _tile(m, 512, SUBLANES)
    tn = _tile(d, 1024)
    gm0 = 0
    ga0 = d // tn
    km, ka = hm.shape[1], ha.shape[1]
    return pl.pallas_call(
        _merge_kernel,
        out_shape=jax.ShapeDtypeStruct((m, d), _BF16),
        grid=(d // tn, m // tm),
        in_specs=[pl.BlockSpec((tm, km), lambda j, i: (i, 0)),
                  pl.BlockSpec((tm, ka), lambda j, i: (i, 0)),
                  pl.BlockSpec((km, tn), lambda j, i: (0, j)),
                  pl.BlockSpec((ka, tn), lambda j, i: (0, j)),
                  pl.BlockSpec((tm, tn), lambda j, i: (i, gm0 + j)),
                  pl.BlockSpec((tm, tn), lambda j, i: (i, ga0 + j))],
        out_specs=pl.BlockSpec((tm, tn), lambda j, i: (i, j)),
        compiler_params=_params(("parallel", "parallel")),
        name="gated_merge",
    )(hm, ha, wm, wa, proj, proj)


def _resnorm_next_kernel(xa_ref, xb_ref, y_ref, g_ref, g2_ref, x1_ref, h_ref, *, n_first):
    def body(x_ref):
        y = y_ref[...]
        ms = jnp.mean(y * y, axis=-1, keepdims=True)
        x1 = x_ref[...] + y * lax.rsqrt(ms + NORM_EPS) * g_ref[...]
        x1_ref[...] = x1
        ms1 = jnp.mean(x1 * x1, axis=-1, keepdims=True)
        h_ref[...] = (x1 * lax.rsqrt(ms1 + NORM_EPS) * g2_ref[...]).astype(h_ref.dtype)

    i = pl.program_id(0)
    pl.when(i < n_first)(lambda: body(xa_ref))
    pl.when(i >= n_first)(lambda: body(xb_ref))


def _resnorm_next(xa, xb, y, g, g_next):
    m, d = y.shape
    tm = _row_tile(xa.shape[0], xb.shape[0])
    n_first = xa.shape[0] // tm
    row = pl.BlockSpec((tm, d), lambda i: (i, 0))
    vec = pl.BlockSpec((1, d), lambda i: (0, 0))
    return pl.pallas_call(
        functools.partial(_resnorm_next_kernel, n_first=n_first),
        out_shape=(jax.ShapeDtypeStruct((m, d), _F32), jax.ShapeDtypeStruct((m, d), _BF16)),
        grid=(m // tm,), in_specs=[*_two_source_specs(tm, d, n_first), row, vec, vec], out_specs=(row, row),
        compiler_params=_params(("parallel",)), name="resnorm_next",
    )(xa, xb, y, g.reshape(1, d), g_next.reshape(1, d))


def _resnorm_kernel(x_ref, y_ref, g_ref, oa_ref, ob_ref, *, n_first):
    y = y_ref[...]
    ms = jnp.mean(y * y, axis=-1, keepdims=True)
    out = x_ref[...] + y * lax.rsqrt(ms + NORM_EPS) * g_ref[...]
    i = pl.program_id(0)

    @pl.when(i < n_first)
    def _():
        oa_ref[...] = out

    @pl.when(i >= n_first)
    def _():
        ob_ref[...] = out


def _resnorm_split(x, y, g, m_a):
    m, d = x.shape
    m_b = m - m_a
    tm = _row_tile(m_a, m_b)
    n_first = m_a // tm
    row = pl.BlockSpec((tm, d), lambda i: (i, 0))
    vec = pl.BlockSpec((1, d), lambda i: (0, 0))
    return pl.pallas_call(
        functools.partial(_resnorm_kernel, n_first=n_first),
        out_shape=(jax.ShapeDtypeStruct((m_a, d), _F32), jax.ShapeDtypeStruct((m_b, d), _F32)),
        grid=(m // tm,), in_specs=[row, row, vec], out_specs=_two_source_specs(tm, d, n_first),
        compiler_params=_params(("arbitrary",)), name="resnorm",
    )(x, y, g.reshape(1, d))


HALO = SUBLANES


def _conv_gelu_gate(cw_ref, cb_ref, prev2, prev1, gate, lin):
    gc = cw_ref[0:1, :] * prev2 + cw_ref[1:2, :] * prev1 + cw_ref[2:3, :] * gate + cb_ref[...]
    c = -2.0 * math.sqrt(2.0 / math.pi)
    act = gc / (1.0 + jnp.exp(gc * (c + (c * 0.044715) * (gc * gc))))
    return act * lin


def _ffn_up_first_kernel(h_ref, wg_ref, wl_ref, cw_ref, cb_ref, z_ref, tail_ref, wg_bf, wl_bf, g_scr,
                         *, tm, seq_len):
    i = pl.program_id(1)
    tn = z_ref.shape[1]

    @pl.when(i == 0)
    def _():
        wg_bf[...] = wg_ref[...].astype(wg_bf.dtype)
        wl_bf[...] = wl_ref[...].astype(wl_bf.dtype)

    @pl.when((i * tm) % seq_len == 0)
    def _():
        g_scr[0:HALO, :] = jnp.zeros((HALO, tn), _F32)

    h = h_ref[...]
    gate = _dot(h, wg_bf[...])
    lin = _dot(h, wl_bf[...])
    head = g_scr[0:HALO, :]
    row = lax.broadcasted_iota(jnp.int32, (SUBLANES, 1), 0)

    def shifted(k):
        rolled = pltpu.roll(gate, k, 0)
        first = jnp.where(row < k, pltpu.roll(head, k, 0), rolled[0:SUBLANES, :])
        return jnp.concatenate([first, rolled[SUBLANES:, :]], axis=0)

    zed = _conv_gelu_gate(cw_ref, cb_ref, shifted(2), shifted(1), gate, lin)
    z_ref[...] = zed.astype(z_ref.dtype)
    g_scr[0:HALO, :] = gate[tm - HALO:tm, :]
    tail_ref[...] = gate[tm - SUBLANES:tm, :]


def _ffn_up_state_kernel(h_ref, wg_ref, wl_ref, cw_ref, cb_ref, init_ref, z_any, z_ref, tail_ref, *, tm, seq_len):
    tn = z_ref.shape[1]
    h = h_ref[...]
    gate = _dot(h, wg_ref[...].astype(_BF16))
    lin = _dot(h, wl_ref[...].astype(_BF16))
    nseg = tm // seq_len
    init0 = jnp.concatenate([jnp.broadcast_to(init_ref[s, 0:1, :], (seq_len, tn)) for s in range(nseg)], axis=0)
    init1 = jnp.concatenate([jnp.broadcast_to(init_ref[s, 1:2, :], (seq_len, tn)) for s in range(nseg)], axis=0)
    t = lax.broadcasted_iota(jnp.int32, (tm, 1), 0) % seq_len
    prev1 = jnp.where(t == 0, init1, pltpu.roll(gate, 1, 0))
    prev2 = jnp.where(t == 0, init0, jnp.where(t == 1, init1, pltpu.roll(gate, 2, 0)))
    z_ref[...] = _conv_gelu_gate(cw_ref, cb_ref, prev2, prev1, gate, lin).astype(z_ref.dtype)
    for s in range(nseg):
        tail_ref[s * SUBLANES:(s + 1) * SUBLANES, :] = gate[(s + 1) * seq_len - SUBLANES:(s + 1) * seq_len, :]


def _conv_state(tails, nseq, segs_per_seq, dff):
    tails = tails.reshape(nseq * segs_per_seq, SUBLANES, dff)
    last = tails[segs_per_seq - 1::segs_per_seq, SUBLANES - (CONV_W - 1):, :]
    return last.reshape(nseq, CONV_W - 1, dff)


def _ffn_up_specs(d, tn, nj, h_spec):
    return [h_spec,
            pl.BlockSpec((d, tn), lambda j, i: (0, j)),
            pl.BlockSpec((d, tn), lambda j, i: (0, nj + j)),
            pl.BlockSpec((CONV_W, tn), lambda j, i: (0, j)),
            pl.BlockSpec((1, tn), lambda j, i: (0, j))]


def _ffn_up_first(hf, w_up, conv_w, conv_b, nrows, seq_len):
    m_all, d = hf.shape
    dff = w_up.shape[1] // 2
    tm = _tile(seq_len, 1024, SUBLANES)
    tn = _tile(dff, 256)
    nj = dff // tn
    nt = nrows // tm
    specs = _ffn_up_specs(d, tn, nj, pl.BlockSpec((tm, d), lambda j, i: (i, 0)))
    z, tails = pl.pallas_call(
        functools.partial(_ffn_up_first_kernel, tm=tm, seq_len=seq_len),
        out_shape=(jax.ShapeDtypeStruct((m_all, dff), _BF16),
                   jax.ShapeDtypeStruct((nt * SUBLANES, dff), _F32)),
        grid=(nj, nt),
        in_specs=specs,
        out_specs=(pl.BlockSpec((tm, tn), lambda j, i: (i, j)),
                   pl.BlockSpec((SUBLANES, tn), lambda j, i: (i, j))),
        scratch_shapes=[pltpu.VMEM((d, tn), _BF16), pltpu.VMEM((d, tn), _BF16),
                        pltpu.VMEM((HALO, tn), _F32)],
        compiler_params=_params(("parallel", "arbitrary")),
        name="ffn_up_conv_first",
    )(hf, w_up, w_up, conv_w, conv_b.reshape(1, dff))
    return z, _conv_state(tails, nrows // seq_len, seq_len // tm, dff)


def _ffn_up_state(hf, w_up, conv_w, conv_b, row0, nrows, seq_len, init, z_prev):
    m_all, d = hf.shape
    dff = w_up.shape[1] // 2
    tm = _tile(nrows, 512, seq_len)
    tn = _tile(dff, 256)
    nj = dff // tn
    nseg = tm // seq_len
    rb0 = row0 // tm
    specs = _ffn_up_specs(d, tn, nj, pl.BlockSpec((tm, d), lambda j, i: (rb0 + i, 0)))
    specs += [pl.BlockSpec((nseg, CONV_W - 1, tn), lambda j, i: (i, 0, j)), pl.BlockSpec(memory_space=pl.ANY)]
    z, tails = pl.pallas_call(
        functools.partial(_ffn_up_state_kernel, tm=tm, seq_len=seq_len),
        out_shape=(jax.ShapeDtypeStruct((m_all, dff), _BF16),
                   jax.ShapeDtypeStruct((nrows // seq_len * SUBLANES, dff), _F32)),
        grid=(nj, nrows // tm),
        in_specs=specs,
        out_specs=(pl.BlockSpec((tm, tn), lambda j, i: (rb0 + i, j)),
                   pl.BlockSpec((nseg * SUBLANES, tn), lambda j, i: (i, j))),
        input_output_aliases={6: 0},
        compiler_params=_params(("parallel", "arbitrary")),
        name="ffn_up_conv_state",
    )(hf, w_up, w_up, conv_w, conv_b.reshape(1, dff), init, z_prev)
    return z, _conv_state(tails, nrows // seq_len, 1, dff)


def _layer(xa, xb, geom, cache, state, w):
    (B, S, Bd, Td, P) = geom
    (cache_k, cache_v, cache_ik) = cache
    (state_C, state_n, state_m, state_conv) = state
    (w_in, b_igate, b_fgate, g_mhnorm, w_proj_m, w_proj_a, w_out,
     g_pre_mix, g_post_mix, g_pre_ffn, g_post_ffn, w_up, conv_w, conv_b, w_down) = w
    mp, d = xa.shape
    dff = w_down.shape[0]
    wk = A_KV_HEADS * A_HD

    c_mi = A_END
    c_aq = c_mi + 2 * M_HEADS
    c_ik = c_aq + B_END
    c_iw = c_ik + IDX_DIM
    c_gm = c_iw + IDX_HEADS
    w_t = jnp.swapaxes(w_in, 0, 1).astype(_F32)
    w_small_t = jnp.concatenate([w_t[c_ik:c_iw], w_t[c_mi:c_aq], w_t[c_iw:c_gm],
                                 jnp.zeros((LANES - SM_END, d), _F32)], axis=0)
    gate_bias = jnp.concatenate([jnp.zeros((SM_MI,), _F32), b_igate.astype(_F32), b_fgate.astype(_F32),
                                 jnp.zeros((LANES - SM_IW,), _F32)]).reshape(1, LANES)

    hn = _rmsnorm_cast(xa, xb, g_pre_mix)
    proj_a = _matmul_w32(hn, w_t, 0, A_END, _F32, transposed=True, name="in_proj_mlstm")
    proj_b = _matmul_w32(hn, w_t, c_aq, B_END, _F32, transposed=True, name="in_proj_attn")
    proj_c = _matmul_w32(hn, w_t, c_gm, 2 * d, _F32, transposed=True, name="in_proj_gates")
    small = _matmul_w32(hn, w_small_t, 0, LANES, _F32, transposed=True, name="in_proj_small")

    lp_chunk = _tile(S, 256, CHUNK)
    hm_p, c_p, n_p, m_p = _mlstm(proj_a, small, gate_bias, g_mhnorm, 0, B, S, lp_chunk, None)
    s0 = jnp.zeros((Bd * M_HEADS, SUBLANES, M_QK), _F32)
    s0 = s0.at[:, 0, :].set(state_n.reshape(Bd * M_HEADS, M_QK).astype(_F32))
    s0 = s0.at[:, 1, :].set(jnp.broadcast_to(state_m.reshape(Bd * M_HEADS, 1).astype(_F32), (Bd * M_HEADS, M_QK)))
    hm, c_s, n_s, m_s = _mlstm(proj_a, small, gate_bias, g_mhnorm, mp, Bd, Td, min(CHUNK, Td),
                               (state_C.astype(_F32), s0), hm_p)

    pos = jnp.concatenate([jnp.tile(jnp.arange(S, dtype=jnp.int32), B),
                           jnp.tile(P + jnp.arange(Td, dtype=jnp.int32), Bd)])
    q_rot, k_rot, k_bf, v_bf, iqx, ik_rot, ikx = _rope_all(proj_b, small, pos)
    tq_p = _tile(S, 256, CHUNK)
    tk_p = _tile(S, 512)
    ha_p = _attention(q_rot, iqx, small, k_bf, v_bf, ikx, 0, B, S, tq_p, tk_p, S, 0, S)
    l_s = P + Td
    tk_s = 256
    lp_s = -(-l_s // tk_s) * tk_s
    pad = lp_s - l_s
    k_s = jnp.concatenate([cache_k.reshape(Bd, P, wk).astype(_BF16), k_bf[mp:].reshape(Bd, Td, wk),
                           jnp.zeros((Bd, pad, wk), _BF16)], axis=1).reshape(Bd * lp_s, wk)
    v_s = jnp.concatenate([cache_v.reshape(Bd, P, wk).astype(_BF16), v_bf[mp:].reshape(Bd, Td, wk),
                           jnp.zeros((Bd, pad, wk), _BF16)], axis=1).reshape(Bd * lp_s, wk)
    cik = cache_ik.astype(_F32)
    cik_hi = cik.astype(_BF16)
    cik_lo = (cik - cik_hi.astype(_F32)).astype(_BF16)
    cikx = jnp.concatenate([cik_hi, cik_lo, cik_hi, jnp.zeros_like(cik_hi)], axis=-1)
    ikx_s = jnp.concatenate([cikx, ikx[mp:].reshape(Bd, Td, IDX_XW),
                             jnp.zeros((Bd, pad, IDX_XW), _BF16)], axis=1).reshape(Bd * lp_s, IDX_XW)
    ha = _attention(q_rot, iqx, small, k_s, v_s, ikx_s, mp, Bd, Td, Td, tk_s, lp_s, P, l_s, ha_p)

    mix = _merge(hm, ha, w_proj_m.astype(_BF16), w_proj_a.astype(_BF16), proj_c, d)
    y1 = _matmul_w32(mix, w_out.astype(_F32), 0, d, _F32, transposed=False, name="out_proj")
    x1, hf = _resnorm_next(xa, xb, y1, g_post_mix, g_pre_ffn)

    w_up = w_up.astype(_F32)
    z_p, conv_p = _ffn_up_first(hf, w_up, conv_w, conv_b, mp, S)
    z, conv_s = _ffn_up_state(hf, w_up, conv_w, conv_b, mp, Bd * Td, Td, state_conv.astype(_F32), z_p)
    y2 = _matmul(z, w_down.astype(_BF16), _F32, name="down_proj")
    x2 = _resnorm_split(x1, y2, g_post_ffn, mp)

    av = proj_b[:, B_AV:B_AV + wk]
    outs_p = (k_rot[:mp].reshape(B, S, A_KV_HEADS, A_HD), av[:mp].reshape(B, S, A_KV_HEADS, A_HD),
              ik_rot[:mp].reshape(B, S, IDX_DIM), c_p, n_p, m_p, conv_p)
    outs_s = (k_rot[mp:].reshape(Bd, Td, A_KV_HEADS, A_HD), av[mp:].reshape(Bd, Td, A_KV_HEADS, A_HD),
              ik_rot[mp:].reshape(Bd, Td, IDX_DIM), c_s, n_s, m_s, conv_s)
    return x2, outs_p, outs_s


def kernel(x_prompt, x_sample, cache_k, cache_v, cache_idx_k, state_C, state_n, state_m, state_conv,
           w_in, b_igate, b_fgate, g_mhnorm, w_proj_m, w_proj_a, w_out,
           g_pre_mix, g_post_mix, g_pre_ffn, g_post_ffn, w_up, conv_w, conv_b, w_down):
    B, S, d = x_prompt.shape
    Bd, Td, _ = x_sample.shape
    P = cache_k.shape[2]
    depth = w_in.shape[0]
    mp = B * S
    xa, xb = x_prompt.reshape(mp, d), x_sample.reshape(Bd * Td, d)
    all_p, all_s = [], []
    for l in range(depth):
        w = (w_in[l], b_igate[l], b_fgate[l], g_mhnorm[l], w_proj_m[l], w_proj_a[l], w_out[l],
             g_pre_mix[l], g_post_mix[l], g_pre_ffn[l], g_post_ffn[l], w_up[l], conv_w[l], conv_b[l], w_down[l])
        (xa, xb), outs_p, outs_s = _layer(xa, xb, (B, S, Bd, Td, P), (cache_k[l], cache_v[l], cache_idx_k[l]),
                                          (state_C[l], state_n[l], state_m[l], state_conv[l]), w)
        all_p.append(outs_p)
        all_s.append(outs_s)

    def stk(outs, i):
        return jnp.stack([o[i] for o in outs])

    yp = xa.reshape(B, S, d)
    ys = xb.reshape(Bd, Td, d)
    return (yp, ys) + tuple(stk(all_p, i) for i in range(7)) + tuple(stk(all_s, i) for i in range(7))
```

```python
import functools
import math

import jax
import jax.numpy as jnp
from jax import lax
from jax.experimental import pallas as pl
from jax.experimental.pallas import tpu as pltpu

CHUNK = 64
NORM_EPS = 1e-6
ROPE_THETA = 500000.0
M_HEADS = 8
M_QK = 128
M_V = 256
A_HEADS = 16
A_KV_HEADS = 4
A_GROUP = A_HEADS // A_KV_HEADS
A_HD = 128
A_ROT = A_HD // 4
IDX_HEADS = 16
IDX_DIM = 64
IDX_ROT = IDX_DIM // 4
TOPK_MAX = 256
CONV_W = 3

LANES = 128
SUBLANES = 8
VMEM_LIMIT_BYTES = 52 * 1024 * 1024

MM_ROWS = 512
MM_COLS = 1024
DOWN_ROWS = 256
NORM_ROWS = 256
FFN_ROWS = 1024
FFN_STATE_ROWS = 512
FFN_COLS = 256
MLSTM_CHUNK = 256
ATTN_TQ = 256
ATTN_TK = 512
ATTN_TK_STATE = 256
XPOSE_SQ = 512

A_MQ = 0
A_MK = A_MQ + M_HEADS * M_QK
A_MV = A_MK + M_HEADS * M_QK
A_MO = A_MV + M_HEADS * M_V
A_END = A_MO + M_HEADS * M_V
B_AQ = 0
B_AK = B_AQ + A_HEADS * A_HD
B_AV = B_AK + A_KV_HEADS * A_HD
B_IQ = B_AV + A_KV_HEADS * A_HD
B_END = B_IQ + IDX_HEADS * IDX_DIM
SM_IK = 0
SM_MI = SM_IK + IDX_DIM
SM_MF = SM_MI + M_HEADS
SM_IW = SM_MF + M_HEADS
SM_END = SM_IW + IDX_HEADS
IDX_XW = 4 * IDX_DIM

INT_MIN = -2 ** 31
KEY_NEG_INF = INT_MIN + 0x7FFFFF
NEG_BIG = -1e30

_BF16 = jnp.bfloat16
_F32 = jnp.float32


def _tile(dim, target, quantum=LANES):
    if dim <= target:
        return dim
    t = (target // quantum) * quantum
    while t >= quantum:
        if dim % t == 0:
            return t
        t -= quantum
    return dim


def _params(sem):
    return pltpu.CompilerParams(dimension_semantics=sem, vmem_limit_bytes=VMEM_LIMIT_BYTES)


def _dot(a, b):
    return jnp.dot(a, b, preferred_element_type=_F32)


def _dot_nt(a, b):
    return lax.dot_general(a, b, (((1,), (1,)), ((), ())), preferred_element_type=_F32)


def _dot_tn(a, b):
    return lax.dot_general(a, b, (((0,), (0,)), ((), ())), preferred_element_type=_F32)


def _mm_w32_kernel(a_ref, w_hbm, o_ref, w_f32, w_bf, sem, *, off0, nj, transposed):
    j = pl.program_id(0)
    kd, tn = w_bf.shape

    def tile_copy(jj):
        if transposed:
            src = w_hbm.at[pl.ds(pl.multiple_of(off0 + jj * tn, SUBLANES), tn), :]
        else:
            src = w_hbm.at[:, pl.ds(pl.multiple_of(off0 + jj * tn, LANES), tn)]
        return pltpu.make_async_copy(src, w_f32, sem)

    @pl.when(pl.program_id(1) == 0)
    def _():
        @pl.when(j == 0)
        def _():
            tile_copy(0).start()

        tile_copy(j).wait()
        if transposed:
            sq = math.gcd(tn, kd, XPOSE_SQ)
            for r in range(0, tn, sq):
                for c in range(0, kd, sq):
                    w_bf[c:c + sq, r:r + sq] = w_f32[r:r + sq, c:c + sq].T.astype(w_bf.dtype)
        else:
            w_bf[...] = w_f32[...].astype(w_bf.dtype)

        @pl.when(j + 1 < nj)
        def _():
            tile_copy(j + 1).start()

    o_ref[...] = _dot(a_ref[...], w_bf[...]).astype(o_ref.dtype)


def _matmul_w32(a, w, off0, n, out_dtype, *, transposed, name="matmul_w32"):
    m, kd = a.shape
    tm = _tile(m, MM_ROWS, SUBLANES)
    tn = _tile(n, MM_COLS)
    nj = n // tn
    return pl.pallas_call(
        functools.partial(_mm_w32_kernel, off0=off0, nj=nj, transposed=transposed),
        out_shape=jax.ShapeDtypeStruct((m, n), out_dtype),
        grid=(nj, m // tm),
        in_specs=[pl.BlockSpec((tm, kd), lambda j, i: (i, 0)),
                  pl.BlockSpec(memory_space=pl.ANY)],
        out_specs=pl.BlockSpec((tm, tn), lambda j, i: (i, j)),
        scratch_shapes=[pltpu.VMEM((tn, kd) if transposed else (kd, tn), _F32), pltpu.VMEM((kd, tn), _BF16),
                        pltpu.SemaphoreType.DMA],
        compiler_params=_params(("arbitrary", "arbitrary")),
        name=name,
    )(a, w)


def _two_source_specs(tm, d, n_first):
    return (pl.BlockSpec((tm, d), lambda i: (jnp.minimum(i, n_first - 1), 0)),
            pl.BlockSpec((tm, d), lambda i: (jnp.maximum(i - n_first, 0), 0)))


def _rmsnorm_kernel(xa_ref, xb_ref, g_ref, o_ref, *, n_first):
    def body(x_ref):
        x = x_ref[...]
        ms = jnp.mean(x * x, axis=-1, keepdims=True)
        o_ref[...] = (x * lax.rsqrt(ms + NORM_EPS) * g_ref[...]).astype(o_ref.dtype)

    i = pl.program_id(0)
    pl.when(i < n_first)(lambda: body(xa_ref))
    pl.when(i >= n_first)(lambda: body(xb_ref))


def _row_tile(m_a, m_b):
    return _tile(math.gcd(m_a, m_b), NORM_ROWS, SUBLANES)


def _rmsnorm_cast(xa, xb, g):
    (m_a, d), m_b = xa.shape, xb.shape[0]
    tm = _row_tile(m_a, m_b)
    n_first = m_a // tm
    return pl.pallas_call(
        functools.partial(_rmsnorm_kernel, n_first=n_first),
        out_shape=jax.ShapeDtypeStruct((m_a + m_b, d), _BF16),
        grid=((m_a + m_b) // tm,),
        in_specs=[*_two_source_specs(tm, d, n_first), pl.BlockSpec((1, d), lambda i: (0, 0))],
        out_specs=pl.BlockSpec((tm, d), lambda i: (i, 0)),
        compiler_params=_params(("parallel",)),
        name="rmsnorm_cast",
    )(xa, xb, g.reshape(1, d))


def _mm_kernel(a_ref, b_ref, o_ref):
    o_ref[...] = _dot(a_ref[...], b_ref[...]).astype(o_ref.dtype)


def _matmul(a, b, out_dtype, *, name="matmul"):
    m, kd = a.shape
    _, n = b.shape
    tm = _tile(m, DOWN_ROWS, SUBLANES)
    tn = _tile(n, MM_COLS)
    return pl.pallas_call(
        _mm_kernel,
        out_shape=jax.ShapeDtypeStruct((m, n), out_dtype),
        grid=(n // tn, m // tm),
        in_specs=[pl.BlockSpec((tm, kd), lambda j, i: (i, 0)),
                  pl.BlockSpec((kd, tn), lambda j, i: (0, j), pipeline_mode=pl.Buffered(1))],
        out_specs=pl.BlockSpec((tm, tn), lambda j, i: (i, j)),
        compiler_params=_params(("parallel", "parallel")),
        name=name,
    )(a, b)


def _mlstm_kernel(*refs, L, has_state, has_alias):
    q_ref, k_ref, v_ref, o_ref, g_ref, gb_ref, gn_ref = refs[:7]
    if has_state:
        c0_ref, s0_ref = refs[7:9]
    h_ref, c_out_ref, s_out_ref, c_scr, n_scr, m_scr = refs[7 + 2 * has_state + has_alias:]
    head = pl.program_id(1)
    c = pl.program_id(2)
    nc = pl.num_programs(2)

    @pl.when(c == 0)
    def _():
        if has_state:
            c_scr[...] = c0_ref[0, 0]
            n_scr[...] = s0_ref[0, 0:1, :]
            m_scr[...] = s0_ref[0, 1:2, :]
        else:
            c_scr[...] = jnp.zeros_like(c_scr)
            n_scr[...] = jnp.zeros_like(n_scr)
            m_scr[...] = jnp.zeros_like(m_scr)

    gates = g_ref[...] + gb_ref[...]
    lane = lax.broadcasted_iota(jnp.int32, gates.shape, 1)
    ig_col = jnp.sum(jnp.where(lane == SM_MI + head, gates, 0.0), axis=1, keepdims=True)
    mf_col = jnp.sum(jnp.where(lane == SM_MF + head, gates, 0.0), axis=1, keepdims=True)
    lf_col = jnp.minimum(mf_col, 0.0) - jnp.log1p(jnp.exp(-jnp.abs(mf_col)))

    ri = lax.broadcasted_iota(jnp.int32, (L, L), 0)
    ci = lax.broadcasted_iota(jnp.int32, (L, L), 1)
    eye = ri == ci
    tril = ci <= ri
    lf_row = jnp.sum(jnp.where(eye, lf_col, 0.0), axis=0, keepdims=True)
    ig_row = jnp.sum(jnp.where(eye, ig_col, 0.0), axis=0, keepdims=True)
    b_col = jnp.sum(jnp.where(tril, lf_row, 0.0), axis=1, keepdims=True)
    b_row = jnp.sum(jnp.where(ri <= ci, lf_col, 0.0), axis=0, keepdims=True)
    logw = jnp.where(tril, b_col - b_row + ig_row, -jnp.inf)

    m_prev = m_scr[:, 0:1]
    inter = b_col + m_prev
    m_t = jnp.maximum(inter, jnp.max(logw, axis=1, keepdims=True))
    a = jnp.exp(inter - m_t)
    sw = jnp.exp(logw - m_t)

    q = q_ref[...]
    k = k_ref[...] * (M_QK ** -0.5)
    v = v_ref[...]
    qb = q.astype(_BF16)
    kb = k.astype(_BF16)
    s = _dot_nt(qb, kb) * sw
    c_old = c_scr[...]
    n_old = n_scr[...]
    num = a * _dot_nt(qb, c_old.astype(_BF16)) + _dot(s.astype(_BF16), v.astype(_BF16))
    den = a * jnp.sum(q * n_old, axis=1, keepdims=True) + jnp.sum(s, axis=1, keepdims=True)
    h = num / jnp.maximum(jnp.abs(den), jnp.exp(-m_t))

    m_new = m_t[L - 1:L, :]
    b_last = b_col[L - 1:L, :]
    g_col = jnp.exp(b_last - b_col + ig_col - m_new)
    decay = jnp.exp(b_last + m_prev - m_new)
    c_new = decay * c_old + _dot_tn((v * g_col).astype(_BF16), kb)
    n_new = decay * n_old + jnp.sum(g_col * k, axis=0, keepdims=True)
    c_scr[...] = c_new
    n_scr[...] = n_new
    m_scr[...] = jnp.broadcast_to(m_new, m_scr.shape)

    ms = jnp.mean(h * h, axis=1, keepdims=True)
    y = h * lax.rsqrt(ms + NORM_EPS) * gn_ref[0]
    h_ref[...] = (y * jax.nn.sigmoid(o_ref[...])).astype(h_ref.dtype)

    @pl.when(c == nc - 1)
    def _():
        c_out_ref[0, 0] = c_new
        s_out_ref[0] = jnp.zeros(s_out_ref.shape[1:], _F32)
        s_out_ref[0, 0:1, :] = n_new
        s_out_ref[0, 1:2, :] = jnp.broadcast_to(m_new, (1, M_QK))


def _mlstm(proj, small, gate_bias, g_mhnorm, row0, nseq, T, L, state, h_prev=None):
    nc = T // L
    rb0 = row0 // L
    has_state = state is not None
    has_alias = h_prev is not None

    def rows(b, h, c):
        return rb0 + b * nc + c

    in_specs = [
        pl.BlockSpec((L, M_QK), lambda b, h, c: (rows(b, h, c), A_MQ // M_QK + h)),
        pl.BlockSpec((L, M_QK), lambda b, h, c: (rows(b, h, c), A_MK // M_QK + h)),
        pl.BlockSpec((L, M_V), lambda b, h, c: (rows(b, h, c), A_MV // M_V + h)),
        pl.BlockSpec((L, M_V), lambda b, h, c: (rows(b, h, c), A_MO // M_V + h)),
        pl.BlockSpec((L, LANES), lambda b, h, c: (rows(b, h, c), 0)),
        pl.BlockSpec((1, LANES), lambda b, h, c: (0, 0)),
        pl.BlockSpec((1, 1, M_V), lambda b, h, c: (h, 0, 0)),
    ]
    args = [proj, proj, proj, proj, small, gate_bias, g_mhnorm.reshape(M_HEADS, 1, M_V)]
    if has_state:
        c0, s0 = state
        in_specs += [pl.BlockSpec((1, 1, M_V, M_QK), lambda b, h, c: (b, h, 0, 0)),
                     pl.BlockSpec((1, SUBLANES, M_QK), lambda b, h, c: (b * M_HEADS + h, 0, 0))]
        args += [c0, s0]
    aliases = {}
    if has_alias:
        aliases = {len(args): 0}
        in_specs.append(pl.BlockSpec(memory_space=pl.ANY))
        args.append(h_prev)
    out_shape = (jax.ShapeDtypeStruct((proj.shape[0], M_HEADS * M_V), _BF16),
                 jax.ShapeDtypeStruct((nseq, M_HEADS, M_V, M_QK), _F32),
                 jax.ShapeDtypeStruct((nseq * M_HEADS, SUBLANES, M_QK), _F32))
    out_specs = (pl.BlockSpec((L, M_V), lambda b, h, c: (rows(b, h, c), h)),
                 pl.BlockSpec((1, 1, M_V, M_QK), lambda b, h, c: (b, h, 0, 0)),
                 pl.BlockSpec((1, SUBLANES, M_QK), lambda b, h, c: (b * M_HEADS + h, 0, 0)))
    hm, c_new, stats = pl.pallas_call(
        functools.partial(_mlstm_kernel, L=L, has_state=has_state, has_alias=has_alias),
        out_shape=out_shape,
        grid=(nseq, M_HEADS, nc),
        in_specs=in_specs,
        out_specs=out_specs,
        scratch_shapes=[pltpu.VMEM((M_V, M_QK), _F32), pltpu.VMEM((1, M_QK), _F32),
                        pltpu.VMEM((1, M_QK), _F32)],
        input_output_aliases=aliases,
        compiler_params=_params(("parallel", "parallel", "arbitrary")),
        name="mlstm",
    )(*args)
    n_new = stats[:, 0, :].reshape(nseq, M_HEADS, M_QK)
    m_new = stats[:, 1, 0].reshape(nseq, M_HEADS)
    return hm, c_new, n_new, m_new


def _rope(x, cos, sin_lo, sin_hi, half):
    n = x.shape[-1]
    return (x * cos + pltpu.roll(x, n - half, 1) * sin_lo + pltpu.roll(x, half, 1) * sin_hi)


def _hi_lo(x):
    hi = x.astype(_BF16).astype(_F32)
    return hi, x - hi


def _rope_kernel(aq_ref, ak_ref, av_ref, iq_ref, sm_ref, ta_ref, ti_ref,
                 q_out, k_out, kb_out, vb_out, iqx_out, ik_out, ikx_out):
    ca, sa_lo, sa_hi = ta_ref[0], ta_ref[1], ta_ref[2]
    ci, si_lo, si_hi = ti_ref[0], ti_ref[1], ti_ref[2]
    for h in range(A_HEADS):
        sl = slice(h * A_HD, (h + 1) * A_HD)
        q_out[:, sl] = _rope(aq_ref[:, sl], ca, sa_lo, sa_hi, A_ROT // 2).astype(q_out.dtype)
    for h in range(A_KV_HEADS):
        sl = slice(h * A_HD, (h + 1) * A_HD)
        kr = _rope(ak_ref[:, sl], ca, sa_lo, sa_hi, A_ROT // 2)
        k_out[:, sl] = kr
        kb_out[:, sl] = kr.astype(kb_out.dtype)
    vb_out[...] = av_ref[...].astype(vb_out.dtype)
    low = lax.broadcasted_iota(jnp.int32, (1, LANES), 1) < IDX_DIM
    for p in range(IDX_HEADS * IDX_DIM // LANES):
        x = _rope(iq_ref[:, p * LANES:(p + 1) * LANES], ci, si_lo, si_hi, IDX_ROT // 2)
        hi, lo = _hi_lo(x)
        hi_sw = pltpu.roll(hi, IDX_DIM, 1)
        lo_sw = pltpu.roll(lo, IDX_DIM, 1)
        c0 = 2 * p * IDX_XW
        iqx_out[:, c0:c0 + LANES] = jnp.where(low, hi, hi_sw).astype(iqx_out.dtype)
        iqx_out[:, c0 + LANES:c0 + 2 * LANES] = jnp.where(low, lo, 0.0).astype(iqx_out.dtype)
        iqx_out[:, c0 + 2 * LANES:c0 + 3 * LANES] = jnp.where(low, hi_sw, hi).astype(iqx_out.dtype)
        iqx_out[:, c0 + 3 * LANES:c0 + 4 * LANES] = jnp.where(low, lo_sw, 0.0).astype(iqx_out.dtype)
    ik = _rope(sm_ref[...], ci, si_lo, si_hi, IDX_ROT // 2)
    ik_out[...] = ik[:, SM_IK:SM_IK + IDX_DIM]
    hi, lo = _hi_lo(ik)
    ikx_out[:, 0:LANES] = jnp.where(low, hi, pltpu.roll(lo, IDX_DIM, 1)).astype(ikx_out.dtype)
    ikx_out[:, LANES:2 * LANES] = jnp.where(low, hi, 0.0).astype(ikx_out.dtype)


def _rope_tables(pos, rot, width, reps_valid):
    half = rot // 2
    inv_freq = jnp.exp(jnp.arange(half, dtype=_F32) * (-2.0 * math.log(ROPE_THETA) / rot))
    ang = pos.astype(_F32)[:, None] * inv_freq[None, :]
    cos, sin = jnp.cos(ang), jnp.sin(ang)
    m = pos.shape[0]
    one = jnp.ones((m, width - rot), _F32)
    zero = jnp.zeros((m, width - rot), _F32)
    zh = jnp.zeros((m, half), _F32)
    c_head = jnp.concatenate([cos, cos, one], axis=1)
    lo_head = jnp.concatenate([-sin, zh, zero], axis=1)
    hi_head = jnp.concatenate([zh, sin, zero], axis=1)
    reps = LANES // width
    ident = (jnp.ones((m, width), _F32), jnp.zeros((m, width), _F32), jnp.zeros((m, width), _F32))
    out = []
    for t, idt in zip((c_head, lo_head, hi_head), ident):
        out.append(jnp.concatenate([t if r < reps_valid else idt for r in range(reps)], axis=1))
    return jnp.stack(out)


def _rope_all(proj, small, pos):
    m = proj.shape[0]
    tm = _tile(m, NORM_ROWS, SUBLANES)
    ta = _rope_tables(pos, A_ROT, A_HD, 1)
    ti = _rope_tables(pos, IDX_ROT, IDX_DIM, LANES // IDX_DIM)
    wq, wk, wi = A_HEADS * A_HD, A_KV_HEADS * A_HD, IDX_HEADS * IDX_DIM
    wix = IDX_HEADS * IDX_XW
    out_shape = (jax.ShapeDtypeStruct((m, wq), _BF16),
                 jax.ShapeDtypeStruct((m, wk), _F32),
                 jax.ShapeDtypeStruct((m, wk), _BF16),
                 jax.ShapeDtypeStruct((m, wk), _BF16),
                 jax.ShapeDtypeStruct((m, wix), _BF16),
                 jax.ShapeDtypeStruct((m, IDX_DIM), _F32),
                 jax.ShapeDtypeStruct((m, IDX_XW), _BF16))
    return pl.pallas_call(
        _rope_kernel,
        out_shape=out_shape,
        grid=(m // tm,),
        in_specs=[pl.BlockSpec((tm, wq), lambda i: (i, B_AQ // wq)),
                  pl.BlockSpec((tm, wk), lambda i: (i, B_AK // wk)),
                  pl.BlockSpec((tm, wk), lambda i: (i, B_AV // wk)),
                  pl.BlockSpec((tm, wi), lambda i: (i, B_IQ // wi)),
                  pl.BlockSpec((tm, LANES), lambda i: (i, 0)),
                  pl.BlockSpec((3, tm, LANES), lambda i: (0, i, 0)),
                  pl.BlockSpec((3, tm, LANES), lambda i: (0, i, 0))],
        out_specs=(pl.BlockSpec((tm, wq), lambda i: (i, 0)),
                   pl.BlockSpec((tm, wk), lambda i: (i, 0)),
                   pl.BlockSpec((tm, wk), lambda i: (i, 0)),
                   pl.BlockSpec((tm, wk), lambda i: (i, 0)),
                   pl.BlockSpec((tm, wix), lambda i: (i, 0)),
                   pl.BlockSpec((tm, IDX_DIM), lambda i: (i, 0)),
                   pl.BlockSpec((tm, IDX_XW), lambda i: (i, 0))),
        compiler_params=_params(("parallel",)),
        name="rope",
    )(proj, proj, proj, proj, small, ta, ti)


def _attn_kernel(*refs, TQ, TK, past_len, l_valid, topk):
    q_ref, iqx_ref, sm_ref, k_ref, v_ref, ikx_ref = refs[:6]
    o_ref, key_scr, bias_scr, w_scr = refs[-4:]
    i = pl.program_id(1)
    q_start = past_len + i * TQ
    qpos = q_start + lax.broadcasted_iota(jnp.int32, (TQ, 1), 0)
    qchunk = qpos // CHUNK
    last_vis = jnp.minimum(((q_start + TQ - 1) // CHUNK + 1) * CHUNK, l_valid)
    nkt = (last_vis + TK - 1) // TK
    lane_pos = lax.broadcasted_iota(jnp.int32, (1, TK), 1)

    w_scale = (IDX_DIM ** -0.5) * (IDX_HEADS ** -0.5)
    for h in range(IDX_HEADS):
        w_scr[h] = jnp.broadcast_to(sm_ref[:, SM_IW + h:SM_IW + h + 1] * w_scale, (TQ, LANES))
    lane128 = lax.broadcasted_iota(jnp.int32, (1, LANES), 1)
    sc = min(TK, 2 * LANES)

    def score_tile(kt, carry):
        for cc in range(TK // sc):
            k0 = pl.multiple_of(kt * TK + cc * sc, sc)
            ik_c = ikx_ref[pl.ds(k0, sc), :]
            accs = [jnp.zeros((TQ, LANES), _F32) for _ in range(sc // LANES)]
            for h in range(IDX_HEADS):
                isc = _dot_nt(iqx_ref[:, h * IDX_XW:(h + 1) * IDX_XW], ik_c)
                w_h = w_scr[h]
                for c in range(sc // LANES):
                    accs[c] = accs[c] + jnp.maximum(isc[:, c * LANES:(c + 1) * LANES], 0.0) * w_h
            for c in range(sc // LANES):
                bits = pltpu.bitcast(accs[c] + 0.0, jnp.int32)
                key = jnp.where(bits < 0, bits ^ 0x7FFFFFFF, bits)
                kpos = k0 + c * LANES + lane128
                vis = ((kpos // CHUNK) <= qchunk) & (kpos < l_valid)
                col = cc * sc + c * LANES
                key_scr[kt, :, col:col + LANES] = jnp.where(vis, key, INT_MIN)
        return carry

    lax.fori_loop(0, nkt, score_tile, 0)

    def lane_fold(x):
        out = x[:, 0:LANES]
        for c in range(1, TK // LANES):
            out = out + x[:, c * LANES:(c + 1) * LANES]
        return out

    def count(pred_fn):
        def body(kt, part):
            return part + lane_fold(pred_fn(key_scr[kt], kt).astype(jnp.int32))
        part = lax.fori_loop(0, nkt, body, jnp.zeros((TQ, LANES), jnp.int32))
        return jnp.sum(part, axis=1, keepdims=True)

    def bit_step(it, carry):
        thr_u, n_ge = carry
        cand_u = thr_u | lax.shift_left(jnp.int32(1), 31 - it)
        cand_s = cand_u ^ INT_MIN
        cnt = count(lambda key, kt: key >= cand_s)
        keep = cnt >= topk
        return jnp.where(keep, cand_u, thr_u), jnp.where(keep, cnt, n_ge)

    zero = jnp.zeros((TQ, 1), jnp.int32)
    thr_u, n_ge = lax.fori_loop(0, 32, bit_step, (zero, zero))
    thr = thr_u ^ INT_MIN

    def tie_search(_):
        need = topk - count(lambda key, kt: key > thr)
        nbits = (key_scr.shape[0] * TK).bit_length()

        def pos_step(it, p):
            cand = p | lax.shift_left(jnp.int32(1), nbits - 1 - it)
            cnt = count(lambda key, kt: (key == thr) & ((kt * TK + lane_pos) < cand))
            return jnp.where(cnt < need, cand, p)

        return lax.fori_loop(0, nbits, pos_step, jnp.zeros((TQ, 1), jnp.int32))

    has_tie = jnp.max(n_ge.astype(_F32)) > topk
    p_last = lax.cond(has_tie, tie_search, lambda _: jnp.full((TQ, 1), 2 ** 31 - 1, jnp.int32), 0)

    def bias_tile(kt, carry):
        key = key_scr[kt]
        sel = ((key > thr) | ((key == thr) & ((kt * TK + lane_pos) <= p_last))) & (key > KEY_NEG_INF)
        bias_scr[kt] = jnp.where(sel, 0.0, NEG_BIG)
        return carry

    lax.fori_loop(0, nkt, bias_tile, 0)

    scale = (A_HD ** -0.5) * math.log2(math.e)
    rows = A_GROUP * TQ
    for g in range(A_KV_HEADS):
        qg = jnp.concatenate(
            [q_ref[:, (g * A_GROUP + j) * A_HD:(g * A_GROUP + j + 1) * A_HD] for j in range(A_GROUP)], axis=0)

        def attn_tile(kt, carry, g=g, qg=qg):
            m_i, l_i, acc = carry
            k0 = pl.multiple_of(kt * TK, TK)
            k_t = k_ref[pl.ds(k0, TK), g * A_HD:(g + 1) * A_HD]
            v_t = v_ref[pl.ds(k0, TK), g * A_HD:(g + 1) * A_HD]
            s = _dot_nt(qg, k_t).reshape(A_GROUP, TQ, TK) * scale + bias_scr[kt][None]
            m_new = jnp.maximum(m_i, jnp.max(s, axis=-1, keepdims=True))
            alpha = jnp.exp2(m_i - m_new)
            p = jnp.exp2(s - m_new)
            l_new = alpha * l_i + jnp.sum(p, axis=-1, keepdims=True)
            pv = _dot(p.reshape(rows, TK).astype(_BF16), v_t).reshape(A_GROUP, TQ, A_HD)
            return m_new, l_new, alpha * acc + pv

        init = (jnp.full((A_GROUP, TQ, 1), NEG_BIG, _F32), jnp.zeros((A_GROUP, TQ, 1), _F32),
                jnp.zeros((A_GROUP, TQ, A_HD), _F32))
        _, l_f, acc_f = lax.fori_loop(0, nkt, attn_tile, init)
        out = acc_f / l_f
        for j in range(A_GROUP):
            hh = g * A_GROUP + j
            o_ref[:, hh * A_HD:(hh + 1) * A_HD] = out[j].astype(o_ref.dtype)


def _attention(q_rot, iq_rot, small, k_all, v_all, ik_all, row0, nseq, T, TQ, TK, lp, past_len, l_valid, o_prev=None):
    nq = T // TQ
    qb0 = row0 // TQ
    topk = min(TOPK_MAX, l_valid // 4)
    wq, wk, wix = A_HEADS * A_HD, A_KV_HEADS * A_HD, IDX_HEADS * IDX_XW
    in_specs = [pl.BlockSpec((TQ, wq), lambda b, i: (qb0 + b * nq + i, 0)),
                pl.BlockSpec((TQ, wix), lambda b, i: (qb0 + b * nq + i, 0)),
                pl.BlockSpec((TQ, LANES), lambda b, i: (qb0 + b * nq + i, 0)),
                pl.BlockSpec((lp, wk), lambda b, i: (b, 0)),
                pl.BlockSpec((lp, wk), lambda b, i: (b, 0)),
                pl.BlockSpec((lp, IDX_XW), lambda b, i: (b, 0))]
    args = [q_rot, iq_rot, small, k_all, v_all, ik_all]
    aliases = {}
    if o_prev is not None:
        aliases = {len(args): 0}
        in_specs.append(pl.BlockSpec(memory_space=pl.ANY))
        args.append(o_prev)
    return pl.pallas_call(
        functools.partial(_attn_kernel, TQ=TQ, TK=TK, past_len=past_len, l_valid=l_valid, topk=topk),
        out_shape=jax.ShapeDtypeStruct((q_rot.shape[0], wq), _BF16),
        grid=(nseq, nq),
        in_specs=in_specs,
        out_specs=pl.BlockSpec((TQ, wq), lambda b, i: (qb0 + b * nq + i, 0)),
        scratch_shapes=[pltpu.VMEM((lp // TK, TQ, TK), jnp.int32),
                        pltpu.VMEM((lp // TK, TQ, TK), _F32),
                        pltpu.VMEM((IDX_HEADS, TQ, LANES), _F32)],
        input_output_aliases=aliases,
        compiler_params=_params(("parallel", "arbitrary")),
        name="sparse_attention",
    )(*args)


def _merge_kernel(hm_ref, ha_ref, wm_ref, wa_ref, gm_ref, ga_ref, o_ref):
    pm = _dot(hm_ref[...], wm_ref[...])
    pa = _dot(ha_ref[...], wa_ref[...])
    o_ref[...] = (jax.nn.sigmoid(gm_ref[...]) * pm + jax.nn.sigmoid(ga_ref[...]) * pa).astype(o_ref.dtype)


def _merge(hm, ha, wm, wa, proj, d):
    m = hm.shape[0]
    tm = _tile(m, MM_ROWS, SUBLANES)
    tn = _tile(d, MM_COLS)
    gm0 = 0
    ga0 = d // tn
    km, ka = hm.shape[1], ha.shape[1]
    return pl.pallas_call(
        _merge_kernel,
        out_shape=jax.ShapeDtypeStruct((m, d), _BF16),
        grid=(d // tn, m // tm),
        in_specs=[pl.BlockSpec((tm, km), lambda j, i: (i, 0)),
                  pl.BlockSpec((tm, ka), lambda j, i: (i, 0)),
                  pl.BlockSpec((km, tn), lambda j, i: (0, j)),
                  pl.BlockSpec((ka, tn), lambda j, i: (0, j)),
                  pl.BlockSpec((tm, tn), lambda j, i: (i, gm0 + j)),
                  pl.BlockSpec((tm, tn), lambda j, i: (i, ga0 + j))],
        out_specs=pl.BlockSpec((tm, tn), lambda j, i: (i, j)),
        compiler_params=_params(("parallel", "parallel")),
        name="gated_merge",
    )(hm, ha, wm, wa, proj, proj)


def _resnorm_next_kernel(xa_ref, xb_ref, y_ref, g_ref, g2_ref, x1_ref, h_ref, *, n_first):
    def body(x_ref):
        y = y_ref[...]
        ms = jnp.mean(y * y, axis=-1, keepdims=True)
        x1 = x_ref[...] + y * lax.rsqrt(ms + NORM_EPS) * g_ref[...]
        x1_ref[...] = x1
        ms1 = jnp.mean(x1 * x1, axis=-1, keepdims=True)
        h_ref[...] = (x1 * lax.rsqrt(ms1 + NORM_EPS) * g2_ref[...]).astype(h_ref.dtype)

    i = pl.program_id(0)
    pl.when(i < n_first)(lambda: body(xa_ref))
    pl.when(i >= n_first)(lambda: body(xb_ref))


def _resnorm_next(xa, xb, y, g, g_next):
    m, d = y.shape
    tm = _row_tile(xa.shape[0], xb.shape[0])
    n_first = xa.shape[0] // tm
    row = pl.BlockSpec((tm, d), lambda i: (i, 0))
    vec = pl.BlockSpec((1, d), lambda i: (0, 0))
    return pl.pallas_call(
        functools.partial(_resnorm_next_kernel, n_first=n_first),
        out_shape=(jax.ShapeDtypeStruct((m, d), _F32), jax.ShapeDtypeStruct((m, d), _BF16)),
        grid=(m // tm,), in_specs=[*_two_source_specs(tm, d, n_first), row, vec, vec], out_specs=(row, row),
        compiler_params=_params(("parallel",)), name="resnorm_next",
    )(xa, xb, y, g.reshape(1, d), g_next.reshape(1, d))


def _resnorm_kernel(x_ref, y_ref, g_ref, oa_ref, ob_ref, *, n_first):
    y = y_ref[...]
    ms = jnp.mean(y * y, axis=-1, keepdims=True)
    out = x_ref[...] + y * lax.rsqrt(ms + NORM_EPS) * g_ref[...]
    i = pl.program_id(0)

    @pl.when(i < n_first)
    def _():
        oa_ref[...] = out

    @pl.when(i >= n_first)
    def _():
        ob_ref[...] = out


def _resnorm_split(x, y, g, m_a):
    m, d = x.shape
    m_b = m - m_a
    tm = _row_tile(m_a, m_b)
    n_first = m_a // tm
    row = pl.BlockSpec((tm, d), lambda i: (i, 0))
    vec = pl.BlockSpec((1, d), lambda i: (0, 0))
    return pl.pallas_call(
        functools.partial(_resnorm_kernel, n_first=n_first),
        out_shape=(jax.ShapeDtypeStruct((m_a, d), _F32), jax.ShapeDtypeStruct((m_b, d), _F32)),
        grid=(m // tm,), in_specs=[row, row, vec], out_specs=_two_source_specs(tm, d, n_first),
        compiler_params=_params(("arbitrary",)), name="resnorm",
    )(x, y, g.reshape(1, d))


HALO = SUBLANES


def _conv_gelu_gate(cw_ref, cb_ref, prev2, prev1, gate, lin):
    gc = cw_ref[0:1, :] * prev2 + cw_ref[1:2, :] * prev1 + cw_ref[2:3, :] * gate + cb_ref[...]
    c = -2.0 * math.sqrt(2.0 / math.pi)
    act = gc / (1.0 + jnp.exp(gc * (c + (c * 0.044715) * (gc * gc))))
    return act * lin


def _ffn_up_first_kernel(h_ref, wg_ref, wl_ref, cw_ref, cb_ref, z_ref, tail_ref, wg_bf, wl_bf, g_scr,
                         *, tm, seq_len):
    i = pl.program_id(1)
    tn = z_ref.shape[1]

    @pl.when(i == 0)
    def _():
        wg_bf[...] = wg_ref[...].astype(wg_bf.dtype)
        wl_bf[...] = wl_ref[...].astype(wl_bf.dtype)

    @pl.when((i * tm) % seq_len == 0)
    def _():
        g_scr[0:HALO, :] = jnp.zeros((HALO, tn), _F32)

    h = h_ref[...]
    gate = _dot(h, wg_bf[...])
    lin = _dot(h, wl_bf[...])
    head = g_scr[0:HALO, :]
    row = lax.broadcasted_iota(jnp.int32, (SUBLANES, 1), 0)

    def shifted(k):
        rolled = pltpu.roll(gate, k, 0)
        first = jnp.where(row < k, pltpu.roll(head, k, 0), rolled[0:SUBLANES, :])
        return jnp.concatenate([first, rolled[SUBLANES:, :]], axis=0)

    zed = _conv_gelu_gate(cw_ref, cb_ref, shifted(2), shifted(1), gate, lin)
    z_ref[...] = zed.astype(z_ref.dtype)
    g_scr[0:HALO, :] = gate[tm - HALO:tm, :]
    tail_ref[...] = gate[tm - SUBLANES:tm, :]


def _ffn_up_state_kernel(h_ref, wg_ref, wl_ref, cw_ref, cb_ref, init_ref, z_any, z_ref, tail_ref, *, tm, seq_len):
    tn = z_ref.shape[1]
    h = h_ref[...]
    gate = _dot(h, wg_ref[...].astype(_BF16))
    lin = _dot(h, wl_ref[...].astype(_BF16))
    nseg = tm // seq_len
    init0 = jnp.concatenate([jnp.broadcast_to(init_ref[s, 0:1, :], (seq_len, tn)) for s in range(nseg)], axis=0)
    init1 = jnp.concatenate([jnp.broadcast_to(init_ref[s, 1:2, :], (seq_len, tn)) for s in range(nseg)], axis=0)
    t = lax.broadcasted_iota(jnp.int32, (tm, 1), 0) % seq_len
    prev1 = jnp.where(t == 0, init1, pltpu.roll(gate, 1, 0))
    prev2 = jnp.where(t == 0, init0, jnp.where(t == 1, init1, pltpu.roll(gate, 2, 0)))
    z_ref[...] = _conv_gelu_gate(cw_ref, cb_ref, prev2, prev1, gate, lin).astype(z_ref.dtype)
    for s in range(nseg):
        tail_ref[s * SUBLANES:(s + 1) * SUBLANES, :] = gate[(s + 1) * seq_len - SUBLANES:(s + 1) * seq_len, :]


def _conv_state(tails, nseq, segs_per_seq, dff):
    tails = tails.reshape(nseq * segs_per_seq, SUBLANES, dff)
    last = tails[segs_per_seq - 1::segs_per_seq, SUBLANES - (CONV_W - 1):, :]
    return last.reshape(nseq, CONV_W - 1, dff)


def _ffn_up_specs(d, tn, nj, h_spec):
    return [h_spec,
            pl.BlockSpec((d, tn), lambda j, i: (0, j)),
            pl.BlockSpec((d, tn), lambda j, i: (0, nj + j)),
            pl.BlockSpec((CONV_W, tn), lambda j, i: (0, j)),
            pl.BlockSpec((1, tn), lambda j, i: (0, j))]


def _ffn_up_first(hf, w_up, conv_w, conv_b, nrows, seq_len):
    m_all, d = hf.shape
    dff = w_up.shape[1] // 2
    tm = _tile(seq_len, FFN_ROWS, SUBLANES)
    tn = _tile(dff, FFN_COLS)
    nj = dff // tn
    nt = nrows // tm
    specs = _ffn_up_specs(d, tn, nj, pl.BlockSpec((tm, d), lambda j, i: (i, 0)))
    z, tails = pl.pallas_call(
        functools.partial(_ffn_up_first_kernel, tm=tm, seq_len=seq_len),
        out_shape=(jax.ShapeDtypeStruct((m_all, dff), _BF16),
                   jax.ShapeDtypeStruct((nt * SUBLANES, dff), _F32)),
        grid=(nj, nt),
        in_specs=specs,
        out_specs=(pl.BlockSpec((tm, tn), lambda j, i: (i, j)),
                   pl.BlockSpec((SUBLANES, tn), lambda j, i: (i, j))),
        scratch_shapes=[pltpu.VMEM((d, tn), _BF16), pltpu.VMEM((d, tn), _BF16),
                        pltpu.VMEM((HALO, tn), _F32)],
        compiler_params=_params(("parallel", "arbitrary")),
        name="ffn_up_conv_first",
    )(hf, w_up, w_up, conv_w, conv_b.reshape(1, dff))
    return z, _conv_state(tails, nrows // seq_len, seq_len // tm, dff)


def _ffn_up_state(hf, w_up, conv_w, conv_b, row0, nrows, seq_len, init, z_prev):
    m_all, d = hf.shape
    dff = w_up.shape[1] // 2
    tm = _tile(nrows, FFN_STATE_ROWS, seq_len)
    tn = _tile(dff, FFN_COLS)
    nj = dff // tn
    nseg = tm // seq_len
    rb0 = row0 // tm
    specs = _ffn_up_specs(d, tn, nj, pl.BlockSpec((tm, d), lambda j, i: (rb0 + i, 0)))
    specs += [pl.BlockSpec((nseg, CONV_W - 1, tn), lambda j, i: (i, 0, j)), pl.BlockSpec(memory_space=pl.ANY)]
    z, tails = pl.pallas_call(
        functools.partial(_ffn_up_state_kernel, tm=tm, seq_len=seq_len),
        out_shape=(jax.ShapeDtypeStruct((m_all, dff), _BF16),
                   jax.ShapeDtypeStruct((nrows // seq_len * SUBLANES, dff), _F32)),
        grid=(nj, nrows // tm),
        in_specs=specs,
        out_specs=(pl.BlockSpec((tm, tn), lambda j, i: (rb0 + i, j)),
                   pl.BlockSpec((nseg * SUBLANES, tn), lambda j, i: (i, j))),
        input_output_aliases={6: 0},
        compiler_params=_params(("parallel", "arbitrary")),
        name="ffn_up_conv_state",
    )(hf, w_up, w_up, conv_w, conv_b.reshape(1, dff), init, z_prev)
    return z, _conv_state(tails, nrows // seq_len, 1, dff)


def _layer(xa, xb, geom, cache, state, w):
    (B, S, Bd, Td, P) = geom
    (cache_k, cache_v, cache_ik) = cache
    (state_C, state_n, state_m, state_conv) = state
    (w_in, b_igate, b_fgate, g_mhnorm, w_proj_m, w_proj_a, w_out,
     g_pre_mix, g_post_mix, g_pre_ffn, g_post_ffn, w_up, conv_w, conv_b, w_down) = w
    mp, d = xa.shape
    dff = w_down.shape[0]
    wk = A_KV_HEADS * A_HD

    c_mi = A_END
    c_aq = c_mi + 2 * M_HEADS
    c_ik = c_aq + B_END
    c_iw = c_ik + IDX_DIM
    c_gm = c_iw + IDX_HEADS
    w_t = jnp.swapaxes(w_in, 0, 1).astype(_F32)
    w_small_t = jnp.concatenate([w_t[c_ik:c_iw], w_t[c_mi:c_aq], w_t[c_iw:c_gm],
                                 jnp.zeros((LANES - SM_END, d), _F32)], axis=0)
    gate_bias = jnp.concatenate([jnp.zeros((SM_MI,), _F32), b_igate.astype(_F32), b_fgate.astype(_F32),
                                 jnp.zeros((LANES - SM_IW,), _F32)]).reshape(1, LANES)

    hn = _rmsnorm_cast(xa, xb, g_pre_mix)
    proj_a = _matmul_w32(hn, w_t, 0, A_END, _F32, transposed=True, name="in_proj_mlstm")
    proj_b = _matmul_w32(hn, w_t, c_aq, B_END, _F32, transposed=True, name="in_proj_attn")
    proj_c = _matmul_w32(hn, w_t, c_gm, 2 * d, _F32, transposed=True, name="in_proj_gates")
    small = _matmul_w32(hn, w_small_t, 0, LANES, _F32, transposed=True, name="in_proj_small")

    lp_chunk = _tile(S, MLSTM_CHUNK, CHUNK)
    hm_p, c_p, n_p, m_p = _mlstm(proj_a, small, gate_bias, g_mhnorm, 0, B, S, lp_chunk, None)
    s0 = jnp.zeros((Bd * M_HEADS, SUBLANES, M_QK), _F32)
    s0 = s0.at[:, 0, :].set(state_n.reshape(Bd * M_HEADS, M_QK).astype(_F32))
    s0 = s0.at[:, 1, :].set(jnp.broadcast_to(state_m.reshape(Bd * M_HEADS, 1).astype(_F32), (Bd * M_HEADS, M_QK)))
    hm, c_s, n_s, m_s = _mlstm(proj_a, small, gate_bias, g_mhnorm, mp, Bd, Td, min(CHUNK, Td),
                               (state_C.astype(_F32), s0), hm_p)

    pos = jnp.concatenate([jnp.tile(jnp.arange(S, dtype=jnp.int32), B),
                           jnp.tile(P + jnp.arange(Td, dtype=jnp.int32), Bd)])
    q_rot, k_rot, k_bf, v_bf, iqx, ik_rot, ikx = _rope_all(proj_b, small, pos)
    tq_p = _tile(S, ATTN_TQ, CHUNK)
    tk_p = _tile(S, ATTN_TK)
    ha_p = _attention(q_rot, iqx, small, k_bf, v_bf, ikx, 0, B, S, tq_p, tk_p, S, 0, S)
    l_s = P + Td
    tk_s = ATTN_TK_STATE
    lp_s = -(-l_s // tk_s) * tk_s
    pad = lp_s - l_s
    k_s = jnp.concatenate([cache_k.reshape(Bd, P, wk).astype(_BF16), k_bf[mp:].reshape(Bd, Td, wk),
                           jnp.zeros((Bd, pad, wk), _BF16)], axis=1).reshape(Bd * lp_s, wk)
    v_s = jnp.concatenate([cache_v.reshape(Bd, P, wk).astype(_BF16), v_bf[mp:].reshape(Bd, Td, wk),
                           jnp.zeros((Bd, pad, wk), _BF16)], axis=1).reshape(Bd * lp_s, wk)
    cik = cache_ik.astype(_F32)
    cik_hi = cik.astype(_BF16)
    cik_lo = (cik - cik_hi.astype(_F32)).astype(_BF16)
    cikx = jnp.concatenate([cik_hi, cik_lo, cik_hi, jnp.zeros_like(cik_hi)], axis=-1)
    ikx_s = jnp.concatenate([cikx, ikx[mp:].reshape(Bd, Td, IDX_XW),
                             jnp.zeros((Bd, pad, IDX_XW), _BF16)], axis=1).reshape(Bd * lp_s, IDX_XW)
    ha = _attention(q_rot, iqx, small, k_s, v_s, ikx_s, mp, Bd, Td, Td, tk_s, lp_s, P, l_s, ha_p)

    mix = _merge(hm, ha, w_proj_m.astype(_BF16), w_proj_a.astype(_BF16), proj_c, d)
    y1 = _matmul_w32(mix, w_out.astype(_F32), 0, d, _F32, transposed=False, name="out_proj")
    x1, hf = _resnorm_next(xa, xb, y1, g_post_mix, g_pre_ffn)

    w_up = w_up.astype(_F32)
    z_p, conv_p = _ffn_up_first(hf, w_up, conv_w, conv_b, mp, S)
    z, conv_s = _ffn_up_state(hf, w_up, conv_w, conv_b, mp, Bd * Td, Td, state_conv.astype(_F32), z_p)
    y2 = _matmul(z, w_down.astype(_BF16), _F32, name="down_proj")
    x2 = _resnorm_split(x1, y2, g_post_ffn, mp)

    av = proj_b[:, B_AV:B_AV + wk]
    outs_p = (k_rot[:mp].reshape(B, S, A_KV_HEADS, A_HD), av[:mp].reshape(B, S, A_KV_HEADS, A_HD),
              ik_rot[:mp].reshape(B, S, IDX_DIM), c_p, n_p, m_p, conv_p)
    outs_s = (k_rot[mp:].reshape(Bd, Td, A_KV_HEADS, A_HD), av[mp:].reshape(Bd, Td, A_KV_HEADS, A_HD),
              ik_rot[mp:].reshape(Bd, Td, IDX_DIM), c_s, n_s, m_s, conv_s)
    return x2, outs_p, outs_s


def kernel(x_prompt, x_sample, cache_k, cache_v, cache_idx_k, state_C, state_n, state_m, state_conv,
           w_in, b_igate, b_fgate, g_mhnorm, w_proj_m, w_proj_a, w_out,
           g_pre_mix, g_post_mix, g_pre_ffn, g_post_ffn, w_up, conv_w, conv_b, w_down):
    B, S, d = x_prompt.shape
    Bd, Td, _ = x_sample.shape
    P = cache_k.shape[2]
    depth = w_in.shape[0]
    mp = B * S
    xa, xb = x_prompt.reshape(mp, d), x_sample.reshape(Bd * Td, d)
    all_p, all_s = [], []
    for l in range(depth):
        w = (w_in[l], b_igate[l], b_fgate[l], g_mhnorm[l], w_proj_m[l], w_proj_a[l], w_out[l],
             g_pre_mix[l], g_post_mix[l], g_pre_ffn[l], g_post_ffn[l], w_up[l], conv_w[l], conv_b[l], w_down[l])
        (xa, xb), outs_p, outs_s = _layer(xa, xb, (B, S, Bd, Td, P), (cache_k[l], cache_v[l], cache_idx_k[l]),
                                          (state_C[l], state_n[l], state_m[l], state_conv[l]), w)
        all_p.append(outs_p)
        all_s.append(outs_s)

    def stk(outs, i):
        return jnp.stack([o[i] for o in outs])

    yp = xa.reshape(B, S, d)
    ys = xb.reshape(Bd, Td, d)
    return (yp, ys) + tuple(stk(all_p, i) for i in range(7)) + tuple(stk(all_s, i) for i in range(7))
```
